```python
import jax, jax.numpy as jnp
from jax import lax
import numpy as np

D_MODEL = 1024
BATCH = 2
SEQ = 8192
DEPTH = 4
DEC_BATCH = 32
DEC_SEQ = 4
PAST_LEN = 8192
PAGE_SIZE = 128

LRU_W = D_MODEL
LRU_HEADS = 8
LRU_BW = LRU_W // LRU_HEADS
CONV_W = 4
LRU_C = 8.0
NSA_HEADS = 16
HEAD_DIM = 64
NSA_KV_GROUPS = 4
NSA_REP = NSA_HEADS // NSA_KV_GROUPS
NSA_Q_W = NSA_HEADS * HEAD_DIM
NSA_KV_W = NSA_KV_GROUPS * HEAD_DIM
NSA_BLOCK = 64
NSA_TOPN = 8
NSA_WINDOW = 512
Q_BLOCK = 128
SEL_FORCE = 1.0e4
GM_W = D_MODEL
GM_GROUPS = 8
GM_GC = GM_W // GM_GROUPS
GM_CHUNK = 128
N_BRANCH = 3
IN_SPLITS = (LRU_W, LRU_W, NSA_Q_W, NSA_KV_W, NSA_KV_W, NSA_KV_W, NSA_KV_W, NSA_KV_W, NSA_KV_W, 3 * NSA_HEADS, 2 * GM_W, N_BRANCH * D_MODEL)
N_IN = sum(IN_SPLITS)
FFN_DENSE = 2816
N_EXPERTS = 8
TOP_K = 2
FFN_EXPERT = 2816
N_DENSE_LAYERS = (DEPTH + 1) // 2
N_MOE_LAYERS = DEPTH // 2
RMS_EPS = 1e-6
NEG_INF = -1e30

kernel_name = 'hybrid_lru_nsa_gmlp_step'


def rmsnorm(x, g):
    x32 = x.astype(jnp.float32)
    y = x32 * lax.rsqrt(jnp.mean(x32 * x32, axis=-1, keepdims=True) + RMS_EPS)
    return y.astype(x.dtype) * g


def masked_softmax(s, mask):
    s = jnp.where(mask, s.astype(jnp.float32), NEG_INF)
    e = jnp.exp(s - jnp.max(s, axis=-1, keepdims=True)) * mask
    return e / jnp.maximum(jnp.sum(e, axis=-1, keepdims=True), 1e-30)


def split_proj(z):
    cuts = np.cumsum(IN_SPLITS)[:-1].tolist()
    return jnp.split(z, cuts, axis=-1)


def causal_conv(x, state, w, b):
    t = x.shape[1]
    xp = jnp.concatenate([state, x], axis=1)
    y = b + xp[:, 0:t] * w[0]
    for k in range(1, CONV_W):
        y = y + xp[:, k:k + t] * w[k]
    return y, xp[:, t:]


def rg_lru(x, h0, pos, wa, ba, wx, bx, lam):
    b, t, w = x.shape
    xb = x.reshape(b, t, LRU_HEADS, LRU_BW)
    r = jax.nn.sigmoid(jnp.einsum('bthi,hij->bthj', xb, wa).reshape(b, t, w) + ba)
    i = jax.nn.sigmoid(jnp.einsum('bthi,hij->bthj', xb, wx).reshape(b, t, w) + bx)
    log_a = -LRU_C * r.astype(jnp.float32) * jax.nn.softplus(-lam.astype(jnp.float32))
    mult = jnp.sqrt(-jnp.expm1(2.0 * log_a))
    mult = jnp.where((pos == 0)[None, :, None], 1.0, mult)
    u = mult * (i * x).astype(jnp.float32)

    def step(h, au):
        a_t, u_t = au
        h = a_t * h + u_t
        return h, h

    h_last, hs = lax.scan(step, h0.astype(jnp.float32), (jnp.swapaxes(jnp.exp(log_a), 0, 1), jnp.swapaxes(u, 0, 1)))
    return jnp.swapaxes(hs, 0, 1).astype(x.dtype), h_last.astype(x.dtype)


def spatial_gate(uv, g, ws, bs):
    u, v = jnp.split(jax.nn.gelu(uv), 2, axis=-1)
    v = rmsnorm(v, g)
    b, l = v.shape[:2]
    n = min(l, GM_CHUNK)
    wsm = ws[:, :n, :n] * jnp.tril(jnp.ones((n, n), ws.dtype))
    vb = v.reshape(b, l // n, n, GM_GROUPS, GM_GC)
    mixed = jnp.einsum('gts,bnsgc->bntgc', wsm, vb) + bs[:, :n].T[:, :, None]
    return u * mixed.reshape(b, l, GM_W), v


def compress(rows, w):
    b, l = rows.shape[:2]
    rb = rows.reshape(b, l // NSA_BLOCK, NSA_BLOCK, NSA_KV_GROUPS, HEAD_DIM)
    return jnp.einsum('bnlgd,lgd->bngd', rb, w)


def nsa_core(q, q_pos, kc, vc, fetch, n_blocks, kw, vw, kw_pos, gates):
    b, tq = q.shape[:2]
    scale = HEAD_DIM ** -0.5
    t = q_pos[:, None]
    n_cmp = kc.shape[1]
    s_c = jnp.einsum('btgrd,bngd->btgrn', q, kc) * scale
    ok_c = ((jnp.arange(n_cmp) + 1) * NSA_BLOCK - 1)[None, :] <= t
    p_c = masked_softmax(s_c, ok_c[None, :, None, None, :])
    o_c = jnp.einsum('btgrn,bngd->btgrd', p_c.astype(vc.dtype), vc)
    imp = jnp.pad(jnp.sum(p_c, axis=3), ((0, 0), (0, 0), (0, 0), (0, n_blocks - n_cmp)))
    blk = jnp.arange(n_blocks)
    cur = (q_pos // NSA_BLOCK)[None, :, None, None]
    score = jnp.where(blk == cur, SEL_FORCE, jnp.where(blk < cur, imp, -1.0))
    _, idx = lax.top_k(score, min(NSA_TOPN, n_blocks))
    n_sel = idx.shape[-1]
    kg, vg = fetch(idx)
    s_s = jnp.einsum('btgrd,btgkld->btgrkl', q, kg) * scale
    tok = idx[..., None] * NSA_BLOCK + jnp.arange(NSA_BLOCK)
    ok_s = (tok <= q_pos[None, :, None, None, None])[:, :, :, None]
    p_s = masked_softmax(s_s.reshape(b, tq, NSA_KV_GROUPS, NSA_REP, n_sel * NSA_BLOCK),
                         ok_s.reshape(b, tq, NSA_KV_GROUPS, 1, n_sel * NSA_BLOCK))
    o_s = jnp.einsum('btgrm,btgmd->btgrd', p_s.astype(vg.dtype),
                     vg.reshape(b, tq, NSA_KV_GROUPS, n_sel * NSA_BLOCK, HEAD_DIM))
    s_w = jnp.einsum('btgrd,blgd->btgrl', q, kw) * scale
    kp = kw_pos[None, :]
    ok_w = (kp <= t) & (kp > t - NSA_WINDOW) & (kp >= 0)
    p_w = masked_softmax(s_w, ok_w[None, :, None, None, :])
    o_w = jnp.einsum('btgrl,blgd->btgrd', p_w.astype(vw.dtype), vw)
    o = gates[..., 0:1] * o_c + gates[..., 1:2] * o_s + gates[..., 2:3] * o_w
    return o.reshape(b, tq, NSA_Q_W)


def nsa_prompt(q, k_c, v_c, k_s, v_s, k_w, v_w, gates, wck, wcv):
    b, s = q.shape[:2]
    kc, vc = compress(k_c, wck), compress(v_c, wcv)
    n_blocks = s // NSA_BLOCK
    ks_b = k_s.reshape(b, n_blocks, NSA_BLOCK, NSA_KV_GROUPS, HEAD_DIM)
    vs_b = v_s.reshape(b, n_blocks, NSA_BLOCK, NSA_KV_GROUPS, HEAD_DIM)
    b_ix = jnp.arange(b)[:, None, None, None]
    g_ix = jnp.arange(NSA_KV_GROUPS)[None, None, :, None]

    def fetch(idx):
        return ks_b[b_ix, idx, :, g_ix, :], vs_b[b_ix, idx, :, g_ix, :]

    pad = ((0, 0), (NSA_WINDOW, 0), (0, 0), (0, 0))
    kw_p, vw_p = jnp.pad(k_w, pad), jnp.pad(v_w, pad)
    band = NSA_WINDOW + Q_BLOCK

    def one_block(c):
        start = c * Q_BLOCK
        sl = lambda a, n: lax.dynamic_slice_in_dim(a, start, n, axis=1)
        return nsa_core(sl(q, Q_BLOCK), start + jnp.arange(Q_BLOCK), kc, vc, fetch, n_blocks,
                        sl(kw_p, band), sl(vw_p, band), start - NSA_WINDOW + jnp.arange(band), sl(gates, Q_BLOCK))

    out = lax.map(one_block, jnp.arange(s // Q_BLOCK))
    return jnp.swapaxes(out, 0, 1).reshape(b, s, NSA_Q_W)


def nsa_sample(q, k_c, v_c, k_s, v_s, k_w, v_w, gates, wck, wcv,
               pool_ck, pool_cv, pool_sk, pool_sv, win_k, win_v, page_table):
    b, t = q.shape[:2]
    page = pool_ck.shape[1]
    past = page_table.shape[1] * page
    q_pos = past + jnp.arange(t)
    gather = lambda pool: pool[page_table].reshape(b, past, NSA_KV_GROUPS, HEAD_DIM)
    full_k = jnp.concatenate([gather(pool_ck), k_c], axis=1)
    full_v = jnp.concatenate([gather(pool_cv), v_c], axis=1)
    n_cmp = (past + t) // NSA_BLOCK
    kc = compress(full_k[:, :n_cmp * NSA_BLOCK], wck)
    vc = compress(full_v[:, :n_cmp * NSA_BLOCK], wcv)
    bpp = page // NSA_BLOCK
    nb_past = past // NSA_BLOCK
    n_ext = -(-t // NSA_BLOCK)
    ext_pad = ((0, 0), (0, n_ext * NSA_BLOCK - t), (0, 0), (0, 0))
    ext_k = jnp.pad(k_s, ext_pad).reshape(b, n_ext, NSA_BLOCK, NSA_KV_GROUPS, HEAD_DIM)
    ext_v = jnp.pad(v_s, ext_pad).reshape(b, n_ext, NSA_BLOCK, NSA_KV_GROUPS, HEAD_DIM)
    pk = pool_sk.reshape(pool_sk.shape[0], bpp, NSA_BLOCK, NSA_KV_GROUPS, HEAD_DIM)
    pv = pool_sv.reshape(pool_sv.shape[0], bpp, NSA_BLOCK, NSA_KV_GROUPS, HEAD_DIM)
    b_ix = jnp.arange(b)[:, None, None, None]
    g_ix = jnp.arange(NSA_KV_GROUPS)[None, None, :, None]

    def fetch(idx):
        pidx = jnp.minimum(idx, nb_past - 1)
        phys = page_table[b_ix, pidx // bpp]
        sub = pidx % bpp
        eidx = jnp.clip(idx - nb_past, 0, n_ext - 1)
        is_past = (idx < nb_past)[..., None, None]
        kg = jnp.where(is_past, pk[phys, sub, :, g_ix, :], ext_k[b_ix, eidx, :, g_ix, :])
        vg = jnp.where(is_past, pv[phys, sub, :, g_ix, :], ext_v[b_ix, eidx, :, g_ix, :])
        return kg, vg

    wb = win_k.shape[1]
    kw = jnp.concatenate([win_k, k_w], axis=1)
    vw = jnp.concatenate([win_v, v_w], axis=1)
    kw_pos = past - wb + jnp.arange(wb + t)
    o = nsa_core(q, q_pos, kc, vc, fetch, nb_past + n_ext, kw, vw, kw_pos, gates)
    keep = min(NSA_WINDOW, wb + t)
    return o, kw[:, wb + t - keep:], vw[:, wb + t - keep:]


def mixer_front(xn, lp, conv_state, h0, pos):
    b, t = xn.shape[:2]
    (lru_x, lru_gate, q, k_c, v_c, k_s, v_s, k_w, v_w, nsa_g, gm_uv, mg) = split_proj(xn @ lp['w_in'])
    xc, conv_new = causal_conv(lru_x, conv_state, lp['conv_w'], lp['conv_b'])
    h, h_last = rg_lru(xc, h0, pos, lp['wa'], lp['ba'], lp['wx'], lp['bx'], lp['lam'])
    o_a = jax.nn.gelu(lru_gate) * h
    o_c, gm_v = spatial_gate(gm_uv, lp['gm_g'], lp['gm_ws'], lp['gm_bs'])
    kv = lambda a: a.reshape(b, t, NSA_KV_GROUPS, HEAD_DIM)
    qh = q.reshape(b, t, NSA_KV_GROUPS, NSA_REP, HEAD_DIM)
    gates = jax.nn.sigmoid(nsa_g).reshape(b, t, NSA_KV_GROUPS, NSA_REP, 3)
    nsa_in = (qh, kv(k_c), kv(v_c), kv(k_s), kv(v_s), kv(k_w), kv(v_w), gates)
    return o_a, o_c, nsa_in, mg, conv_new, h_last, gm_v


def mixer_merge(o_a, o_b, o_c, mg, lp):
    g = jax.nn.sigmoid(mg).reshape(mg.shape[:-1] + (N_BRANCH, D_MODEL))
    wb = lp['w_branch']
    y = g[..., 0, :] * (o_a @ wb[0]) + g[..., 1, :] * (o_b @ wb[1]) + g[..., 2, :] * (o_c @ wb[2])
    return y @ lp['w_out']


def swiglu(x, w1, w3, w2):
    return (jax.nn.silu(x @ w1) * (x @ w3)) @ w2


def moe_swiglu(x, wr, br, w1, w3, w2):
    logits = (x @ wr + br).astype(jnp.float32)
    top_v, top_i = lax.top_k(logits, TOP_K)
    gate = jax.nn.softmax(top_v, axis=-1)
    comb = jnp.sum(jax.nn.one_hot(top_i, N_EXPERTS, dtype=jnp.float32) * gate[..., None], axis=-2).astype(x.dtype)
    y = comb[..., 0:1] * swiglu(x, w1[0], w3[0], w2[0])
    for e in range(1, N_EXPERTS):
        y = y + comb[..., e:e + 1] * swiglu(x, w1[e], w3[e], w2[e])
    return y


def setup_inputs(seed: int = 0) -> dict:
    key = jax.random.key(seed)
    ks = iter(jax.random.split(key, 64))
    f32 = jnp.float32

    def nrm(shape, scale=1.0):
        return jax.random.normal(next(ks), shape, f32) * scale

    def gain(shape):
        return 1.0 + nrm(shape, 0.1)

    n_pages = PAST_LEN // PAGE_SIZE
    n_pool = (DEC_BATCH * n_pages * 5) // 4
    win_buf = min(NSA_WINDOW, PAST_LEN)
    pool_shape = (DEPTH, n_pool, PAGE_SIZE, NSA_KV_GROUPS, HEAD_DIM)
    win_shape = (DEPTH, DEC_BATCH, win_buf, NSA_KV_GROUPS, HEAD_DIM)
    page_table = jax.random.permutation(next(ks), n_pool)[:DEC_BATCH * n_pages].reshape(DEC_BATCH, n_pages).astype(jnp.int32)
    a0 = jax.random.uniform(next(ks), (DEPTH, LRU_W), f32, 0.9, 0.999) ** (1.0 / LRU_C)
    lam = jnp.log(a0) - jnp.log1p(-a0)
    return {
        'x_prompt': nrm((BATCH, SEQ, D_MODEL)),
        'x_sample': nrm((DEC_BATCH, DEC_SEQ, D_MODEL)),
        'cache_cmp_k': nrm(pool_shape),
        'cache_cmp_v': nrm(pool_shape),
        'cache_slc_k': nrm(pool_shape),
        'cache_slc_v': nrm(pool_shape),
        'cache_win_k': nrm(win_shape),
        'cache_win_v': nrm(win_shape),
        'state_conv': nrm((DEPTH, DEC_BATCH, CONV_W - 1, LRU_W)),
        'state_lru': nrm((DEPTH, DEC_BATCH, LRU_W), 0.5),
        'page_table': page_table,
        'norm_mix_g': gain((DEPTH, D_MODEL)),
        'w_in': nrm((DEPTH, D_MODEL, N_IN), D_MODEL ** -0.5),
        'lru_conv_w': nrm((DEPTH, CONV_W, LRU_W), CONV_W ** -0.5),
        'lru_conv_b': nrm((DEPTH, LRU_W), 0.02),
        'lru_wa': nrm((DEPTH, LRU_HEADS, LRU_BW, LRU_BW), LRU_BW ** -0.5),
        'lru_ba': nrm((DEPTH, LRU_W), 0.1),
        'lru_wx': nrm((DEPTH, LRU_HEADS, LRU_BW, LRU_BW), LRU_BW ** -0.5),
        'lru_bx': nrm((DEPTH, LRU_W), 0.1),
        'lru_lambda': lam,
        'nsa_cmp_wk': (1.0 + nrm((DEPTH, NSA_BLOCK, NSA_KV_GROUPS, HEAD_DIM), 0.1)) * NSA_BLOCK ** -0.5,
        'nsa_cmp_wv': (1.0 + nrm((DEPTH, NSA_BLOCK, NSA_KV_GROUPS, HEAD_DIM), 0.1)) * NSA_BLOCK ** -0.5,
        'gm_norm_g': gain((DEPTH, GM_W)),
        'gm_ws': nrm((DEPTH, GM_GROUPS, GM_CHUNK, GM_CHUNK), GM_CHUNK ** -0.5),
        'gm_bs': gain((DEPTH, GM_GROUPS, GM_CHUNK)),
        'w_branch': nrm((DEPTH, N_BRANCH, D_MODEL, D_MODEL), D_MODEL ** -0.5),
        'w_out': nrm((DEPTH, D_MODEL, D_MODEL), D_MODEL ** -0.5),
        'norm_ffn_g': gain((DEPTH, D_MODEL)),
        'ffn_w1': nrm((N_DENSE_LAYERS, D_MODEL, FFN_DENSE), D_MODEL ** -0.5),
        'ffn_w3': nrm((N_DENSE_LAYERS, D_MODEL, FFN_DENSE), D_MODEL ** -0.5),
        'ffn_w2': nrm((N_DENSE_LAYERS, FFN_DENSE, D_MODEL), FFN_DENSE ** -0.5),
        'moe_wr': nrm((N_MOE_LAYERS, D_MODEL, N_EXPERTS), D_MODEL ** -0.5),
        'moe_br': nrm((N_MOE_LAYERS, N_EXPERTS), 0.01),
        'moe_w1': nrm((N_MOE_LAYERS, N_EXPERTS, D_MODEL, FFN_EXPERT), D_MODEL ** -0.5),
        'moe_w3': nrm((N_MOE_LAYERS, N_EXPERTS, D_MODEL, FFN_EXPERT), D_MODEL ** -0.5),
        'moe_w2': nrm((N_MOE_LAYERS, N_EXPERTS, FFN_EXPERT, D_MODEL), FFN_EXPERT ** -0.5),
        'final_norm_g': gain((D_MODEL,)),
    }


def reference(x_prompt, x_sample, cache_cmp_k, cache_cmp_v, cache_slc_k, cache_slc_v, cache_win_k, cache_win_v,
              state_conv, state_lru, page_table, norm_mix_g, w_in, lru_conv_w, lru_conv_b, lru_wa, lru_ba,
              lru_wx, lru_bx, lru_lambda, nsa_cmp_wk, nsa_cmp_wv, gm_norm_g, gm_ws, gm_bs, w_branch, w_out,
              norm_ffn_g, ffn_w1, ffn_w3, ffn_w2, moe_wr, moe_br, moe_w1, moe_w3, moe_w2, final_norm_g):
    xp, xs = x_prompt, x_sample
    bp, seq = xp.shape[:2]
    tn = xs.shape[1]
    past = page_table.shape[1] * cache_cmp_k.shape[2]
    pos_p = jnp.arange(seq)
    pos_s = past + jnp.arange(tn)
    keep_p = min(NSA_WINDOW, seq)

    def channel_mixer(x, l):
        j = l // 2
        if l % 2 == 0:
            return swiglu(x, ffn_w1[j], ffn_w3[j], ffn_w2[j])
        return moe_swiglu(x, moe_wr[j], moe_br[j], moe_w1[j], moe_w3[j], moe_w2[j])

    p_st = [[] for _ in range(8)]
    s_st = [[] for _ in range(9)]
    for l in range(DEPTH):
        lp = dict(w_in=w_in[l], conv_w=lru_conv_w[l], conv_b=lru_conv_b[l], wa=lru_wa[l], ba=lru_ba[l],
                  wx=lru_wx[l], bx=lru_bx[l], lam=lru_lambda[l], gm_g=gm_norm_g[l], gm_ws=gm_ws[l],
                  gm_bs=gm_bs[l], w_branch=w_branch[l], w_out=w_out[l])
        xn = rmsnorm(xp, norm_mix_g[l])
        o_a, o_c, nin, mg, conv_new, h_last, _ = mixer_front(
            xn, lp, jnp.zeros((bp, CONV_W - 1, LRU_W), xn.dtype), jnp.zeros((bp, LRU_W), xn.dtype), pos_p)
        o_b = nsa_prompt(*nin, nsa_cmp_wk[l], nsa_cmp_wv[l])
        xp = xp + mixer_merge(o_a, o_b, o_c, mg, lp)
        xp = xp + channel_mixer(rmsnorm(xp, norm_ffn_g[l]), l)
        _, k_c, v_c, k_s, v_s, k_w, v_w, _ = nin
        for lst, a in zip(p_st, (k_c, v_c, k_s, v_s, k_w[:, seq - keep_p:], v_w[:, seq - keep_p:], conv_new, h_last)):
            lst.append(a)
        xn = rmsnorm(xs, norm_mix_g[l])
        o_a, o_c, nin, mg, conv_new, h_last, gm_v = mixer_front(xn, lp, state_conv[l], state_lru[l], pos_s)
        o_b, win_k, win_v = nsa_sample(*nin, nsa_cmp_wk[l], nsa_cmp_wv[l], cache_cmp_k[l], cache_cmp_v[l],
                                       cache_slc_k[l], cache_slc_v[l], cache_win_k[l], cache_win_v[l], page_table)
        xs = xs + mixer_merge(o_a, o_b, o_c, mg, lp)
        xs = xs + channel_mixer(rmsnorm(xs, norm_ffn_g[l]), l)
        _, k_c, v_c, k_s, v_s, _, _, _ = nin
        for lst, a in zip(s_st, (k_c, v_c, k_s, v_s, win_k, win_v, conv_new, h_last, gm_v)):
            lst.append(a)

    y_prompt = rmsnorm(xp, final_norm_g)
    y_sample = rmsnorm(xs, final_norm_g)
    p_cmp_k = jnp.stack(p_st[0])
    p_cmp_v = jnp.stack(p_st[1])
    p_slc_k = jnp.stack(p_st[2])
    p_slc_v = jnp.stack(p_st[3])
    p_win_k = jnp.stack(p_st[4])
    p_win_v = jnp.stack(p_st[5])
    p_conv = jnp.stack(p_st[6])
    p_lru = jnp.stack(p_st[7])
    s_cmp_k = jnp.stack(s_st[0])
    s_cmp_v = jnp.stack(s_st[1])
    s_slc_k = jnp.stack(s_st[2])
    s_slc_v = jnp.stack(s_st[3])
    s_win_k = jnp.stack(s_st[4])
    s_win_v = jnp.stack(s_st[5])
    s_conv = jnp.stack(s_st[6])
    s_lru = jnp.stack(s_st[7])
    s_gm_v = jnp.stack(s_st[8])
    return (y_prompt, y_sample, p_cmp_k, p_cmp_v, p_slc_k, p_slc_v, p_win_k, p_win_v, p_conv, p_lru,
            s_cmp_k, s_cmp_v, s_slc_k, s_slc_v, s_win_k, s_win_v, s_conv, s_lru, s_gm_v)
```

```python
import functools

import jax
import jax.numpy as jnp
from jax import lax
from jax.experimental import pallas as pl
from jax.experimental.pallas import tpu as pltpu

F32 = jnp.float32
BF16 = jnp.bfloat16

NSA_HEADS = 16
NSA_BLOCK = 64
NSA_TOPN = 8
NSA_WINDOW = 512
SEL_FORCE = 1.0e4
LRU_C = 8.0
TOP_K = 2
RMS_EPS = 1e-6
NEG = -1e30
M_INIT = -3.0e38

Q_BLOCK = 128
KV_TILE = 512
LRU_CHUNK = 256
ROW_TILE = 512
PAGES_PER_STEP = 8
VMEM_LIMIT = 56 * 2 ** 20


def _cp(*sem):
    return pltpu.CompilerParams(dimension_semantics=sem, vmem_limit_bytes=VMEM_LIMIT)


def _dot(a, b):
    return jnp.dot(a, b, preferred_element_type=F32)


def _dot_nt(a, b):
    return lax.dot_general(a, b, (((1,), (1,)), ((), ())), preferred_element_type=F32)


def _rms(x, g):
    return x * lax.rsqrt(jnp.mean(x * x, axis=-1, keepdims=True) + RMS_EPS) * g


def _msoftmax(s, mask):
    s = jnp.where(mask, s, NEG)
    e = jnp.where(mask, jnp.exp(s - jnp.max(s, axis=-1, keepdims=True)), 0.0)
    return e / jnp.maximum(jnp.sum(e, axis=-1, keepdims=True), 1e-30)


def _full(shape):
    n = len(shape)
    return pl.BlockSpec(shape, lambda *_: (0,) * n)


def _lru_body(x_ref, g_ref, w_ref, cs_ref, h0_ref, cw_ref, cb_ref, wa_ref, ba_ref, wx_ref, bx_ref, lam_ref,
              oa_ref, cn_ref, hl_ref, xbuf, hc, *, nb, tch, conv_w, starts_at_zero):
    c = pl.program_id(1)
    rows = nb * tch
    wd = hc.shape[-1]
    nst = (conv_w - 1) * nb
    cr = xbuf.shape[0] - rows
    nh = wa_ref.shape[0]
    bw = wd // nh

    @pl.when(c == 0)
    def _():
        if cr > nst:
            xbuf[0:cr - nst, :] = jnp.zeros((cr - nst, wd), F32)
        xbuf[cr - nst:cr, :] = cs_ref[...]
        hc[...] = h0_ref[...]

    xn = _rms(x_ref[...], g_ref[...]).astype(BF16)
    z = _dot(xn, w_ref[...])
    gate = z[:, wd:]
    xbuf[cr:cr + rows, :] = z[:, :wd]
    y = cb_ref[...] + xbuf[cr - nst:cr - nst + rows, :] * cw_ref[0:1, :]
    for k in range(1, conv_w):
        st = cr - nst + k * nb
        y = y + xbuf[st:st + rows, :] * cw_ref[k:k + 1, :]
    cn_ref[...] = xbuf[cr + rows - nst:cr + rows, :]
    xbuf[0:cr, :] = xbuf[rows:rows + cr, :]

    yb = y.astype(BF16)

    def block_diag(wr):
        return jnp.concatenate([_dot(yb[:, j * bw:(j + 1) * bw], wr[j]) for j in range(nh)], axis=1)

    r = jax.nn.sigmoid(block_diag(wa_ref) + ba_ref[...])
    i = jax.nn.sigmoid(block_diag(wx_ref) + bx_ref[...])
    lam = lam_ref[...]
    softplus_neg_lam = jnp.maximum(-lam, 0.0) + jnp.log1p(jnp.exp(-jnp.abs(lam)))
    a = jnp.exp((-LRU_C) * r * softplus_neg_lam)
    mult = jnp.sqrt(1.0 - a * a)
    row = lax.broadcasted_iota(jnp.int32, (rows, wd), 0)
    if starts_at_zero:
        mult = jnp.where((row < nb) & (c == 0), 1.0, mult)
    u = mult * (i * y)
    if nb == 1:
        u = u + jnp.where(row < 1, a * hc[...], 0.0)
    else:
        u = jnp.concatenate([u[:nb] + a[:nb] * hc[...], u[nb:]], axis=0)
    d = nb
    while d < rows:
        a_sh = jnp.where(row < d, 1.0, pltpu.roll(a, d, axis=0))
        u_sh = jnp.where(row < d, 0.0, pltpu.roll(u, d, axis=0))
        u = a * u_sh + u
        a = a * a_sh
        d *= 2
    hc[...] = u[rows - nb:rows, :]
    hl_ref[...] = u[rows - nb:rows, :]
    oa_ref[...] = (jax.nn.gelu(gate) * u).astype(oa_ref.dtype)


def _lru_call(x, g, w, cs, h0, cw, cb, wa, ba, wx, bx, lam, *, nb, tch, starts_at_zero):
    nbatch, rows_total, dm = x.shape
    wd = h0.shape[-1]
    conv_w = cw.shape[0]
    nst = (conv_w - 1) * nb
    rows = nb * tch
    cr = -(-nst // 8) * 8
    assert rows_total % rows == 0 and rows >= cr
    body = functools.partial(_lru_body, nb=nb, tch=tch, conv_w=conv_w, starts_at_zero=starts_at_zero)
    return pl.pallas_call(
        body, name="lru",
        grid=(nbatch, rows_total // rows),
        in_specs=[pl.BlockSpec((None, rows, dm), lambda b, c: (b, c, 0)),
                  _full(g.shape), _full(w.shape),
                  pl.BlockSpec((None, nst, wd), lambda b, c: (b, 0, 0)),
                  pl.BlockSpec((None, nb, wd), lambda b, c: (b, 0, 0)),
                  _full(cw.shape), _full(cb.shape), _full(wa.shape), _full(ba.shape), _full(wx.shape),
                  _full(bx.shape), _full(lam.shape)],
        out_specs=[pl.BlockSpec((None, rows, wd), lambda b, c: (b, c, 0)),
                   pl.BlockSpec((None, nst, wd), lambda b, c: (b, 0, 0)),
                   pl.BlockSpec((None, nb, wd), lambda b, c: (b, 0, 0))],
        out_shape=[jax.ShapeDtypeStruct((nbatch, rows_total, wd), BF16),
                   jax.ShapeDtypeStruct((nbatch, nst, wd), F32),
                   jax.ShapeDtypeStruct((nbatch, nb, wd), F32)],
        scratch_shapes=[pltpu.VMEM((cr + rows, wd), F32), pltpu.VMEM((nb, wd), F32)],
        compiler_params=_cp("arbitrary", "arbitrary"),
    )(x, g, w, cs, h0, cw, cb, wa, ba, wx, bx, lam)


def _gmlp_body(x_ref, g_ref, w_ref, gg_ref, ws_ref, bs_ref, o_ref, *, chunk):
    tm = x_ref.shape[0]
    xn = _rms(x_ref[...], g_ref[...]).astype(BF16)
    ge = jax.nn.gelu(_dot(xn, w_ref[...]))
    wd = ge.shape[1] // 2
    u = ge[:, :wd]
    vb = _rms(ge[:, wd:], gg_ref[...]).astype(BF16)
    ng = ws_ref.shape[0]
    gc = wd // ng
    tril = lax.broadcasted_iota(jnp.int32, (chunk, chunk), 0) >= lax.broadcasted_iota(jnp.int32, (chunk, chunk), 1)
    wsm = [jnp.where(tril, ws_ref[gi], 0.0).astype(BF16) for gi in range(ng)]
    for ch in range(tm // chunk):
        lo, hi = ch * chunk, (ch + 1) * chunk
        mixed = jnp.concatenate([_dot(wsm[gi], vb[lo:hi, gi * gc:(gi + 1) * gc]) for gi in range(ng)], axis=1)
        o_ref[lo:hi, :] = (u[lo:hi] * (mixed + bs_ref[...])).astype(o_ref.dtype)


def _gmlp_call(x, g, w, gg, ws, bs_full, *, tm):
    t, dm = x.shape
    wd = gg.shape[-1]
    chunk = ws.shape[-1]
    assert t % tm == 0 and tm % chunk == 0
    return pl.pallas_call(
        functools.partial(_gmlp_body, chunk=chunk), name="gmlp",
        grid=(t // tm,),
        in_specs=[pl.BlockSpec((tm, dm), lambda i: (i, 0)), _full(g.shape), _full(w.shape), _full(gg.shape),
                  _full(ws.shape), _full(bs_full.shape)],
        out_specs=pl.BlockSpec((tm, wd), lambda i: (i, 0)),
        out_shape=jax.ShapeDtypeStruct((t, wd), BF16),
        compiler_params=_cp("parallel"),
    )(x, g, w, gg, ws, bs_full)


def _gmlp_short_body(x_ref, g_ref, w_ref, gg_ref, wexp_ref, bexp_ref, o_ref, v_ref, *, n, nb):
    xn = _rms(x_ref[...], g_ref[...]).astype(BF16)
    ge = jax.nn.gelu(_dot(xn, w_ref[...]))
    wd = ge.shape[1] // 2
    u = ge[:, :wd]
    v = _rms(ge[:, wd:], gg_ref[...])
    v_ref[...] = v
    for t in range(n):
        m = bexp_ref[t:t + 1, :] + wexp_ref[t * n:t * n + 1, :] * v[0:nb]
        for s in range(1, t + 1):
            m = m + wexp_ref[t * n + s:t * n + s + 1, :] * v[s * nb:(s + 1) * nb]
        o_ref[t * nb:(t + 1) * nb, :] = (u[t * nb:(t + 1) * nb] * m).astype(o_ref.dtype)


def _gmlp_short_call(x, g, w, gg, wexp, bexp, *, n, nb):
    t, dm = x.shape
    wd = gg.shape[-1]
    return pl.pallas_call(
        functools.partial(_gmlp_short_body, n=n, nb=nb), name="gmlp_short",
        grid=(1,),
        in_specs=[_full(x.shape), _full(g.shape), _full(w.shape), _full(gg.shape), _full(wexp.shape),
                  _full(bexp.shape)],
        out_specs=[_full((t, wd)), _full((t, wd))],
        out_shape=[jax.ShapeDtypeStruct((t, wd), BF16), jax.ShapeDtypeStruct((t, wd), F32)],
        compiler_params=_cp("arbitrary"),
    )(x, g, w, gg, wexp, bexp)


def _qkv_body(x_ref, g_ref, w_ref, wck_ref, wcv_ref, *out_refs, qw, kvw, dh, seq, prompt):
    q_ref, kc_ref, vc_ref, ks_ref, vs_ref, kw_ref, vw_ref, gt_ref = out_refs[:8]
    tm = x_ref.shape[0]
    xn = _rms(x_ref[...], g_ref[...]).astype(BF16)
    z = _dot(xn, w_ref[...])
    q_ref[...] = z[:, :qw].astype(BF16)
    parts = [z[:, qw + j * kvw:qw + (j + 1) * kvw] for j in range(6)]
    for ref, p in zip((kc_ref, vc_ref, ks_ref, vs_ref, kw_ref, vw_ref), parts):
        ref[...] = p
    gt_ref[...] = jax.nn.sigmoid(z[:, qw + 6 * kvw:])
    if prompt:
        kcmp_ref, vcmp_ref, ka_ref, vsg_ref, kwb_ref, vwb_ref = out_refs[8:]
        nbt = tm // NSA_BLOCK
        kcmp_ref[...] = jnp.sum(parts[0].reshape(nbt, NSA_BLOCK, kvw) * wck_ref[...][None], axis=1)
        vcmp_ref[...] = jnp.sum(parts[1].reshape(nbt, NSA_BLOCK, kvw) * wcv_ref[...][None], axis=1)
        nbs = seq // NSA_BLOCK
        pos0 = lax.rem(pl.program_id(0) * tm, seq)
        blk = (pos0 + lax.broadcasted_iota(jnp.int32, (tm, nbs), 0)) // NSA_BLOCK
        onehot = (blk == lax.broadcasted_iota(jnp.int32, (tm, nbs), 1)).astype(BF16)
        ksb = parts[2].astype(BF16)
        vsb = parts[3].astype(BF16)
        for gi in range(kvw // dh):
            ka_ref[gi, :, 0:dh] = ksb[:, gi * dh:(gi + 1) * dh]
            ka_ref[gi, :, dh:dh + nbs] = onehot
            vsg_ref[gi] = vsb[:, gi * dh:(gi + 1) * dh]
        kwb_ref[...] = parts[4].astype(BF16)
        vwb_ref[...] = parts[5].astype(BF16)


def _qkv_call(x, g, w, wck, wcv, *, tm, qw, kvw, dh, seq, prompt):
    t, dm = x.shape
    ng = kvw // dh
    assert t % tm == 0
    row = lambda n: pl.BlockSpec((tm, n), lambda i: (i, 0))
    out_specs = [row(qw)] + [row(kvw)] * 6 + [row(128)]
    out_shape = [jax.ShapeDtypeStruct((t, qw), BF16)] + [jax.ShapeDtypeStruct((t, kvw), F32)] * 6 + [
        jax.ShapeDtypeStruct((t, 128), F32)]
    if prompt:
        assert tm % (8 * NSA_BLOCK) == 0 and seq % tm == 0
        nbs = seq // NSA_BLOCK
        assert dh + nbs <= 256
        nbt = tm // NSA_BLOCK
        out_specs += [pl.BlockSpec((nbt, kvw), lambda i: (i, 0))] * 2
        out_specs += [pl.BlockSpec((ng, tm, dh + nbs), lambda i: (0, i, 0)),
                      pl.BlockSpec((ng, tm, dh), lambda i: (0, i, 0)), row(kvw), row(kvw)]
        out_shape += [jax.ShapeDtypeStruct((t // NSA_BLOCK, kvw), F32)] * 2
        out_shape += [jax.ShapeDtypeStruct((ng, t, dh + nbs), BF16), jax.ShapeDtypeStruct((ng, t, dh), BF16),
                      jax.ShapeDtypeStruct((t, kvw), BF16), jax.ShapeDtypeStruct((t, kvw), BF16)]
    body = functools.partial(_qkv_body, qw=qw, kvw=kvw, dh=dh, seq=seq, prompt=prompt)
    return pl.pallas_call(
        body, name="qkv_prompt" if prompt else "qkv_sample",
        grid=(t // tm,),
        in_specs=[row(dm), _full(g.shape), _full(w.shape), _full(wck.shape), _full(wcv.shape)],
        out_specs=out_specs, out_shape=out_shape,
        compiler_params=_cp("parallel"),
    )(x, g, w, wck, wcv)


def _select_blocks(score, blk, nblk, count):
    sel = jnp.zeros(score.shape, jnp.bool_)
    for _ in range(count):
        m = jnp.max(score, axis=1, keepdims=True)
        idx = jnp.min(jnp.where(score == m, blk, nblk), axis=1, keepdims=True)
        hit = blk == idx
        sel = sel | hit
        score = jnp.where(hit, M_INIT, score)
    return sel


def _nsa_cw_body(q_ref, gt_ref, kc_ref, vc_ref, kw_ref, vw_ref, ocw_ref, gsel_ref, selb_ref,
                 *, seq, ng, rep, dh, scale):
    c = pl.program_id(1)
    qb = q_ref.shape[0]
    nblk = kc_ref.shape[0]
    band = min(NSA_WINDOW + qb, seq)
    st = pl.multiple_of(jnp.maximum(c * qb - NSA_WINDOW, 0), qb)
    rows = rep * qb
    t_q = c * qb + lax.broadcasted_iota(jnp.int32, (qb, 1), 0)
    t_r = jnp.concatenate([t_q] * rep, axis=0)
    blk_r = lax.broadcasted_iota(jnp.int32, (rows, nblk), 1)
    ok_c = (blk_r + 1) * NSA_BLOCK - 1 <= t_r
    kp = st + lax.broadcasted_iota(jnp.int32, (rows, band), 1)
    ok_w = (kp <= t_r) & (kp > t_r - NSA_WINDOW)
    blk = lax.broadcasted_iota(jnp.int32, (qb, nblk), 1)
    cur = t_q // NSA_BLOCK
    gates = gt_ref[...]
    for gi in range(ng):
        heads = [gi * rep + r for r in range(rep)]
        qg = jnp.concatenate([q_ref[:, h * dh:(h + 1) * dh] for h in heads], axis=0)
        kcg = kc_ref[:, gi * dh:(gi + 1) * dh].astype(BF16)
        vcg = vc_ref[:, gi * dh:(gi + 1) * dh].astype(BF16)
        p_c = _msoftmax(_dot_nt(qg, kcg) * scale, ok_c)
        o_c = _dot(p_c.astype(BF16), vcg)
        imp = p_c[0:qb]
        for r in range(1, rep):
            imp = imp + p_c[r * qb:(r + 1) * qb]
        score = jnp.where(blk == cur, SEL_FORCE, jnp.where(blk < cur, imp, -1.0))
        sel = _select_blocks(score, blk, nblk, min(NSA_TOPN, nblk))
        selb_ref[gi] = jnp.where(sel, 0.0, NEG).astype(BF16)
        kwg = kw_ref[pl.ds(st, band), gi * dh:(gi + 1) * dh]
        vwg = vw_ref[pl.ds(st, band), gi * dh:(gi + 1) * dh]
        p_w = _msoftmax(_dot_nt(qg, kwg) * scale, ok_w)
        o_w = _dot(p_w.astype(BF16), vwg)
        g_c = jnp.concatenate([gates[:, 3 * h:3 * h + 1] for h in heads], axis=0)
        g_w = jnp.concatenate([gates[:, 3 * h + 2:3 * h + 3] for h in heads], axis=0)
        ocw = g_c * o_c + g_w * o_w
        for r, h in enumerate(heads):
            ocw_ref[:, h * dh:(h + 1) * dh] = ocw[r * qb:(r + 1) * qb]
            gsel_ref[:, h * dh:(h + 1) * dh] = jnp.broadcast_to(gates[:, 3 * h + 1:3 * h + 2], (qb, dh))


def _nsa_cw_call(q, gates, kcmp, vcmp, kwb, vwb, *, nbatch, seq, ng, dh):
    t, qw = q.shape
    kvw = ng * dh
    rep = qw // kvw
    qb = min(Q_BLOCK, seq)
    nq = seq // qb
    nblk = seq // NSA_BLOCK
    body = functools.partial(_nsa_cw_body, seq=seq, ng=ng, rep=rep, dh=dh, scale=dh ** -0.5)
    qrow = lambda n: pl.BlockSpec((qb, n), lambda b, c: (b * nq + c, 0))
    return pl.pallas_call(
        body, name="nsa_cmp_win",
        grid=(nbatch, nq),
        in_specs=[qrow(qw), qrow(128),
                  pl.BlockSpec((nblk, kvw), lambda b, c: (b, 0)), pl.BlockSpec((nblk, kvw), lambda b, c: (b, 0)),
                  pl.BlockSpec((seq, kvw), lambda b, c: (b, 0)), pl.BlockSpec((seq, kvw), lambda b, c: (b, 0))],
        out_specs=[qrow(qw), qrow(qw), pl.BlockSpec((ng, qb, nblk), lambda b, c: (0, b * nq + c, 0))],
        out_shape=[jax.ShapeDtypeStruct((t, qw), F32), jax.ShapeDtypeStruct((t, qw), F32),
                   jax.ShapeDtypeStruct((ng, t, nblk), BF16)],
        compiler_params=_cp("parallel", "arbitrary"),
    )(q, gates, kcmp, vcmp, kwb, vwb)


def _nsa_sel_body(q_ref, selb_ref, ka_ref, vs_ref, ocw_ref, gsel_ref, o_ref, m_sc, l_sc, acc_sc,
                  *, rep, dh, kt, scale):
    c = pl.program_id(2)
    qb = q_ref.shape[0]
    rows = rep * qb
    qg = jnp.concatenate([q_ref[:, r * dh:(r + 1) * dh] for r in range(rep)], axis=0)
    qa = jnp.concatenate([qg, jnp.concatenate([selb_ref[...]] * rep, axis=0)], axis=1)
    m_sc[...] = jnp.full(m_sc.shape, M_INIT, F32)
    l_sc[...] = jnp.zeros(l_sc.shape, F32)
    acc_sc[...] = jnp.zeros(acc_sc.shape, F32)

    def tile(j, causal):
        st = pl.multiple_of(j * kt, kt)
        s = _dot_nt(qa, ka_ref[pl.ds(st, kt), :]) * scale
        if causal:
            kp = st + lax.broadcasted_iota(jnp.int32, (rows, kt), 1)
            t_r = c * qb + (lax.broadcasted_iota(jnp.int32, (rows, kt), 0) & (qb - 1))
            s = jnp.where(kp <= t_r, s, NEG)
        m_old = m_sc[...]
        m_new = jnp.maximum(m_old, jnp.max(s, axis=1, keepdims=True))
        alpha = jnp.exp(m_old - m_new)
        p = jnp.exp(s - m_new)
        l_sc[...] = alpha * l_sc[...] + jnp.sum(p, axis=1, keepdims=True)
        acc_sc[...] = alpha * acc_sc[...] + _dot(p.astype(BF16), vs_ref[pl.ds(st, kt), :])
        m_sc[...] = m_new

    n_full = (c * qb) // kt

    def step(j, carry):
        tile(j, False)
        return carry

    lax.fori_loop(0, n_full, step, 0)
    tile(n_full, True)
    o_s = acc_sc[...] / l_sc[...]
    for r in range(rep):
        lanes = slice(r * dh, (r + 1) * dh)
        o_ref[:, lanes] = (ocw_ref[:, lanes] + gsel_ref[:, lanes] * o_s[r * qb:(r + 1) * qb]).astype(o_ref.dtype)


def _nsa_sel_call(q, selb, ka, vsg, ocw, gsel, *, nbatch, seq, ng, dh):
    t, qw = q.shape
    rep = qw // (ng * dh)
    qb = min(Q_BLOCK, seq)
    assert qb & (qb - 1) == 0
    nq = seq // qb
    nblk = selb.shape[-1]
    kt = min(KV_TILE, seq)
    assert seq % kt == 0 and kt % qb == 0
    body = functools.partial(_nsa_sel_body, rep=rep, dh=dh, kt=kt, scale=dh ** -0.5)
    qcol = pl.BlockSpec((qb, rep * dh), lambda b, g, c: (b * nq + c, g))
    return pl.pallas_call(
        body, name="nsa_selected",
        grid=(nbatch, ng, nq),
        in_specs=[qcol,
                  pl.BlockSpec((None, qb, nblk), lambda b, g, c: (g, b * nq + c, 0)),
                  pl.BlockSpec((None, seq, dh + nblk), lambda b, g, c: (g, b, 0)),
                  pl.BlockSpec((None, seq, dh), lambda b, g, c: (g, b, 0)),
                  qcol, qcol],
        out_specs=qcol,
        out_shape=jax.ShapeDtypeStruct((t, qw), BF16),
        scratch_shapes=[pltpu.VMEM((rep * qb, 1), F32), pltpu.VMEM((rep * qb, 1), F32),
                        pltpu.VMEM((rep * qb, dh), F32)],
        compiler_params=_cp("parallel", "parallel", "arbitrary"),
    )(q, selb, ka, vsg, ocw, gsel)


def _cmp_pages_body(pt_ref, *refs, pps):
    del pt_ref
    k_refs, v_refs = refs[:pps], refs[pps:2 * pps]
    wk_ref, wv_ref, kc_ref, vc_ref = refs[2 * pps:]
    page, kvw = k_refs[0].shape
    bpp = page // NSA_BLOCK

    def summarise(page_refs, w_ref):
        w = w_ref[...][None]
        return jnp.concatenate([jnp.sum(r[...].reshape(bpp, NSA_BLOCK, kvw) * w, axis=1) for r in page_refs], axis=0)

    kc_ref[...] = summarise(k_refs, wk_ref)
    vc_ref[...] = summarise(v_refs, wv_ref)


def _page_specs(pps, page, kvw, base):
    return [pl.BlockSpec((None, page, kvw), lambda b, j, pt, i=i: (base + pt[b, j * pps + i], 0, 0))
            for i in range(pps)]


def _cmp_pages_call(page_table, pool_k, pool_v, wck, wcv, *, base, pps):
    nbs, npg = page_table.shape
    _, page, kvw = pool_k.shape
    bpp = page // NSA_BLOCK
    assert npg % pps == 0 and (pps * bpp) % 8 == 0 or pps == npg
    grid_spec = pltpu.PrefetchScalarGridSpec(
        num_scalar_prefetch=1, grid=(nbs, npg // pps),
        in_specs=_page_specs(pps, page, kvw, base) * 2 + [pl.BlockSpec(wck.shape, lambda b, j, pt: (0, 0))] * 2,
        out_specs=[pl.BlockSpec((None, pps * bpp, kvw), lambda b, j, pt: (b, j, 0))] * 2)
    return pl.pallas_call(
        functools.partial(_cmp_pages_body, pps=pps), name="sample_cmp_pages",
        grid_spec=grid_spec,
        out_shape=[jax.ShapeDtypeStruct((nbs, npg * bpp, kvw), F32)] * 2,
        compiler_params=_cp("parallel", "arbitrary"),
    )(page_table, *([pool_k] * pps), *([pool_v] * pps), wck, wcv)


def _samp_cw_body(q_ref, g_ref, kc_ref, vc_ref, wk_ref, wv_ref, kn_ref, vn_ref,
                  ocw_ref, selb_ref, wko_ref, wvo_ref, *, ts, past, rep, scale):
    rows = q_ref.shape[0]
    nblk = kc_ref.shape[0]
    wb = wk_ref.shape[0]
    npad = kn_ref.shape[0]
    gt = rows // rep
    q = q_ref[...]
    ti = lax.rem(lax.broadcasted_iota(jnp.int32, (rows, 1), 0), ts)
    t = past + ti
    blk_r = lax.broadcasted_iota(jnp.int32, (rows, nblk), 1)
    p_c = _msoftmax(_dot_nt(q, kc_ref[...].astype(BF16)) * scale, (blk_r + 1) * NSA_BLOCK - 1 <= t)
    o_c = _dot(p_c.astype(BF16), vc_ref[...].astype(BF16))
    imp = p_c[0:gt]
    for r in range(1, rep):
        imp = imp + p_c[r * gt:(r + 1) * gt]
    blk = lax.broadcasted_iota(jnp.int32, (gt, nblk), 1)
    sel = _select_blocks(imp, blk, nblk, min(NSA_TOPN - 1, nblk))
    selb_ref[...] = jnp.where(sel, 0.0, NEG)
    s1 = _dot_nt(q, wk_ref[...].astype(BF16)) * scale
    s2 = _dot_nt(q, kn_ref[...].astype(BF16)) * scale
    kp1 = past - wb + lax.broadcasted_iota(jnp.int32, (rows, wb), 1)
    j2 = lax.broadcasted_iota(jnp.int32, (rows, npad), 1)
    ok1 = (kp1 <= t) & (kp1 > t - NSA_WINDOW) & (kp1 >= 0)
    ok2 = (j2 <= ti) & (j2 < ts)
    s1 = jnp.where(ok1, s1, NEG)
    s2 = jnp.where(ok2, s2, NEG)
    m = jnp.maximum(jnp.max(s1, axis=1, keepdims=True), jnp.max(s2, axis=1, keepdims=True))
    e1 = jnp.where(ok1, jnp.exp(s1 - m), 0.0)
    e2 = jnp.where(ok2, jnp.exp(s2 - m), 0.0)
    den = jnp.maximum(jnp.sum(e1, axis=1, keepdims=True) + jnp.sum(e2, axis=1, keepdims=True), 1e-30)
    o_w = _dot((e1 / den).astype(BF16), wv_ref[...].astype(BF16)) + _dot((e2 / den).astype(BF16),
                                                                          vn_ref[...].astype(BF16))
    g = g_ref[...]
    ocw_ref[...] = g[:, 0:1] * o_c + g[:, 2:3] * o_w
    wko_ref[0:wb - ts, :] = wk_ref[ts:wb, :]
    wko_ref[wb - ts:wb, :] = kn_ref[0:ts, :]
    wvo_ref[0:wb - ts, :] = wv_ref[ts:wb, :]
    wvo_ref[wb - ts:wb, :] = vn_ref[0:ts, :]


def _samp_cw_call(qall, gall, kc, vc, win_k, win_v, kw_new, vw_new, *, base, ts, past, rep, dh):
    nbs, rows, kvw = qall.shape
    nblk = kc.shape[1]
    wb = win_k.shape[1]
    npad = kw_new.shape[1]
    gt = rows // rep
    per_b = lambda r, n: pl.BlockSpec((None, r, n), lambda b: (b, 0, 0))
    win = pl.BlockSpec((None, wb, kvw), lambda b: (base + b, 0, 0))
    body = functools.partial(_samp_cw_body, ts=ts, past=past, rep=rep, scale=dh ** -0.5)
    return pl.pallas_call(
        body, name="sample_cmp_win",
        grid=(nbs,),
        in_specs=[per_b(rows, kvw), per_b(rows, 128), per_b(nblk, kvw), per_b(nblk, kvw), win, win,
                  per_b(npad, kvw), per_b(npad, kvw)],
        out_specs=[per_b(rows, kvw), per_b(gt, nblk), per_b(wb, kvw), per_b(wb, kvw)],
        out_shape=[jax.ShapeDtypeStruct((nbs, rows, kvw), F32), jax.ShapeDtypeStruct((nbs, gt, nblk), F32),
                   jax.ShapeDtypeStruct((nbs, wb, kvw), F32), jax.ShapeDtypeStruct((nbs, wb, kvw), F32)],
        compiler_params=_cp("parallel"),
    )(qall, gall, kc, vc, win_k, win_v, kw_new, vw_new)


def _samp_sel_body(pt_ref, *refs, pps, ts, rep, scale):
    del pt_ref
    k_refs, v_refs = refs[:pps], refs[pps:2 * pps]
    q_ref, selb_ref, kn_ref, vn_ref, ocw_ref, g_ref, o_ref, m_sc, l_sc, acc_sc = refs[2 * pps:]
    j = pl.program_id(1)
    rows = q_ref.shape[0]
    nblk = selb_ref.shape[1]
    page = k_refs[0].shape[0]
    kt = pps * page
    q = q_ref[...]

    @pl.when(j == 0)
    def _():
        m_sc[...] = jnp.full(m_sc.shape, M_INIT, F32)
        l_sc[...] = jnp.zeros(l_sc.shape, F32)
        acc_sc[...] = jnp.zeros(acc_sc.shape, F32)

    def update(s, v):
        m_old = m_sc[...]
        m_new = jnp.maximum(m_old, jnp.max(s, axis=1, keepdims=True))
        alpha = jnp.exp(m_old - m_new)
        p = jnp.exp(s - m_new)
        l_sc[...] = alpha * l_sc[...] + jnp.sum(p, axis=1, keepdims=True)
        acc_sc[...] = alpha * acc_sc[...] + _dot(p.astype(BF16), v)
        m_sc[...] = m_new

    kc = jnp.concatenate([r[...] for r in k_refs], axis=0).astype(BF16)
    vc = jnp.concatenate([r[...] for r in v_refs], axis=0).astype(BF16)
    sel_rows = jnp.concatenate([selb_ref[...]] * rep, axis=0).astype(BF16)
    key_blk = j * (kt // NSA_BLOCK) + lax.broadcasted_iota(jnp.int32, (nblk, kt), 1) // NSA_BLOCK
    onehot = (key_blk == lax.broadcasted_iota(jnp.int32, (nblk, kt), 0)).astype(BF16)
    update(_dot_nt(q, kc) * scale + _dot(sel_rows, onehot), vc)

    @pl.when(j == pl.num_programs(1) - 1)
    def _():
        npad = kn_ref.shape[0]
        ti = lax.rem(lax.broadcasted_iota(jnp.int32, (rows, npad), 0), ts)
        j2 = lax.broadcasted_iota(jnp.int32, (rows, npad), 1)
        s = jnp.where((j2 <= ti) & (j2 < ts), _dot_nt(q, kn_ref[...].astype(BF16)) * scale, NEG)
        update(s, vn_ref[...].astype(BF16))
        o_ref[...] = ocw_ref[...] + g_ref[:, 1:2] * (acc_sc[...] / l_sc[...])


def _samp_sel_call(page_table, pool_k, pool_v, qall, selb, ks_new, vs_new, ocw, gall, *, base, pps, ts, rep, dh):
    nbs, npg = page_table.shape
    _, page, kvw = pool_k.shape
    rows = qall.shape[1]
    per_b = lambda r, n: pl.BlockSpec((None, r, n), lambda b, j, pt: (b, 0, 0))
    grid_spec = pltpu.PrefetchScalarGridSpec(
        num_scalar_prefetch=1, grid=(nbs, npg // pps),
        in_specs=_page_specs(pps, page, kvw, base) * 2 + [
            per_b(rows, kvw), per_b(selb.shape[1], selb.shape[2]), per_b(ks_new.shape[1], kvw),
            per_b(ks_new.shape[1], kvw), per_b(rows, kvw), per_b(rows, 128)],
        out_specs=per_b(rows, kvw),
        scratch_shapes=[pltpu.VMEM((rows, 1), F32), pltpu.VMEM((rows, 1), F32), pltpu.VMEM((rows, kvw), F32)])
    body = functools.partial(_samp_sel_body, pps=pps, ts=ts, rep=rep, scale=dh ** -0.5)
    return pl.pallas_call(
        body, name="sample_selected",
        grid_spec=grid_spec,
        out_shape=jax.ShapeDtypeStruct((nbs, rows, kvw), F32),
        compiler_params=_cp("parallel", "arbitrary"),
    )(page_table, *([pool_k] * pps), *([pool_v] * pps), qall, selb, ks_new, vs_new, ocw, gall)


def _merge_body(x_ref, g_ref, wmg_ref, oa_ref, ob_ref, oc_ref, wb_ref, wo_ref, o_ref):
    x = x_ref[...]
    dm = x.shape[1]
    xn = _rms(x, g_ref[...]).astype(BF16)
    mg = jax.nn.sigmoid(_dot(xn, wmg_ref[...]))
    y = mg[:, 0:dm] * _dot(oa_ref[...], wb_ref[0])
    y = y + mg[:, dm:2 * dm] * _dot(ob_ref[...], wb_ref[1])
    y = y + mg[:, 2 * dm:3 * dm] * _dot(oc_ref[...], wb_ref[2])
    o_ref[...] = x + _dot(y.astype(BF16), wo_ref[...])


def _merge_call(x, g, wmg, oa, ob, oc, wb, wo, *, tm):
    t, dm = x.shape
    row = pl.BlockSpec((tm, dm), lambda i: (i, 0))
    return pl.pallas_call(
        _merge_body, name="merge",
        grid=(t // tm,),
        in_specs=[row, _full(g.shape), _full(wmg.shape), row, row, row, _full(wb.shape), _full(wo.shape)],
        out_specs=row,
        out_shape=jax.ShapeDtypeStruct((t, dm), F32),
        compiler_params=_cp("parallel"),
    )(x, g, wmg, oa, ob, oc, wb, wo)


def _ffn_body(x_ref, g_ref, w1_ref, w3_ref, w2_ref, fg_ref, o_ref, xn_sc, acc_sc, *, final):
    f = pl.program_id(1)

    @pl.when(f == 0)
    def _():
        xn_sc[...] = _rms(x_ref[...], g_ref[...]).astype(BF16)
        acc_sc[...] = jnp.zeros(acc_sc.shape, F32)

    xn = xn_sc[...]
    h = jax.nn.silu(_dot(xn, w1_ref[...])) * _dot(xn, w3_ref[...])
    acc_sc[...] += _dot(h.astype(BF16), w2_ref[...])

    @pl.when(f == pl.num_programs(1) - 1)
    def _():
        y = x_ref[...] + acc_sc[...]
        o_ref[...] = _rms(y, fg_ref[...]) if final else y


def _ffn_tile(f_dim):
    half = f_dim // 2
    return half if f_dim % 2 == 0 and half % 128 == 0 else f_dim


def _ffn_call(x, g, w1, w3, w2, fg, *, tm, final):
    t, dm = x.shape
    fd = w1.shape[1]
    ft = _ffn_tile(fd)
    row = pl.BlockSpec((tm, dm), lambda i, f: (i, 0))
    vec = pl.BlockSpec((1, dm), lambda i, f: (0, 0))
    return pl.pallas_call(
        functools.partial(_ffn_body, final=final), name="ffn",
        grid=(t // tm, fd // ft),
        in_specs=[row, vec, pl.BlockSpec((dm, ft), lambda i, f: (0, f)), pl.BlockSpec((dm, ft), lambda i, f: (0, f)),
                  pl.BlockSpec((ft, dm), lambda i, f: (f, 0)), vec],
        out_specs=row,
        out_shape=jax.ShapeDtypeStruct((t, dm), F32),
        scratch_shapes=[pltpu.VMEM((tm, dm), BF16), pltpu.VMEM((tm, dm), F32)],
        compiler_params=_cp("parallel", "arbitrary"),
    )(x, g, w1, w3, w2, fg)


def _moe_body(x_ref, g_ref, wr_ref, br_ref, w1_ref, w3_ref, w2_ref, fg_ref, o_ref, xn_sc, comb_sc, acc_sc, *, final):
    e = pl.program_id(1)
    f = pl.program_id(2)

    @pl.when((e == 0) & (f == 0))
    def _():
        xn = _rms(x_ref[...], g_ref[...])
        xn_sc[...] = xn.astype(BF16)
        acc_sc[...] = jnp.zeros(acc_sc.shape, F32)
        logits = jnp.dot(xn, wr_ref[...], preferred_element_type=F32, precision=lax.Precision.HIGHEST) + br_ref[...]
        lane = lax.broadcasted_iota(jnp.int32, logits.shape, 1)
        nl = logits.shape[1]
        m1 = jnp.max(logits, axis=1, keepdims=True)
        i1 = jnp.min(jnp.where(logits == m1, lane, nl), axis=1, keepdims=True)
        rest = jnp.where(lane == i1, M_INIT, logits)
        m2 = jnp.max(rest, axis=1, keepdims=True)
        i2 = jnp.min(jnp.where(rest == m2, lane, nl), axis=1, keepdims=True)
        e2 = jnp.exp(m2 - m1)
        den = 1.0 + e2
        comb_sc[...] = jnp.where(lane == i1, 1.0 / den, 0.0) + jnp.where(lane == i2, e2 / den, 0.0)

    xn = xn_sc[...]
    h = jax.nn.silu(_dot(xn, w1_ref[...])) * _dot(xn, w3_ref[...])
    comb = comb_sc[...]
    lane = lax.broadcasted_iota(jnp.int32, comb.shape, 1)
    ce = jnp.sum(jnp.where(lane == e, comb, 0.0), axis=1, keepdims=True)
    acc_sc[...] += ce * _dot(h.astype(BF16), w2_ref[...])

    @pl.when((e == pl.num_programs(1) - 1) & (f == pl.num_programs(2) - 1))
    def _():
        y = x_ref[...] + acc_sc[...]
        o_ref[...] = _rms(y, fg_ref[...]) if final else y


def _moe_call(x, g, wr, br, w1, w3, w2, fg, *, tm, final):
    t, dm = x.shape
    ne, _, fd = w1.shape
    ft = _ffn_tile(fd)
    row = pl.BlockSpec((tm, dm), lambda i, e, f: (i, 0))
    vec = lambda n: pl.BlockSpec((1, n), lambda i, e, f: (0, 0))
    return pl.pallas_call(
        functools.partial(_moe_body, final=final), name="moe",
        grid=(t // tm, ne, fd // ft),
        in_specs=[row, vec(dm), pl.BlockSpec(wr.shape, lambda i, e, f: (0, 0)), vec(wr.shape[1]),
                  pl.BlockSpec((None, dm, ft), lambda i, e, f: (e, 0, f)),
                  pl.BlockSpec((None, dm, ft), lambda i, e, f: (e, 0, f)),
                  pl.BlockSpec((None, ft, dm), lambda i, e, f: (e, f, 0)), vec(dm)],
        out_specs=row,
        out_shape=jax.ShapeDtypeStruct((t, dm), F32),
        scratch_shapes=[pltpu.VMEM((tm, dm), BF16), pltpu.VMEM((tm, wr.shape[1]), F32), pltpu.VMEM((tm, dm), F32)],
        compiler_params=_cp("parallel", "arbitrary", "arbitrary"),
    )(x, g, wr, br, w1, w3, w2, fg)


def kernel(x_prompt, x_sample, cache_cmp_k, cache_cmp_v, cache_slc_k, cache_slc_v, cache_win_k, cache_win_v, state_conv, state_lru, page_table, norm_mix_g, w_in, lru_conv_w, lru_conv_b, lru_wa, lru_ba, lru_wx, lru_bx, lru_lambda, nsa_cmp_wk, nsa_cmp_wv, gm_norm_g, gm_ws, gm_bs, w_branch, w_out, norm_ffn_g, ffn_w1, ffn_w3, ffn_w2, moe_wr, moe_br, moe_w1, moe_w3, moe_w2, final_norm_g):
    depth, dm, _ = w_in.shape
    nbp, seq, _ = x_prompt.shape
    nbs, ts, _ = x_sample.shape
    _, n_pool, page, ng, dh = cache_cmp_k.shape
    wd = state_lru.shape[-1]
    conv_w = lru_conv_w.shape[1]
    gmw = gm_norm_g.shape[-1]
    n_gm, chunk = gm_ws.shape[1], gm_ws.shape[2]
    kvw = ng * dh
    qw = NSA_HEADS * dh
    rep = NSA_HEADS // ng
    npg = page_table.shape[1]
    past = npg * page
    wbuf = cache_win_k.shape[2]
    tp, tsm = nbp * seq, nbs * ts
    ne = moe_wr.shape[-1]

    assert seq % chunk == 0 and seq % LRU_CHUNK == 0 and seq >= NSA_WINDOW
    assert past % NSA_BLOCK == 0 and ts < NSA_BLOCK and ts < chunk and wbuf == NSA_WINDOW and nbs % 8 == 0
    assert past // NSA_BLOCK >= NSA_TOPN - 1

    offs = [0, wd, 2 * wd, 2 * wd + qw, 2 * wd + qw + 6 * kvw, 2 * wd + qw + 6 * kvw + 3 * NSA_HEADS]
    offs += [offs[-1] + 2 * gmw, offs[-1] + 2 * gmw + 3 * dm]
    w_in_b = w_in.astype(BF16)
    w_lru = w_in_b[:, :, 0:offs[2]]
    w_nsa = jnp.pad(w_in_b[:, :, offs[2]:offs[5]], ((0, 0), (0, 0), (0, 128 - 3 * NSA_HEADS)))
    w_uv = w_in_b[:, :, offs[5]:offs[6]]
    w_mg = w_in_b[:, :, offs[6]:offs[7]]
    wa_b, wx_b = lru_wa.astype(BF16), lru_wx.astype(BF16)
    wb_b, wo_b = w_branch.astype(BF16), w_out.astype(BF16)
    f1_b, f3_b, f2_b = ffn_w1.astype(BF16), ffn_w3.astype(BF16), ffn_w2.astype(BF16)
    m1_b, m3_b, m2_b = moe_w1.astype(BF16), moe_w3.astype(BF16), moe_w2.astype(BF16)
    wr_pad = jnp.pad(moe_wr, ((0, 0), (0, 0), (0, 128 - ne)))
    br_pad = jnp.pad(moe_br, ((0, 0), (0, 128 - ne)), constant_values=NEG)[:, None, :]
    wck = nsa_cmp_wk.reshape(depth, NSA_BLOCK, kvw)
    wcv = nsa_cmp_wv.reshape(depth, NSA_BLOCK, kvw)
    gc = gmw // n_gm
    bs_full = jnp.repeat(jnp.swapaxes(gm_bs, 1, 2), gc, axis=2)
    ws_short = jnp.repeat(jnp.transpose(gm_ws[:, :, :ts, :ts], (0, 2, 3, 1)).reshape(depth, ts * ts, n_gm), gc, axis=2)
    vec = lambda a, l: a[l][None, :]

    pool = lambda a: a.reshape(depth * n_pool, page, kvw)
    pool_ck, pool_cv, pool_sk, pool_sv = pool(cache_cmp_k), pool(cache_cmp_v), pool(cache_slc_k), pool(cache_slc_v)
    win_k = cache_win_k.reshape(depth * nbs, wbuf, kvw)
    win_v = cache_win_v.reshape(depth * nbs, wbuf, kvw)
    pps = min(PAGES_PER_STEP, npg)
    eye_g = jnp.eye(ng, dtype=F32)

    tm_p = min(ROW_TILE, tp)
    tm_s = tsm
    xp = x_prompt.reshape(tp, dm)
    xs = x_sample.reshape(tsm, dm)
    zeros_cs = jnp.zeros((nbp, conv_w - 1, wd), F32)
    zeros_h = jnp.zeros((nbp, 1, wd), F32)
    to_tm = lambda a, n: jnp.swapaxes(a.reshape(nbs, n, -1), 0, 1).reshape(1, n * nbs, -1)
    from_tm = lambda a, n: jnp.swapaxes(a.reshape(n, nbs, -1), 0, 1)

    p_out = [[] for _ in range(8)]
    s_out = [[] for _ in range(9)]
    for l in range(depth):
        last = l == depth - 1
        gmix = vec(norm_mix_g, l)
        lru_w = (lru_conv_w[l], vec(lru_conv_b, l), wa_b[l], vec(lru_ba, l), wx_b[l], vec(lru_bx, l), vec(lru_lambda, l))

        o_a, conv_new, h_last = _lru_call(xp.reshape(nbp, seq, dm), gmix, w_lru[l], zeros_cs, zeros_h, *lru_w,
                                          nb=1, tch=min(LRU_CHUNK, seq), starts_at_zero=True)
        o_c = _gmlp_call(xp, gmix, w_uv[l], vec(gm_norm_g, l), gm_ws[l], bs_full[l], tm=tm_p)
        (q, k_c, v_c, k_s, v_s, k_w, v_w, gates, kcmp, vcmp, ka, vsg, kwb, vwb) = _qkv_call(
            xp, gmix, w_nsa[l], wck[l], wcv[l], tm=tm_p, qw=qw, kvw=kvw, dh=dh, seq=seq, prompt=True)
        ocw, gsel, selb = _nsa_cw_call(q, gates, kcmp, vcmp, kwb, vwb, nbatch=nbp, seq=seq, ng=ng, dh=dh)
        o_b = _nsa_sel_call(q, selb, ka, vsg, ocw, gsel, nbatch=nbp, seq=seq, ng=ng, dh=dh)
        xp = _merge_call(xp, gmix, w_mg[l], o_a.reshape(tp, wd), o_b, o_c, wb_b[l], wo_b[l], tm=tm_p)
        if l % 2 == 0:
            xp = _ffn_call(xp, vec(norm_ffn_g, l), f1_b[l // 2], f3_b[l // 2], f2_b[l // 2], final_norm_g[None, :],
                           tm=tm_p, final=last)
        else:
            xp = _moe_call(xp, vec(norm_ffn_g, l), wr_pad[l // 2], br_pad[l // 2], m1_b[l // 2], m3_b[l // 2],
                           m2_b[l // 2], final_norm_g[None, :], tm=tm_p, final=last)
        keep = min(NSA_WINDOW, seq)
        shp = (nbp, seq, ng, dh)
        for lst, a in zip(p_out, (k_c.reshape(shp), v_c.reshape(shp), k_s.reshape(shp), v_s.reshape(shp),
                                  k_w.reshape(shp)[:, seq - keep:], v_w.reshape(shp)[:, seq - keep:],
                                  conv_new, h_last.reshape(nbp, wd))):
            lst.append(a)

        xs_tm = to_tm(xs, ts)
        o_a, conv_new, h_last = _lru_call(xs_tm, gmix, w_lru[l], to_tm(state_conv[l], conv_w - 1),
                                          state_lru[l][None], *lru_w, nb=nbs, tch=ts, starts_at_zero=False)
        o_a = from_tm(o_a, ts).reshape(tsm, wd)
        o_c, gm_v = _gmlp_short_call(xs_tm[0], gmix, w_uv[l], vec(gm_norm_g, l), ws_short[l], bs_full[l][:ts],
                                     n=ts, nb=nbs)
        o_c = from_tm(o_c, ts).reshape(tsm, gmw)
        q, k_c, v_c, k_s, v_s, k_w, v_w, gates = _qkv_call(
            xs, gmix, w_nsa[l], wck[l], wcv[l], tm=tm_s, qw=qw, kvw=kvw, dh=dh, seq=ts, prompt=False)
        q5 = jnp.transpose(q.reshape(nbs, ts, ng, rep, dh), (0, 3, 2, 1, 4))
        qall = (q5[:, :, :, :, None, :] * eye_g.astype(q5.dtype)[None, None, :, None, :, None]).reshape(
            nbs, rep * ng * ts, kvw)
        g5 = jnp.transpose(gates[:, :3 * NSA_HEADS].reshape(nbs, ts, ng, rep, 3), (0, 3, 2, 1, 4))
        gall = jnp.pad(g5.reshape(nbs, rep * ng * ts, 3), ((0, 0), (0, 0), (0, 125)))
        npad = -(-ts // 8) * 8
        new_rows = lambda a: jnp.pad(a.reshape(nbs, ts, kvw), ((0, 0), (0, npad - ts), (0, 0)))
        kc_past, vc_past = _cmp_pages_call(page_table, pool_ck, pool_cv, wck[l], wcv[l], base=l * n_pool, pps=pps)
        ocw, selb, win_k_new, win_v_new = _samp_cw_call(
            qall, gall, kc_past, vc_past, win_k, win_v, new_rows(k_w), new_rows(v_w),
            base=l * nbs, ts=ts, past=past, rep=rep, dh=dh)
        o_all = _samp_sel_call(page_table, pool_sk, pool_sv, qall, selb, new_rows(k_s), new_rows(v_s), ocw, gall,
                               base=l * n_pool, pps=pps, ts=ts, rep=rep, dh=dh)
        o6 = o_all.reshape(nbs, rep, ng, ts, ng, dh)
        o_b = jnp.stack([o6[:, :, gi, :, gi, :] for gi in range(ng)], axis=2)
        o_b = jnp.transpose(o_b, (0, 3, 2, 1, 4)).reshape(tsm, qw).astype(BF16)
        xs = _merge_call(xs, gmix, w_mg[l], o_a, o_b, o_c, wb_b[l], wo_b[l], tm=tm_s)
        if l % 2 == 0:
            xs = _ffn_call(xs, vec(norm_ffn_g, l), f1_b[l // 2], f3_b[l // 2], f2_b[l // 2], final_norm_g[None, :],
                           tm=tm_s, final=last)
        else:
            xs = _moe_call(xs, vec(norm_ffn_g, l), wr_pad[l // 2], br_pad[l // 2], m1_b[l // 2], m3_b[l // 2],
                           m2_b[l // 2], final_norm_g[None, :], tm=tm_s, final=last)
        shs = (nbs, ts, ng, dh)
        for lst, a in zip(s_out, (k_c.reshape(shs), v_c.reshape(shs), k_s.reshape(shs), v_s.reshape(shs),
                                  win_k_new.reshape(nbs, wbuf, ng, dh), win_v_new.reshape(nbs, wbuf, ng, dh),
                                  from_tm(conv_new[0], conv_w - 1), h_last[0], from_tm(gm_v, ts))):
            lst.append(a)

    outs = [xp.reshape(nbp, seq, dm), xs.reshape(nbs, ts, dm)]
    outs += [jnp.stack(a) for a in p_out]
    outs += [jnp.stack(a) for a in s_out]
    return tuple(outs)
```

```python
import functools
import math

import jax
import jax.numpy as jnp
from jax import lax
from jax.experimental import pallas as pl
from jax.experimental.pallas import tpu as pltpu

F32 = jnp.float32
BF16 = jnp.bfloat16

NSA_HEADS = 16
NSA_BLOCK = 64
NSA_TOPN = 8
NSA_WINDOW = 512
SEL_FORCE = 1.0e4
LRU_C = 8.0
TOP_K = 2
RMS_EPS = 1e-6
NEG = -1e30
M_INIT = -3.0e38

Q_BLOCK = 128
KV_TILE = 1024
LRU_CHUNK = 256
ROW_TILE = 512
PAGES_PER_STEP = 8
BF16_ROWS = 16
VMEM_LIMIT = 56 * 2 ** 20


def _cp(*sem):
    return pltpu.CompilerParams(dimension_semantics=sem, vmem_limit_bytes=VMEM_LIMIT)


def _dot(a, b):
    return jnp.dot(a, b, preferred_element_type=F32)


def _dot_nt(a, b):
    return lax.dot_general(a, b, (((1,), (1,)), ((), ())), preferred_element_type=F32)


def _rms(x, g):
    return x * lax.rsqrt(jnp.mean(x * x, axis=-1, keepdims=True) + RMS_EPS) * g


def _msoftmax(s, mask, axis=-1):
    s = jnp.where(mask, s, NEG)
    e = jnp.where(mask, jnp.exp(s - jnp.max(s, axis=axis, keepdims=True)), 0.0)
    return e / jnp.maximum(jnp.sum(e, axis=axis, keepdims=True), 1e-30)


def _is_power_of_two(x):
    return math.frexp(x)[0] == 0.5


def _full(shape):
    n = len(shape)
    return pl.BlockSpec(shape, lambda *_: (0,) * n)


def _lru_body(x_ref, g_ref, w_ref, cs_ref, h0_ref, cw_ref, cb_ref, wa_ref, ba_ref, wx_ref, bx_ref, lam_ref,
              oa_ref, cn_ref, hl_ref, xbuf, hc, *, nb, tch, conv_w, starts_at_zero):
    c = pl.program_id(1)
    rows = nb * tch
    wd = hc.shape[-1]
    nst = (conv_w - 1) * nb
    cr = xbuf.shape[0] - rows
    nh = wa_ref.shape[0]
    bw = wd // nh

    @pl.when(c == 0)
    def _():
        if cr > nst:
            xbuf[0:cr - nst, :] = jnp.zeros((cr - nst, wd), F32)
        xbuf[cr - nst:cr, :] = cs_ref[...]
        hc[...] = h0_ref[...]

    xn = _rms(x_ref[...], g_ref[...]).astype(BF16)
    z = _dot(xn, w_ref[...])
    gate = z[:, wd:]
    xbuf[cr:cr + rows, :] = z[:, :wd]
    y = cb_ref[...] + xbuf[cr - nst:cr - nst + rows, :] * cw_ref[0:1, :]
    for k in range(1, conv_w):
        st = cr - nst + k * nb
        y = y + xbuf[st:st + rows, :] * cw_ref[k:k + 1, :]
    cn_ref[...] = xbuf[cr + rows - nst:cr + rows, :]
    xbuf[0:cr, :] = xbuf[rows:rows + cr, :]

    yb = y.astype(BF16)

    def block_diag(wr):
        return jnp.concatenate([_dot(yb[:, j * bw:(j + 1) * bw], wr[j]) for j in range(nh)], axis=1)

    r = jax.nn.sigmoid(block_diag(wa_ref) + ba_ref[...])
    i = jax.nn.sigmoid(block_diag(wx_ref) + bx_ref[...])
    lam = lam_ref[...]
    softplus_neg_lam = jnp.maximum(-lam, 0.0) + jnp.log1p(jnp.exp(-jnp.abs(lam)))
    a = jnp.exp((-LRU_C) * r * softplus_neg_lam)
    mult = jnp.sqrt(1.0 - a * a)
    row = lax.broadcasted_iota(jnp.int32, (rows, wd), 0)
    if starts_at_zero:
        mult = jnp.where((row < nb) & (c == 0), 1.0, mult)
    u = mult * (i * y)
    if nb == 1:
        u = u + jnp.where(row < 1, a * hc[...], 0.0)
    else:
        u = jnp.concatenate([u[:nb] + a[:nb] * hc[...], u[nb:]], axis=0)
    d = nb
    while d < rows:
        a_sh = jnp.where(row < d, 1.0, pltpu.roll(a, d, axis=0))
        u_sh = jnp.where(row < d, 0.0, pltpu.roll(u, d, axis=0))
        u = a * u_sh + u
        a = a * a_sh
        d *= 2
    hc[...] = u[rows - nb:rows, :]
    hl_ref[...] = u[rows - nb:rows, :]
    oa_ref[...] = (jax.nn.gelu(gate) * u).astype(oa_ref.dtype)


def _lru_call(x, g, w, cs, h0, cw, cb, wa, ba, wx, bx, lam, *, nb, tch, starts_at_zero):
    nbatch, rows_total, dm = x.shape
    wd = h0.shape[-1]
    conv_w = cw.shape[0]
    nst = (conv_w - 1) * nb
    rows = nb * tch
    cr = -(-nst // 8) * 8
    assert rows_total % rows == 0 and rows >= cr
    body = functools.partial(_lru_body, nb=nb, tch=tch, conv_w=conv_w, starts_at_zero=starts_at_zero)
    return pl.pallas_call(
        body, name="lru",
        grid=(nbatch, rows_total // rows),
        in_specs=[pl.BlockSpec((None, rows, dm), lambda b, c: (b, c, 0)),
                  _full(g.shape), _full(w.shape),
                  pl.BlockSpec((None, nst, wd), lambda b, c: (b, 0, 0)),
                  pl.BlockSpec((None, nb, wd), lambda b, c: (b, 0, 0)),
                  _full(cw.shape), _full(cb.shape), _full(wa.shape), _full(ba.shape), _full(wx.shape),
                  _full(bx.shape), _full(lam.shape)],
        out_specs=[pl.BlockSpec((None, rows, wd), lambda b, c: (b, c, 0)),
                   pl.BlockSpec((None, nst, wd), lambda b, c: (b, 0, 0)),
                   pl.BlockSpec((None, nb, wd), lambda b, c: (b, 0, 0))],
        out_shape=[jax.ShapeDtypeStruct((nbatch, rows_total, wd), BF16),
                   jax.ShapeDtypeStruct((nbatch, nst, wd), F32),
                   jax.ShapeDtypeStruct((nbatch, nb, wd), F32)],
        scratch_shapes=[pltpu.VMEM((cr + rows, wd), F32), pltpu.VMEM((nb, wd), F32)],
        compiler_params=_cp("arbitrary", "arbitrary"),
    )(x, g, w, cs, h0, cw, cb, wa, ba, wx, bx, lam)


def _gmlp_body(x_ref, g_ref, w_ref, gg_ref, ws_ref, bs_ref, o_ref, *, chunk):
    tm = x_ref.shape[0]
    xn = _rms(x_ref[...], g_ref[...]).astype(BF16)
    ge = jax.nn.gelu(_dot(xn, w_ref[...]))
    wd = ge.shape[1] // 2
    u = ge[:, :wd]
    vb = _rms(ge[:, wd:], gg_ref[...]).astype(BF16)
    ng = ws_ref.shape[0]
    gc = wd // ng
    tril = lax.broadcasted_iota(jnp.int32, (chunk, chunk), 0) >= lax.broadcasted_iota(jnp.int32, (chunk, chunk), 1)
    wsm = [jnp.where(tril, ws_ref[gi], 0.0).astype(BF16) for gi in range(ng)]
    for ch in range(tm // chunk):
        lo, hi = ch * chunk, (ch + 1) * chunk
        mixed = jnp.concatenate([_dot(wsm[gi], vb[lo:hi, gi * gc:(gi + 1) * gc]) for gi in range(ng)], axis=1)
        o_ref[lo:hi, :] = (u[lo:hi] * (mixed + bs_ref[...])).astype(o_ref.dtype)


def _gmlp_call(x, g, w, gg, ws, bs_full, *, tm):
    t, dm = x.shape
    wd = gg.shape[-1]
    chunk = ws.shape[-1]
    assert t % tm == 0 and tm % chunk == 0
    return pl.pallas_call(
        functools.partial(_gmlp_body, chunk=chunk), name="gmlp",
        grid=(t // tm,),
        in_specs=[pl.BlockSpec((tm, dm), lambda i: (i, 0)), _full(g.shape), _full(w.shape), _full(gg.shape),
                  _full(ws.shape), _full(bs_full.shape)],
        out_specs=pl.BlockSpec((tm, wd), lambda i: (i, 0)),
        out_shape=jax.ShapeDtypeStruct((t, wd), BF16),
        compiler_params=_cp("parallel"),
    )(x, g, w, gg, ws, bs_full)


def _gmlp_short_body(x_ref, g_ref, w_ref, gg_ref, wexp_ref, bexp_ref, o_ref, v_ref, *, n, nb):
    xn = _rms(x_ref[...], g_ref[...]).astype(BF16)
    ge = jax.nn.gelu(_dot(xn, w_ref[...]))
    wd = ge.shape[1] // 2
    u = ge[:, :wd]
    v = _rms(ge[:, wd:], gg_ref[...])
    v_ref[...] = v
    for t in range(n):
        m = bexp_ref[t:t + 1, :] + wexp_ref[t * n:t * n + 1, :] * v[0:nb]
        for s in range(1, t + 1):
            m = m + wexp_ref[t * n + s:t * n + s + 1, :] * v[s * nb:(s + 1) * nb]
        o_ref[t * nb:(t + 1) * nb, :] = (u[t * nb:(t + 1) * nb] * m).astype(o_ref.dtype)


def _gmlp_short_call(x, g, w, gg, wexp, bexp, *, n, nb):
    t, dm = x.shape
    wd = gg.shape[-1]
    return pl.pallas_call(
        functools.partial(_gmlp_short_body, n=n, nb=nb), name="gmlp_short",
        grid=(1,),
        in_specs=[_full(x.shape), _full(g.shape), _full(w.shape), _full(gg.shape), _full(wexp.shape),
                  _full(bexp.shape)],
        out_specs=[_full((t, wd)), _full((t, wd))],
        out_shape=[jax.ShapeDtypeStruct((t, wd), BF16), jax.ShapeDtypeStruct((t, wd), F32)],
        compiler_params=_cp("arbitrary"),
    )(x, g, w, gg, wexp, bexp)


def _qkv_body(x_ref, g_ref, w_ref, wck_ref, wcv_ref, *out_refs, qw, kvw, dh, seq, kt, q_scale, prompt):
    q_ref, kc_ref, vc_ref, ks_ref, vs_ref, kw_ref, vw_ref, gt_ref = out_refs[:8]
    tm = x_ref.shape[0]
    ng = kvw // dh
    xn = _rms(x_ref[...], g_ref[...]).astype(BF16)
    z = _dot(xn, w_ref[...])
    parts = [z[:, qw + j * kvw:qw + (j + 1) * kvw] for j in range(6)]
    for ref, p in zip((kc_ref, vc_ref, ks_ref, vs_ref, kw_ref, vw_ref), parts):
        for gi in range(ng):
            ref[:, gi, :] = p[:, gi * dh:(gi + 1) * dh]
    gates = jax.nn.sigmoid(z[:, qw + 6 * kvw:])
    if not prompt:
        q_ref[...] = z[:, :qw].astype(BF16)
        gt_ref[...] = gates
        return
    kcmp_ref, vcmp_ref, ka_ref, vst_ref, kwg_ref, vwt_ref = out_refs[8:]
    q_ref[...] = (z[:, :qw] * q_scale).T.astype(BF16)
    gt_ref[...] = gates.T
    nbt = tm // NSA_BLOCK
    kcmp_ref[...] = jnp.sum(parts[0].reshape(nbt, NSA_BLOCK, kvw) * wck_ref[...][None], axis=1)
    vcmp_ref[...] = jnp.sum(parts[1].reshape(nbt, NSA_BLOCK, kvw) * wcv_ref[...][None], axis=1)
    bpt = kt // NSA_BLOCK
    pos0 = lax.rem(pl.program_id(0) * tm, kt)
    blk = (pos0 + lax.broadcasted_iota(jnp.int32, (tm, bpt), 0)) // NSA_BLOCK
    onehot = (blk == lax.broadcasted_iota(jnp.int32, (tm, bpt), 1)).astype(BF16)
    ones_row = (lax.broadcasted_iota(jnp.int32, (BF16_ROWS, tm), 0) == 0).astype(BF16)
    ksb = parts[2].astype(BF16)
    kwb = parts[4].astype(BF16)
    vst = parts[3].T.astype(BF16)
    vwt_ref[...] = parts[5].T.astype(BF16)
    for gi in range(ng):
        ka_ref[gi, :, 0:dh] = ksb[:, gi * dh:(gi + 1) * dh]
        ka_ref[gi, :, dh:dh + bpt] = onehot
        kwg_ref[gi] = kwb[:, gi * dh:(gi + 1) * dh]
        vst_ref[gi, 0:dh, :] = vst[gi * dh:(gi + 1) * dh, :]
        vst_ref[gi, dh:dh + BF16_ROWS, :] = ones_row


def _qkv_call(x, g, w, wck, wcv, *, tm, qw, kvw, dh, seq, prompt):
    t, dm = x.shape
    ng = kvw // dh
    assert t % tm == 0
    row = lambda n: pl.BlockSpec((tm, n), lambda i: (i, 0))
    col = lambda n: pl.BlockSpec((n, tm), lambda i: (0, i))
    native = pl.BlockSpec((tm, ng, dh), lambda i: (i, 0, 0))
    native_shape = jax.ShapeDtypeStruct((t, ng, dh), F32)
    kt = min(KV_TILE, seq)
    scale = dh ** -0.5
    if prompt:
        assert tm % (8 * NSA_BLOCK) == 0 and seq % tm == 0 and kt % tm == 0 and tm % 128 == 0
        bpt = kt // NSA_BLOCK
        nbt = tm // NSA_BLOCK
        out_specs = [col(qw)] + [native] * 6 + [col(128)]
        out_shape = [jax.ShapeDtypeStruct((qw, t), BF16)] + [native_shape] * 6 + [jax.ShapeDtypeStruct((128, t), F32)]
        out_specs += [pl.BlockSpec((nbt, kvw), lambda i: (i, 0))] * 2
        out_specs += [pl.BlockSpec((ng, tm, dh + bpt), lambda i: (0, i, 0)),
                      pl.BlockSpec((ng, dh + BF16_ROWS, tm), lambda i: (0, 0, i)),
                      pl.BlockSpec((ng, tm, dh), lambda i: (0, i, 0)), col(kvw)]
        out_shape += [jax.ShapeDtypeStruct((t // NSA_BLOCK, kvw), F32)] * 2
        out_shape += [jax.ShapeDtypeStruct((ng, t, dh + bpt), BF16),
                      jax.ShapeDtypeStruct((ng, dh + BF16_ROWS, t), BF16),
                      jax.ShapeDtypeStruct((ng, t, dh), BF16), jax.ShapeDtypeStruct((kvw, t), BF16)]
    else:
        out_specs = [row(qw)] + [native] * 6 + [row(128)]
        out_shape = [jax.ShapeDtypeStruct((t, qw), BF16)] + [native_shape] * 6 + [jax.ShapeDtypeStruct((t, 128), F32)]
    q_scale = scale if _is_power_of_two(scale) else 1.0
    body = functools.partial(_qkv_body, qw=qw, kvw=kvw, dh=dh, seq=seq, kt=kt, q_scale=q_scale, prompt=prompt)
    return pl.pallas_call(
        body, name="qkv_prompt" if prompt else "qkv_sample",
        grid=(t // tm,),
        in_specs=[row(dm), _full(g.shape), _full(w.shape), _full(wck.shape), _full(wcv.shape)],
        out_specs=out_specs, out_shape=out_shape,
        compiler_params=_cp("parallel"),
    )(x, g, w, wck, wcv)


def _select_blocks(score, blk, nblk, count, axis):
    sel = jnp.zeros(score.shape, jnp.bool_)
    for _ in range(count):
        m = jnp.max(score, axis=axis, keepdims=True)
        idx = jnp.min(jnp.where(score == m, blk, nblk), axis=axis, keepdims=True)
        hit = blk == idx
        sel = sel | hit
        score = jnp.where(hit, M_INIT, score)
    return sel


def _nsa_cw_body(qt_ref, gt_ref, kc_ref, vct_ref, kw_ref, vwt_ref, ocw_ref, selb_ref,
                 *, seq, ng, rep, dh, s_scale):
    c = pl.program_id(1)
    qb = qt_ref.shape[1]
    nblk = kc_ref.shape[0]
    band = min(NSA_WINDOW + qb, seq)
    st = pl.multiple_of(jnp.maximum(c * qb - NSA_WINDOW, 0), qb)
    cols = rep * qb
    t_q = c * qb + lax.broadcasted_iota(jnp.int32, (1, qb), 1)
    t_c = jnp.concatenate([t_q] * rep, axis=1)
    ok_c = (lax.broadcasted_iota(jnp.int32, (nblk, cols), 0) + 1) * NSA_BLOCK - 1 <= t_c
    kp = st + lax.broadcasted_iota(jnp.int32, (band, cols), 0)
    ok_w = (kp <= t_c) & (kp > t_c - NSA_WINDOW)
    blk = lax.broadcasted_iota(jnp.int32, (nblk, qb), 0)
    cur = t_q // NSA_BLOCK
    for gi in range(ng):
        heads = [gi * rep + r for r in range(rep)]
        lanes = slice(gi * dh, (gi + 1) * dh)
        qg = jnp.concatenate([qt_ref[h * dh:(h + 1) * dh, :] for h in heads], axis=1)
        s_c = _dot(kc_ref[:, lanes].astype(BF16), qg)
        if s_scale != 1.0:
            s_c = s_c * s_scale
        p_c = _msoftmax(s_c, ok_c, axis=0)
        o_c = _dot(vct_ref[lanes, :].astype(BF16), p_c.astype(BF16))
        imp = p_c[:, 0:qb]
        for r in range(1, rep):
            imp = imp + p_c[:, r * qb:(r + 1) * qb]
        score = jnp.where(blk == cur, SEL_FORCE, jnp.where(blk < cur, imp, -1.0))
        sel = _select_blocks(score, blk, nblk, min(NSA_TOPN, nblk), axis=0)
        selb_ref[gi] = jnp.where(sel, 0.0, NEG).astype(BF16)
        s_w = _dot(kw_ref[gi, pl.ds(st, band), :], qg)
        if s_scale != 1.0:
            s_w = s_w * s_scale
        p_w = _msoftmax(s_w, ok_w, axis=0)
        o_w = _dot(vwt_ref[lanes, pl.ds(st, band)], p_w.astype(BF16))
        g_c = jnp.concatenate([gt_ref[3 * h:3 * h + 1, :] for h in heads], axis=1)
        g_w = jnp.concatenate([gt_ref[3 * h + 2:3 * h + 3, :] for h in heads], axis=1)
        ocw = g_c * o_c + g_w * o_w
        for r, h in enumerate(heads):
            ocw_ref[h * dh:(h + 1) * dh, :] = ocw[:, r * qb:(r + 1) * qb]


def _nsa_cw_call(qt, gates_t, kcmp, vcmp_t, kwg, vwt, *, nbatch, seq, ng, dh, s_scale):
    qw, t = qt.shape
    kvw = ng * dh
    rep = qw // kvw
    qb = min(Q_BLOCK, seq)
    nq = seq // qb
    nblk = seq // NSA_BLOCK
    body = functools.partial(_nsa_cw_body, seq=seq, ng=ng, rep=rep, dh=dh, s_scale=s_scale)
    qcol = lambda n: pl.BlockSpec((n, qb), lambda b, c: (0, b * nq + c))
    return pl.pallas_call(
        body, name="nsa_cmp_win",
        grid=(nbatch, nq),
        in_specs=[qcol(qw), qcol(128),
                  pl.BlockSpec((nblk, kvw), lambda b, c: (b, 0)), pl.BlockSpec((None, kvw, nblk), lambda b, c: (b, 0, 0)),
                  pl.BlockSpec((ng, seq, dh), lambda b, c: (0, b, 0)), pl.BlockSpec((kvw, seq), lambda b, c: (0, b))],
        out_specs=[qcol(qw), pl.BlockSpec((ng, None, nblk, qb), lambda b, c: (0, b * nq + c, 0, 0))],
        out_shape=[jax.ShapeDtypeStruct((qw, t), F32), jax.ShapeDtypeStruct((ng, nbatch * nq, nblk, qb), BF16)],
        compiler_params=_cp("parallel", "arbitrary"),
    )(qt, gates_t, kcmp, vcmp_t, kwg, vwt)


def _nsa_sel_body(qt_ref, selb_ref, ka_ref, vst_ref, ocw_ref, gt_ref, o_ref, m_sc, acc_sc,
                  *, rep, dh, kt, s_scale):
    g = pl.program_id(1)
    c = pl.program_id(2)
    qb = qt_ref.shape[1]
    cols = rep * qb
    bpt = kt // NSA_BLOCK
    qg = jnp.concatenate([qt_ref[r * dh:(r + 1) * dh, :] for r in range(rep)], axis=1)
    m_sc[...] = jnp.full(m_sc.shape, M_INIT, F32)
    acc_sc[...] = jnp.zeros(acc_sc.shape, F32)

    def tile(j, causal):
        st = pl.multiple_of(j * kt, kt)
        sb = selb_ref[pl.ds(pl.multiple_of(j * bpt, bpt), bpt), :]
        qa = jnp.concatenate([qg, jnp.concatenate([sb] * rep, axis=1)], axis=0)
        s = _dot(ka_ref[pl.ds(st, kt), :], qa)
        if s_scale != 1.0:
            s = s * s_scale
        if causal:
            kp = st + lax.broadcasted_iota(jnp.int32, (kt, cols), 0)
            t_c = c * qb + (lax.broadcasted_iota(jnp.int32, (kt, cols), 1) & (qb - 1))
            s = jnp.where(kp <= t_c, s, NEG)
        m_old = m_sc[...]
        m_new = jnp.maximum(m_old, jnp.max(s, axis=0, keepdims=True))
        p = jnp.exp(s - m_new)
        acc_sc[...] = jnp.exp(m_old - m_new) * acc_sc[...] + _dot(vst_ref[:, pl.ds(st, kt)], p.astype(BF16))
        m_sc[...] = m_new

    n_full = (c * qb) // kt

    def step(j, carry):
        tile(j, False)
        return carry

    lax.fori_loop(0, n_full, step, 0)
    tile(n_full, True)
    acc = acc_sc[...]
    o_s = acc[0:dh] / acc[dh:dh + 1]
    out = []
    for r in range(rep):
        g_s = gt_ref[pl.ds(3 * (g * rep + r) + 1, 1), :]
        out.append(ocw_ref[r * dh:(r + 1) * dh, :] + g_s * o_s[:, r * qb:(r + 1) * qb])
    o_ref[...] = jnp.concatenate(out, axis=0).T.astype(o_ref.dtype)


def _nsa_sel_call(qt, selb, ka, vst, ocw_t, gates_t, *, nbatch, seq, ng, dh, s_scale):
    qw, t = qt.shape
    rep = qw // (ng * dh)
    qb = min(Q_BLOCK, seq)
    assert qb & (qb - 1) == 0
    nq = seq // qb
    nblk = selb.shape[2]
    kt = min(KV_TILE, seq)
    bpt = kt // NSA_BLOCK
    assert seq % kt == 0 and kt % qb == 0 and bpt % BF16_ROWS == 0 and ka.shape[-1] == dh + bpt
    body = functools.partial(_nsa_sel_body, rep=rep, dh=dh, kt=kt, s_scale=s_scale)
    qcol = pl.BlockSpec((rep * dh, qb), lambda b, g, c: (g, b * nq + c))
    return pl.pallas_call(
        body, name="nsa_selected",
        grid=(nbatch, ng, nq),
        in_specs=[qcol,
                  pl.BlockSpec((None, None, nblk, qb), lambda b, g, c: (g, b * nq + c, 0, 0)),
                  pl.BlockSpec((None, seq, dh + bpt), lambda b, g, c: (g, b, 0)),
                  pl.BlockSpec((None, dh + BF16_ROWS, seq), lambda b, g, c: (g, 0, b)),
                  qcol,
                  pl.BlockSpec((128, qb), lambda b, g, c: (0, b * nq + c))],
        out_specs=pl.BlockSpec((qb, rep * dh), lambda b, g, c: (b * nq + c, g)),
        out_shape=jax.ShapeDtypeStruct((t, qw), BF16),
        scratch_shapes=[pltpu.VMEM((1, rep * qb), F32), pltpu.VMEM((dh + BF16_ROWS, rep * qb), F32)],
        compiler_params=_cp("parallel", "parallel", "arbitrary"),
    )(qt, selb, ka, vst, ocw_t, gates_t)


def _page_specs(pps, page, ng, dh, layer):
    return [pl.BlockSpec((None, None, page, ng, dh), lambda b, j, pt, i=i: (layer, pt[b, j * pps + i], 0, 0, 0))
            for i in range(pps)]


def _cmp_pages_body(pt_ref, *refs, pps):
    del pt_ref
    k_refs, v_refs = refs[:pps], refs[pps:2 * pps]
    wk_ref, wv_ref, kc_ref, vc_ref = refs[2 * pps:]
    page, ng, dh = k_refs[0].shape
    bpp = page // NSA_BLOCK
    for page_refs, w_ref, o_ref in ((k_refs, wk_ref, kc_ref), (v_refs, wv_ref, vc_ref)):
        w = w_ref[...][None]
        for i, r in enumerate(page_refs):
            o_ref[i * bpp:(i + 1) * bpp] = jnp.sum(r[...].reshape(bpp, NSA_BLOCK, ng, dh) * w, axis=1)


def _cmp_pages_call(page_table, pool_k, pool_v, wck, wcv, *, layer, pps):
    nbs, npg = page_table.shape
    _, _, page, ng, dh = pool_k.shape
    bpp = page // NSA_BLOCK
    assert npg % pps == 0
    wspec = pl.BlockSpec((None, NSA_BLOCK, ng, dh), lambda b, j, pt: (layer, 0, 0, 0))
    grid_spec = pltpu.PrefetchScalarGridSpec(
        num_scalar_prefetch=1, grid=(nbs, npg // pps),
        in_specs=_page_specs(pps, page, ng, dh, layer) * 2 + [wspec] * 2,
        out_specs=[pl.BlockSpec((None, pps * bpp, ng, dh), lambda b, j, pt: (b, j, 0, 0))] * 2)
    return pl.pallas_call(
        functools.partial(_cmp_pages_body, pps=pps), name="sample_cmp_pages",
        grid_spec=grid_spec,
        out_shape=[jax.ShapeDtypeStruct((nbs, npg * bpp, ng, dh), F32)] * 2,
        compiler_params=_cp("parallel", "arbitrary"),
    )(page_table, *([pool_k] * pps), *([pool_v] * pps), wck, wcv)


def _samp_cw_body(q_ref, g_ref, kc_ref, vc_ref, wk_ref, wv_ref, kn_ref, vn_ref,
                  ocw_ref, selb_ref, wko_ref, wvo_ref, *, ts, past, rep, scale):
    nblk, ng, _ = kc_ref.shape
    wb = wk_ref.shape[0]
    npad = kn_ref.shape[0]
    gr = rep * ts
    ti = lax.rem(lax.broadcasted_iota(jnp.int32, (gr, 1), 0), ts)
    t = past + ti
    ok_c = (lax.broadcasted_iota(jnp.int32, (gr, nblk), 1) + 1) * NSA_BLOCK - 1 <= t
    kp1 = past - wb + lax.broadcasted_iota(jnp.int32, (gr, wb), 1)
    j2 = lax.broadcasted_iota(jnp.int32, (gr, npad), 1)
    ok1 = (kp1 <= t) & (kp1 > t - NSA_WINDOW) & (kp1 >= 0)
    ok2 = (j2 <= ti) & (j2 < ts)
    blk = lax.broadcasted_iota(jnp.int32, (ts, nblk), 1)
    for gi in range(ng):
        rows = slice(gi * gr, (gi + 1) * gr)
        q = q_ref[rows, :]
        p_c = _msoftmax(_dot_nt(q, kc_ref[:, gi, :].astype(BF16)) * scale, ok_c)
        o_c = _dot(p_c.astype(BF16), vc_ref[:, gi, :].astype(BF16))
        imp = p_c[0:ts]
        for r in range(1, rep):
            imp = imp + p_c[r * ts:(r + 1) * ts]
        sel = _select_blocks(imp, blk, nblk, min(NSA_TOPN - 1, nblk), axis=1)
        bias = jnp.where(sel, 0.0, NEG)
        for r in range(rep):
            selb_ref[gi * gr + r * ts:gi * gr + (r + 1) * ts, :] = bias
        s1 = jnp.where(ok1, _dot_nt(q, wk_ref[:, gi, :].astype(BF16)) * scale, NEG)
        s2 = jnp.where(ok2, _dot_nt(q, kn_ref[:, gi, :].astype(BF16)) * scale, NEG)
        m = jnp.maximum(jnp.max(s1, axis=1, keepdims=True), jnp.max(s2, axis=1, keepdims=True))
        e1 = jnp.where(ok1, jnp.exp(s1 - m), 0.0)
        e2 = jnp.where(ok2, jnp.exp(s2 - m), 0.0)
        den = jnp.maximum(jnp.sum(e1, axis=1, keepdims=True) + jnp.sum(e2, axis=1, keepdims=True), 1e-30)
        o_w = (_dot((e1 / den).astype(BF16), wv_ref[:, gi, :].astype(BF16))
               + _dot((e2 / den).astype(BF16), vn_ref[:, gi, :].astype(BF16)))
        g = g_ref[rows, :]
        ocw_ref[rows, :] = g[:, 0:1] * o_c + g[:, 2:3] * o_w
    wko_ref[0:wb - ts] = wk_ref[ts:wb]
    wko_ref[wb - ts:wb] = kn_ref[0:ts]
    wvo_ref[0:wb - ts] = wv_ref[ts:wb]
    wvo_ref[wb - ts:wb] = vn_ref[0:ts]


def _samp_cw_call(q_rows, gall, kc, vc, win_k, win_v, kw_new, vw_new, *, layer, ts, past, rep):
    nbs, rows, dh = q_rows.shape
    _, nblk, ng, _ = kc.shape
    wb = win_k.shape[2]
    npad = kw_new.shape[1]
    per_b = lambda *s: pl.BlockSpec((None,) + s, lambda b: (b,) + (0,) * len(s))
    win = pl.BlockSpec((None, None, wb, ng, dh), lambda b: (layer, b, 0, 0, 0))
    body = functools.partial(_samp_cw_body, ts=ts, past=past, rep=rep, scale=dh ** -0.5)
    return pl.pallas_call(
        body, name="sample_cmp_win",
        grid=(nbs,),
        in_specs=[per_b(rows, dh), per_b(rows, 128), per_b(nblk, ng, dh), per_b(nblk, ng, dh), win, win,
                  per_b(npad, ng, dh), per_b(npad, ng, dh)],
        out_specs=[per_b(rows, dh), per_b(rows, nblk), per_b(wb, ng, dh), per_b(wb, ng, dh)],
        out_shape=[jax.ShapeDtypeStruct((nbs, rows, dh), F32), jax.ShapeDtypeStruct((nbs, rows, nblk), F32),
                   jax.ShapeDtypeStruct((nbs, wb, ng, dh), F32), jax.ShapeDtypeStruct((nbs, wb, ng, dh), F32)],
        compiler_params=_cp("parallel"),
    )(q_rows, gall, kc, vc, win_k, win_v, kw_new, vw_new)


def _samp_sel_body(pt_ref, *refs, pps, ts, rep, scale):
    del pt_ref
    k_refs, v_refs = refs[:pps], refs[pps:2 * pps]
    q_ref, selb_ref, kn_ref, vn_ref, ocw_ref, g_ref, o_ref, m_sc, l_sc, acc_sc = refs[2 * pps:]
    j = pl.program_id(1)
    nblk = selb_ref.shape[1]
    page, ng, _ = k_refs[0].shape
    kt = pps * page
    gr = rep * ts

    @pl.when(j == 0)
    def _():
        m_sc[...] = jnp.full(m_sc.shape, M_INIT, F32)
        l_sc[...] = jnp.zeros(l_sc.shape, F32)
        acc_sc[...] = jnp.zeros(acc_sc.shape, F32)

    def update(rows, s, v):
        m_old = m_sc[rows, :]
        m_new = jnp.maximum(m_old, jnp.max(s, axis=1, keepdims=True))
        alpha = jnp.exp(m_old - m_new)
        p = jnp.exp(s - m_new)
        l_sc[rows, :] = alpha * l_sc[rows, :] + jnp.sum(p, axis=1, keepdims=True)
        acc_sc[rows, :] = alpha * acc_sc[rows, :] + _dot(p.astype(BF16), v)
        m_sc[rows, :] = m_new

    key_blk = j * (kt // NSA_BLOCK) + lax.broadcasted_iota(jnp.int32, (nblk, kt), 1) // NSA_BLOCK
    onehot = (key_blk == lax.broadcasted_iota(jnp.int32, (nblk, kt), 0)).astype(BF16)
    for gi in range(ng):
        rows = slice(gi * gr, (gi + 1) * gr)
        kg = jnp.concatenate([r[:, gi, :] for r in k_refs], axis=0).astype(BF16)
        vg = jnp.concatenate([r[:, gi, :] for r in v_refs], axis=0).astype(BF16)
        bias = _dot(selb_ref[rows, :].astype(BF16), onehot)
        update(rows, _dot_nt(q_ref[rows, :], kg) * scale + bias, vg)

    @pl.when(j == pl.num_programs(1) - 1)
    def _():
        npad = kn_ref.shape[0]
        ti = lax.rem(lax.broadcasted_iota(jnp.int32, (gr, npad), 0), ts)
        j2 = lax.broadcasted_iota(jnp.int32, (gr, npad), 1)
        ok = (j2 <= ti) & (j2 < ts)
        for gi in range(ng):
            rows = slice(gi * gr, (gi + 1) * gr)
            s = jnp.where(ok, _dot_nt(q_ref[rows, :], kn_ref[:, gi, :].astype(BF16)) * scale, NEG)
            update(rows, s, vn_ref[:, gi, :].astype(BF16))
        o_ref[...] = ocw_ref[...] + g_ref[:, 1:2] * (acc_sc[...] / l_sc[...])


def _samp_sel_call(page_table, pool_k, pool_v, q_rows, selb, ks_new, vs_new, ocw, gall, *, layer, pps, ts, rep):
    nbs, npg = page_table.shape
    _, _, page, ng, dh = pool_k.shape
    rows = q_rows.shape[1]
    per_b = lambda *s: pl.BlockSpec((None,) + s, lambda b, j, pt: (b,) + (0,) * len(s))
    npad = ks_new.shape[1]
    grid_spec = pltpu.PrefetchScalarGridSpec(
        num_scalar_prefetch=1, grid=(nbs, npg // pps),
        in_specs=_page_specs(pps, page, ng, dh, layer) * 2 + [
            per_b(rows, dh), per_b(rows, selb.shape[2]), per_b(npad, ng, dh), per_b(npad, ng, dh),
            per_b(rows, dh), per_b(rows, 128)],
        out_specs=per_b(rows, dh),
        scratch_shapes=[pltpu.VMEM((rows, 1), F32), pltpu.VMEM((rows, 1), F32), pltpu.VMEM((rows, dh), F32)])
    body = functools.partial(_samp_sel_body, pps=pps, ts=ts, rep=rep, scale=dh ** -0.5)
    return pl.pallas_call(
        body, name="sample_selected",
        grid_spec=grid_spec,
        out_shape=jax.ShapeDtypeStruct((nbs, rows, dh), F32),
        compiler_params=_cp("parallel", "arbitrary"),
    )(page_table, *([pool_k] * pps), *([pool_v] * pps), q_rows, selb, ks_new, vs_new, ocw, gall)


def _merge_body(x_ref, g_ref, wmg_ref, oa_ref, ob_ref, oc_ref, wb_ref, wo_ref, o_ref):
    x = x_ref[...]
    dm = x.shape[1]
    xn = _rms(x, g_ref[...]).astype(BF16)
    mg = jax.nn.sigmoid(_dot(xn, wmg_ref[...]))
    y = mg[:, 0:dm] * _dot(oa_ref[...], wb_ref[0])
    y = y + mg[:, dm:2 * dm] * _dot(ob_ref[...], wb_ref[1])
    y = y + mg[:, 2 * dm:3 * dm] * _dot(oc_ref[...], wb_ref[2])
    o_ref[...] = x + _dot(y.astype(BF16), wo_ref[...])


def _merge_call(x, g, wmg, oa, ob, oc, wb, wo, *, tm):
    t, dm = x.shape
    row = pl.BlockSpec((tm, dm), lambda i: (i, 0))
    return pl.pallas_call(
        _merge_body, name="merge",
        grid=(t // tm,),
        in_specs=[row, _full(g.shape), _full(wmg.shape), row, row, row, _full(wb.shape), _full(wo.shape)],
        out_specs=row,
        out_shape=jax.ShapeDtypeStruct((t, dm), F32),
        compiler_params=_cp("parallel"),
    )(x, g, wmg, oa, ob, oc, wb, wo)


def _ffn_body(x_ref, g_ref, w1_ref, w3_ref, w2_ref, fg_ref, o_ref, xn_sc, acc_sc, *, final):
    f = pl.program_id(1)

    @pl.when(f == 0)
    def _():
        xn_sc[...] = _rms(x_ref[...], g_ref[...]).astype(BF16)
        acc_sc[...] = jnp.zeros(acc_sc.shape, F32)

    xn = xn_sc[...]
    h = jax.nn.silu(_dot(xn, w1_ref[...])) * _dot(xn, w3_ref[...])
    acc_sc[...] += _dot(h.astype(BF16), w2_ref[...])

    @pl.when(f == pl.num_programs(1) - 1)
    def _():
        y = x_ref[...] + acc_sc[...]
        o_ref[...] = _rms(y, fg_ref[...]) if final else y


def _ffn_tile(f_dim):
    half = f_dim // 2
    return half if f_dim % 2 == 0 and half % 128 == 0 else f_dim


def _ffn_call(x, g, w1, w3, w2, fg, *, tm, final):
    t, dm = x.shape
    fd = w1.shape[1]
    ft = _ffn_tile(fd)
    row = pl.BlockSpec((tm, dm), lambda i, f: (i, 0))
    vec = pl.BlockSpec((1, dm), lambda i, f: (0, 0))
    return pl.pallas_call(
        functools.partial(_ffn_body, final=final), name="ffn",
        grid=(t // tm, fd // ft),
        in_specs=[row, vec, pl.BlockSpec((dm, ft), lambda i, f: (0, f)), pl.BlockSpec((dm, ft), lambda i, f: (0, f)),
                  pl.BlockSpec((ft, dm), lambda i, f: (f, 0)), vec],
        out_specs=row,
        out_shape=jax.ShapeDtypeStruct((t, dm), F32),
        scratch_shapes=[pltpu.VMEM((tm, dm), BF16), pltpu.VMEM((tm, dm), F32)],
        compiler_params=_cp("parallel", "arbitrary"),
    )(x, g, w1, w3, w2, fg)


def _moe_body(x_ref, g_ref, wr_ref, br_ref, w1_ref, w3_ref, w2_ref, fg_ref, o_ref, xn_sc, comb_sc, acc_sc, *, final):
    e = pl.program_id(1)
    f = pl.program_id(2)

    @pl.when((e == 0) & (f == 0))
    def _():
        xn = _rms(x_ref[...], g_ref[...])
        xn_sc[...] = xn.astype(BF16)
        acc_sc[...] = jnp.zeros(acc_sc.shape, F32)
        logits = jnp.dot(xn, wr_ref[...], preferred_element_type=F32, precision=lax.Precision.HIGHEST) + br_ref[...]
        lane = lax.broadcasted_iota(jnp.int32, logits.shape, 1)
        nl = logits.shape[1]
        m1 = jnp.max(logits, axis=1, keepdims=True)
        i1 = jnp.min(jnp.where(logits == m1, lane, nl), axis=1, keepdims=True)
        rest = jnp.where(lane == i1, M_INIT, logits)
        m2 = jnp.max(rest, axis=1, keepdims=True)
        i2 = jnp.min(jnp.where(rest == m2, lane, nl), axis=1, keepdims=True)
        e2 = jnp.exp(m2 - m1)
        den = 1.0 + e2
        comb_sc[...] = jnp.where(lane == i1, 1.0 / den, 0.0) + jnp.where(lane == i2, e2 / den, 0.0)

    xn = xn_sc[...]
    h = jax.nn.silu(_dot(xn, w1_ref[...])) * _dot(xn, w3_ref[...])
    comb = comb_sc[...]
    lane = lax.broadcasted_iota(jnp.int32, comb.shape, 1)
    ce = jnp.sum(jnp.where(lane == e, comb, 0.0), axis=1, keepdims=True)
    acc_sc[...] += ce * _dot(h.astype(BF16), w2_ref[...])

    @pl.when((e == pl.num_programs(1) - 1) & (f == pl.num_programs(2) - 1))
    def _():
        y = x_ref[...] + acc_sc[...]
        o_ref[...] = _rms(y, fg_ref[...]) if final else y


def _moe_call(x, g, wr, br, w1, w3, w2, fg, *, tm, final):
    t, dm = x.shape
    ne, _, fd = w1.shape
    ft = _ffn_tile(fd)
    row = pl.BlockSpec((tm, dm), lambda i, e, f: (i, 0))
    vec = lambda n: pl.BlockSpec((1, n), lambda i, e, f: (0, 0))
    return pl.pallas_call(
        functools.partial(_moe_body, final=final), name="moe",
        grid=(t // tm, ne, fd // ft),
        in_specs=[row, vec(dm), pl.BlockSpec(wr.shape, lambda i, e, f: (0, 0)), vec(wr.shape[1]),
                  pl.BlockSpec((None, dm, ft), lambda i, e, f: (e, 0, f)),
                  pl.BlockSpec((None, dm, ft), lambda i, e, f: (e, 0, f)),
                  pl.BlockSpec((None, ft, dm), lambda i, e, f: (e, f, 0)), vec(dm)],
        out_specs=row,
        out_shape=jax.ShapeDtypeStruct((t, dm), F32),
        scratch_shapes=[pltpu.VMEM((tm, dm), BF16), pltpu.VMEM((tm, wr.shape[1]), F32), pltpu.VMEM((tm, dm), F32)],
        compiler_params=_cp("parallel", "arbitrary", "arbitrary"),
    )(x, g, wr, br, w1, w3, w2, fg)


def kernel(x_prompt, x_sample, cache_cmp_k, cache_cmp_v, cache_slc_k, cache_slc_v, cache_win_k, cache_win_v, state_conv, state_lru, page_table, norm_mix_g, w_in, lru_conv_w, lru_conv_b, lru_wa, lru_ba, lru_wx, lru_bx, lru_lambda, nsa_cmp_wk, nsa_cmp_wv, gm_norm_g, gm_ws, gm_bs, w_branch, w_out, norm_ffn_g, ffn_w1, ffn_w3, ffn_w2, moe_wr, moe_br, moe_w1, moe_w3, moe_w2, final_norm_g):
    depth, dm, _ = w_in.shape
    nbp, seq, _ = x_prompt.shape
    nbs, ts, _ = x_sample.shape
    _, n_pool, page, ng, dh = cache_cmp_k.shape
    wd = state_lru.shape[-1]
    conv_w = lru_conv_w.shape[1]
    gmw = gm_norm_g.shape[-1]
    n_gm, chunk = gm_ws.shape[1], gm_ws.shape[2]
    kvw = ng * dh
    qw = NSA_HEADS * dh
    rep = NSA_HEADS // ng
    npg = page_table.shape[1]
    past = npg * page
    wbuf = cache_win_k.shape[2]
    tp, tsm = nbp * seq, nbs * ts
    ne = moe_wr.shape[-1]
    nblk = seq // NSA_BLOCK
    scale = dh ** -0.5
    s_scale = 1.0 if _is_power_of_two(scale) else scale

    assert seq % chunk == 0 and seq % LRU_CHUNK == 0 and seq >= NSA_WINDOW
    assert past % NSA_BLOCK == 0 and ts < NSA_BLOCK and ts < chunk and wbuf == NSA_WINDOW and nbs % 8 == 0
    assert past // NSA_BLOCK >= NSA_TOPN - 1

    offs = [0, wd, 2 * wd, 2 * wd + qw, 2 * wd + qw + 6 * kvw, 2 * wd + qw + 6 * kvw + 3 * NSA_HEADS]
    offs += [offs[-1] + 2 * gmw, offs[-1] + 2 * gmw + 3 * dm]
    w_in_b = w_in.astype(BF16)
    w_lru = w_in_b[:, :, 0:offs[2]]
    w_nsa = jnp.pad(w_in_b[:, :, offs[2]:offs[5]], ((0, 0), (0, 0), (0, 128 - 3 * NSA_HEADS)))
    w_uv = w_in_b[:, :, offs[5]:offs[6]]
    w_mg = w_in_b[:, :, offs[6]:offs[7]]
    wa_b, wx_b = lru_wa.astype(BF16), lru_wx.astype(BF16)
    wb_b, wo_b = w_branch.astype(BF16), w_out.astype(BF16)
    f1_b, f3_b, f2_b = ffn_w1.astype(BF16), ffn_w3.astype(BF16), ffn_w2.astype(BF16)
    m1_b, m3_b, m2_b = moe_w1.astype(BF16), moe_w3.astype(BF16), moe_w2.astype(BF16)
    wr_pad = jnp.pad(moe_wr, ((0, 0), (0, 0), (0, 128 - ne)))
    br_pad = jnp.pad(moe_br, ((0, 0), (0, 128 - ne)), constant_values=NEG)[:, None, :]
    wck = nsa_cmp_wk.reshape(depth, NSA_BLOCK, kvw)
    wcv = nsa_cmp_wv.reshape(depth, NSA_BLOCK, kvw)
    gc = gmw // n_gm
    bs_full = jnp.repeat(jnp.swapaxes(gm_bs, 1, 2), gc, axis=2)
    ws_short = jnp.repeat(jnp.transpose(gm_ws[:, :, :ts, :ts], (0, 2, 3, 1)).reshape(depth, ts * ts, n_gm), gc, axis=2)
    vec = lambda a, l: a[l][None, :]
    pps = min(PAGES_PER_STEP, npg)

    tm_p = min(ROW_TILE, tp)
    tm_s = tsm
    xp = x_prompt.reshape(tp, dm)
    xs = x_sample.reshape(tsm, dm)
    zeros_cs = jnp.zeros((nbp, conv_w - 1, wd), F32)
    zeros_h = jnp.zeros((nbp, 1, wd), F32)
    to_tm = lambda a, n: jnp.swapaxes(a.reshape(nbs, n, -1), 0, 1).reshape(1, n * nbs, -1)
    from_tm = lambda a, n: jnp.swapaxes(a.reshape(n, nbs, -1), 0, 1)
    npad = -(-ts // 8) * 8
    new_rows = lambda a: jnp.pad(a.reshape(nbs, ts, ng, dh), ((0, 0), (0, npad - ts), (0, 0), (0, 0)))

    p_out = [[] for _ in range(8)]
    s_out = [[] for _ in range(9)]
    for l in range(depth):
        last = l == depth - 1
        gmix = vec(norm_mix_g, l)
        lru_w = (lru_conv_w[l], vec(lru_conv_b, l), wa_b[l], vec(lru_ba, l), wx_b[l], vec(lru_bx, l), vec(lru_lambda, l))

        o_a, conv_new, h_last = _lru_call(xp.reshape(nbp, seq, dm), gmix, w_lru[l], zeros_cs, zeros_h, *lru_w,
                                          nb=1, tch=min(LRU_CHUNK, seq), starts_at_zero=True)
        o_c = _gmlp_call(xp, gmix, w_uv[l], vec(gm_norm_g, l), gm_ws[l], bs_full[l], tm=tm_p)
        (qt, k_c, v_c, k_s, v_s, k_w, v_w, gates_t, kcmp, vcmp, ka, vst, kwg, vwt) = _qkv_call(
            xp, gmix, w_nsa[l], wck[l], wcv[l], tm=tm_p, qw=qw, kvw=kvw, dh=dh, seq=seq, prompt=True)
        vcmp_t = jnp.swapaxes(vcmp.reshape(nbp, nblk, kvw), 1, 2)
        ocw_t, selb = _nsa_cw_call(qt, gates_t, kcmp, vcmp_t, kwg, vwt, nbatch=nbp, seq=seq, ng=ng, dh=dh,
                                   s_scale=s_scale)
        o_b = _nsa_sel_call(qt, selb, ka, vst, ocw_t, gates_t, nbatch=nbp, seq=seq, ng=ng, dh=dh, s_scale=s_scale)
        xp = _merge_call(xp, gmix, w_mg[l], o_a.reshape(tp, wd), o_b, o_c, wb_b[l], wo_b[l], tm=tm_p)
        if l % 2 == 0:
            xp = _ffn_call(xp, vec(norm_ffn_g, l), f1_b[l // 2], f3_b[l // 2], f2_b[l // 2], final_norm_g[None, :],
                           tm=tm_p, final=last)
        else:
            xp = _moe_call(xp, vec(norm_ffn_g, l), wr_pad[l // 2], br_pad[l // 2], m1_b[l // 2], m3_b[l // 2],
                           m2_b[l // 2], final_norm_g[None, :], tm=tm_p, final=last)
        keep = min(NSA_WINDOW, seq)
        shp = (nbp, seq, ng, dh)
        for lst, a in zip(p_out, (k_c.reshape(shp), v_c.reshape(shp), k_s.reshape(shp), v_s.reshape(shp),
                                  k_w.reshape(shp)[:, seq - keep:], v_w.reshape(shp)[:, seq - keep:],
                                  conv_new, h_last.reshape(nbp, wd))):
            lst.append(a)

        xs_tm = to_tm(xs, ts)
        o_a, conv_new, h_last = _lru_call(xs_tm, gmix, w_lru[l], to_tm(state_conv[l], conv_w - 1),
                                          state_lru[l][None], *lru_w, nb=nbs, tch=ts, starts_at_zero=False)
        o_a = from_tm(o_a, ts).reshape(tsm, wd)
        o_c, gm_v = _gmlp_short_call(xs_tm[0], gmix, w_uv[l], vec(gm_norm_g, l), ws_short[l], bs_full[l][:ts],
                                     n=ts, nb=nbs)
        o_c = from_tm(o_c, ts).reshape(tsm, gmw)
        q, k_c, v_c, k_s, v_s, k_w, v_w, gates = _qkv_call(
            xs, gmix, w_nsa[l], wck[l], wcv[l], tm=tm_s, qw=qw, kvw=kvw, dh=dh, seq=ts, prompt=False)
        q_rows = jnp.transpose(q.reshape(nbs, ts, ng, rep, dh), (0, 2, 3, 1, 4)).reshape(nbs, ng * rep * ts, dh)
        g5 = jnp.transpose(gates[:, :3 * NSA_HEADS].reshape(nbs, ts, ng, rep, 3), (0, 2, 3, 1, 4))
        gall = jnp.pad(g5.reshape(nbs, ng * rep * ts, 3), ((0, 0), (0, 0), (0, 125)))
        kc_past, vc_past = _cmp_pages_call(page_table, cache_cmp_k, cache_cmp_v, nsa_cmp_wk, nsa_cmp_wv,
                                           layer=l, pps=pps)
        ocw, selb, win_k_new, win_v_new = _samp_cw_call(
            q_rows, gall, kc_past, vc_past, cache_win_k, cache_win_v, new_rows(k_w), new_rows(v_w),
            layer=l, ts=ts, past=past, rep=rep)
        o_rows = _samp_sel_call(page_table, cache_slc_k, cache_slc_v, q_rows, selb, new_rows(k_s), new_rows(v_s),
                                ocw, gall, layer=l, pps=pps, ts=ts, rep=rep)
        o_b = jnp.transpose(o_rows.reshape(nbs, ng, rep, ts, dh), (0, 3, 1, 2, 4)).reshape(tsm, qw).astype(BF16)
        xs = _merge_call(xs, gmix, w_mg[l], o_a, o_b, o_c, wb_b[l], wo_b[l], tm=tm_s)
        if l % 2 == 0:
            xs = _ffn_call(xs, vec(norm_ffn_g, l), f1_b[l // 2], f3_b[l // 2], f2_b[l // 2], final_norm_g[None, :],
                           tm=tm_s, final=last)
        else:
            xs = _moe_call(xs, vec(norm_ffn_g, l), wr_pad[l // 2], br_pad[l // 2], m1_b[l // 2], m3_b[l // 2],
                           m2_b[l // 2], final_norm_g[None, :], tm=tm_s, final=last)
        shs = (nbs, ts, ng, dh)
        for lst, a in zip(s_out, (k_c.reshape(shs), v_c.reshape(shs), k_s.reshape(shs), v_s.reshape(shs),
                                  win_k_new, win_v_new,
                                  from_tm(conv_new[0], conv_w - 1), h_last[0], from_tm(gm_v, ts))):
            lst.append(a)

    outs = [xp.reshape(nbp, seq, dm), xs.reshape(nbs, ts, dm)]
    outs += [jnp.stack(a) for a in p_out]
    outs += [jnp.stack(a) for a in s_out]
    return tuple(outs)
```

```python
import functools
import math

import jax
import jax.numpy as jnp
from jax import lax
from jax.experimental import pallas as pl
from jax.experimental.pallas import tpu as pltpu

F32 = jnp.float32
BF16 = jnp.bfloat16

NSA_HEADS = 16
NSA_BLOCK = 64
NSA_TOPN = 8
NSA_WINDOW = 512
SEL_FORCE = 1.0e4
LRU_C = 8.0
TOP_K = 2
RMS_EPS = 1e-6
NEG = -1e30
M_INIT = -3.0e38

Q_BLOCK = 128
KV_TILE = 1024
LRU_CHUNK = 256
ROW_TILE = 512
PAGES_PER_STEP = 8
BF16_ROWS = 16
LANES = 128
VMEM_LIMIT = 56 * 2 ** 20


def _cp(*sem):
    return pltpu.CompilerParams(dimension_semantics=sem, vmem_limit_bytes=VMEM_LIMIT)


def _dot(a, b):
    return jnp.dot(a, b, preferred_element_type=F32)


def _dot_nt(a, b):
    return lax.dot_general(a, b, (((1,), (1,)), ((), ())), preferred_element_type=F32)


def _rms(x, g):
    return x * lax.rsqrt(jnp.mean(x * x, axis=-1, keepdims=True) + RMS_EPS) * g


def _msoftmax(s, mask, axis=-1):
    s = jnp.where(mask, s, NEG)
    e = jnp.where(mask, jnp.exp(s - jnp.max(s, axis=axis, keepdims=True)), 0.0)
    return e / jnp.maximum(jnp.sum(e, axis=axis, keepdims=True), 1e-30)


def _is_power_of_two(x):
    return math.frexp(x)[0] == 0.5


def _full(shape):
    n = len(shape)
    return pl.BlockSpec(shape, lambda *_: (0,) * n)


def _lru_body(x_ref, g_ref, w_ref, cs_ref, h0_ref, cw_ref, cb_ref, wa_ref, ba_ref, wx_ref, bx_ref, lam_ref,
              oa_ref, cn_ref, hl_ref, xbuf, hc, *, nb, tch, conv_w, starts_at_zero):
    c = pl.program_id(1)
    rows = nb * tch
    wd = hc.shape[-1]
    nst = (conv_w - 1) * nb
    cr = xbuf.shape[0] - rows
    nh = wa_ref.shape[0]
    bw = wd // nh

    @pl.when(c == 0)
    def _():
        if cr > nst:
            xbuf[0:cr - nst, :] = jnp.zeros((cr - nst, wd), F32)
        xbuf[cr - nst:cr, :] = cs_ref[...]
        hc[...] = h0_ref[...]

    xn = _rms(x_ref[...], g_ref[...]).astype(BF16)
    z = _dot(xn, w_ref[...])
    gate = z[:, wd:]
    xbuf[cr:cr + rows, :] = z[:, :wd]
    y = cb_ref[...] + xbuf[cr - nst:cr - nst + rows, :] * cw_ref[0:1, :]
    for k in range(1, conv_w):
        st = cr - nst + k * nb
        y = y + xbuf[st:st + rows, :] * cw_ref[k:k + 1, :]
    cn_ref[...] = xbuf[cr + rows - nst:cr + rows, :]
    xbuf[0:cr, :] = xbuf[rows:rows + cr, :]

    yb = y.astype(BF16)

    def block_diag(wr):
        return jnp.concatenate([_dot(yb[:, j * bw:(j + 1) * bw], wr[j]) for j in range(nh)], axis=1)

    r = jax.nn.sigmoid(block_diag(wa_ref) + ba_ref[...])
    i = jax.nn.sigmoid(block_diag(wx_ref) + bx_ref[...])
    lam = lam_ref[...]
    softplus_neg_lam = jnp.maximum(-lam, 0.0) + jnp.log1p(jnp.exp(-jnp.abs(lam)))
    a = jnp.exp((-LRU_C) * r * softplus_neg_lam)
    mult = jnp.sqrt(1.0 - a * a)
    row = lax.broadcasted_iota(jnp.int32, (rows, wd), 0)
    if starts_at_zero:
        mult = jnp.where((row < nb) & (c == 0), 1.0, mult)
    u = mult * (i * y)
    if nb == 1:
        u = u + jnp.where(row < 1, a * hc[...], 0.0)
    else:
        u = jnp.concatenate([u[:nb] + a[:nb] * hc[...], u[nb:]], axis=0)
    d = nb
    while d < rows:
        a_sh = jnp.where(row < d, 1.0, pltpu.roll(a, d, axis=0))
        u_sh = jnp.where(row < d, 0.0, pltpu.roll(u, d, axis=0))
        u = a * u_sh + u
        a = a * a_sh
        d *= 2
    hc[...] = u[rows - nb:rows, :]
    hl_ref[...] = u[rows - nb:rows, :]
    oa_ref[...] = (jax.nn.gelu(gate) * u).astype(oa_ref.dtype)


def _lru_call(x, g, w, cs, h0, cw, cb, wa, ba, wx, bx, lam, *, nb, tch, starts_at_zero):
    nbatch, rows_total, dm = x.shape
    wd = h0.shape[-1]
    conv_w = cw.shape[0]
    nst = (conv_w - 1) * nb
    rows = nb * tch
    cr = -(-nst // 8) * 8
    assert rows_total % rows == 0 and rows >= cr
    body = functools.partial(_lru_body, nb=nb, tch=tch, conv_w=conv_w, starts_at_zero=starts_at_zero)
    return pl.pallas_call(
        body, name="lru",
        grid=(nbatch, rows_total // rows),
        in_specs=[pl.BlockSpec((None, rows, dm), lambda b, c: (b, c, 0)),
                  _full(g.shape), _full(w.shape),
                  pl.BlockSpec((None, nst, wd), lambda b, c: (b, 0, 0)),
                  pl.BlockSpec((None, nb, wd), lambda b, c: (b, 0, 0)),
                  _full(cw.shape), _full(cb.shape), _full(wa.shape), _full(ba.shape), _full(wx.shape),
                  _full(bx.shape), _full(lam.shape)],
        out_specs=[pl.BlockSpec((None, rows, wd), lambda b, c: (b, c, 0)),
                   pl.BlockSpec((None, nst, wd), lambda b, c: (b, 0, 0)),
                   pl.BlockSpec((None, nb, wd), lambda b, c: (b, 0, 0))],
        out_shape=[jax.ShapeDtypeStruct((nbatch, rows_total, wd), BF16),
                   jax.ShapeDtypeStruct((nbatch, nst, wd), F32),
                   jax.ShapeDtypeStruct((nbatch, nb, wd), F32)],
        scratch_shapes=[pltpu.VMEM((cr + rows, wd), F32), pltpu.VMEM((nb, wd), F32)],
        compiler_params=_cp("arbitrary", "arbitrary"),
    )(x, g, w, cs, h0, cw, cb, wa, ba, wx, bx, lam)


def _gmlp_body(x_ref, g_ref, w_ref, gg_ref, ws_ref, bs_ref, o_ref, *, chunk):
    tm = x_ref.shape[0]
    xn = _rms(x_ref[...], g_ref[...]).astype(BF16)
    ge = jax.nn.gelu(_dot(xn, w_ref[...]))
    wd = ge.shape[1] // 2
    u = ge[:, :wd]
    vb = _rms(ge[:, wd:], gg_ref[...]).astype(BF16)
    ng = ws_ref.shape[0]
    gc = wd // ng
    tril = lax.broadcasted_iota(jnp.int32, (chunk, chunk), 0) >= lax.broadcasted_iota(jnp.int32, (chunk, chunk), 1)
    wsm = [jnp.where(tril, ws_ref[gi], 0.0).astype(BF16) for gi in range(ng)]
    for ch in range(tm // chunk):
        lo, hi = ch * chunk, (ch + 1) * chunk
        mixed = jnp.concatenate([_dot(wsm[gi], vb[lo:hi, gi * gc:(gi + 1) * gc]) for gi in range(ng)], axis=1)
        o_ref[lo:hi, :] = (u[lo:hi] * (mixed + bs_ref[...])).astype(o_ref.dtype)


def _gmlp_call(x, g, w, gg, ws, bs_full, *, tm):
    t, dm = x.shape
    wd = gg.shape[-1]
    chunk = ws.shape[-1]
    assert t % tm == 0 and tm % chunk == 0
    return pl.pallas_call(
        functools.partial(_gmlp_body, chunk=chunk), name="gmlp",
        grid=(t // tm,),
        in_specs=[pl.BlockSpec((tm, dm), lambda i: (i, 0)), _full(g.shape), _full(w.shape), _full(gg.shape),
                  _full(ws.shape), _full(bs_full.shape)],
        out_specs=pl.BlockSpec((tm, wd), lambda i: (i, 0)),
        out_shape=jax.ShapeDtypeStruct((t, wd), BF16),
        compiler_params=_cp("parallel"),
    )(x, g, w, gg, ws, bs_full)


def _gmlp_short_body(x_ref, g_ref, w_ref, gg_ref, wexp_ref, bexp_ref, o_ref, v_ref, *, n, nb):
    xn = _rms(x_ref[...], g_ref[...]).astype(BF16)
    ge = jax.nn.gelu(_dot(xn, w_ref[...]))
    wd = ge.shape[1] // 2
    u = ge[:, :wd]
    v = _rms(ge[:, wd:], gg_ref[...])
    v_ref[...] = v
    for t in range(n):
        m = bexp_ref[t:t + 1, :] + wexp_ref[t * n:t * n + 1, :] * v[0:nb]
        for s in range(1, t + 1):
            m = m + wexp_ref[t * n + s:t * n + s + 1, :] * v[s * nb:(s + 1) * nb]
        o_ref[t * nb:(t + 1) * nb, :] = (u[t * nb:(t + 1) * nb] * m).astype(o_ref.dtype)


def _gmlp_short_call(x, g, w, gg, wexp, bexp, *, n, nb):
    t, dm = x.shape
    wd = gg.shape[-1]
    return pl.pallas_call(
        functools.partial(_gmlp_short_body, n=n, nb=nb), name="gmlp_short",
        grid=(1,),
        in_specs=[_full(x.shape), _full(g.shape), _full(w.shape), _full(gg.shape), _full(wexp.shape),
                  _full(bexp.shape)],
        out_specs=[_full((t, wd)), _full((t, wd))],
        out_shape=[jax.ShapeDtypeStruct((t, wd), BF16), jax.ShapeDtypeStruct((t, wd), F32)],
        compiler_params=_cp("arbitrary"),
    )(x, g, w, gg, wexp, bexp)


def _qkv_prompt_body(x_ref, g_ref, w_ref, wck_ref, wcv_ref, *refs, qw, kvw, dh, kt, q_scale, n_alias):
    (q_ref, gt_ref, kcmp_ref, vcmp_ref, ka_ref, vst_ref, kwg_ref, vwt_ref) = refs[n_alias:n_alias + 8]
    stack_refs = refs[n_alias + 8:]
    tm = x_ref.shape[0]
    ng = kvw // dh
    xn = _rms(x_ref[...], g_ref[...]).astype(BF16)
    z = _dot(xn, w_ref[...])
    parts = [z[:, qw + j * kvw:qw + (j + 1) * kvw] for j in range(6)]
    parts_t = [p.T for p in parts]
    for ref, p in zip(stack_refs, parts_t):
        ref[...] = p
    q_ref[...] = (z[:, :qw] * q_scale).T.astype(BF16)
    gt_ref[...] = jax.nn.sigmoid(z[:, qw + 6 * kvw:]).T
    nbt = tm // NSA_BLOCK
    kcmp_ref[...] = jnp.sum(parts[0].reshape(nbt, NSA_BLOCK, kvw) * wck_ref[...][None], axis=1)
    vcmp_ref[...] = jnp.sum(parts[1].reshape(nbt, NSA_BLOCK, kvw) * wcv_ref[...][None], axis=1)
    bpt = kt // NSA_BLOCK
    pos0 = lax.rem(pl.program_id(0) * tm, kt)
    blk = (pos0 + lax.broadcasted_iota(jnp.int32, (tm, bpt), 0)) // NSA_BLOCK
    onehot = (blk == lax.broadcasted_iota(jnp.int32, (tm, bpt), 1)).astype(BF16)
    ones_row = (lax.broadcasted_iota(jnp.int32, (BF16_ROWS, tm), 0) == 0).astype(BF16)
    ksb = parts[2].astype(BF16)
    kwb = parts[4].astype(BF16)
    vst = parts_t[3].astype(BF16)
    vwt_ref[...] = parts_t[5].astype(BF16)
    for gi in range(ng):
        ka_ref[gi, :, 0:dh] = ksb[:, gi * dh:(gi + 1) * dh]
        ka_ref[gi, :, dh:dh + bpt] = onehot
        kwg_ref[gi] = kwb[:, gi * dh:(gi + 1) * dh]
        vst_ref[gi, 0:dh, :] = vst[gi * dh:(gi + 1) * dh, :]
        vst_ref[gi, dh:dh + BF16_ROWS, :] = ones_row


def _qkv_prompt_call(x, g, w, wck, wcv, stacks, *, layer, depth, nbatch, seq, tm, qw, kvw, dh):
    t, dm = x.shape
    ng = kvw // dh
    kt = min(KV_TILE, seq)
    bpt = kt // NSA_BLOCK
    nbt = tm // NSA_BLOCK
    nst = seq // tm
    assert t % tm == 0 and tm % (8 * NSA_BLOCK) == 0 and seq % tm == 0 and kt % tm == 0 and tm % LANES == 0
    row = lambda n: pl.BlockSpec((tm, n), lambda i: (i, 0))
    col = lambda n: pl.BlockSpec((n, tm), lambda i: (0, i))
    stack_spec = pl.BlockSpec((None, None, kvw, tm), lambda i: (layer, i // nst, 0, i % nst))
    stack_shape = jax.ShapeDtypeStruct((depth, nbatch, kvw, seq), F32)
    out_specs = [col(qw), col(LANES), pl.BlockSpec((nbt, kvw), lambda i: (i, 0)), pl.BlockSpec((nbt, kvw), lambda i: (i, 0)),
                 pl.BlockSpec((ng, tm, dh + bpt), lambda i: (0, i, 0)),
                 pl.BlockSpec((ng, dh + BF16_ROWS, tm), lambda i: (0, 0, i)),
                 pl.BlockSpec((ng, tm, dh), lambda i: (0, i, 0)), col(kvw)] + [stack_spec] * 6
    out_shape = [jax.ShapeDtypeStruct((qw, t), BF16), jax.ShapeDtypeStruct((LANES, t), F32),
                 jax.ShapeDtypeStruct((t // NSA_BLOCK, kvw), F32), jax.ShapeDtypeStruct((t // NSA_BLOCK, kvw), F32),
                 jax.ShapeDtypeStruct((ng, t, dh + bpt), BF16), jax.ShapeDtypeStruct((ng, dh + BF16_ROWS, t), BF16),
                 jax.ShapeDtypeStruct((ng, t, dh), BF16), jax.ShapeDtypeStruct((kvw, t), BF16)] + [stack_shape] * 6
    scale = dh ** -0.5
    q_scale = scale if _is_power_of_two(scale) else 1.0
    n_alias = len(stacks)
    body = functools.partial(_qkv_prompt_body, qw=qw, kvw=kvw, dh=dh, kt=kt, q_scale=q_scale, n_alias=n_alias)
    return pl.pallas_call(
        body, name="qkv_prompt",
        grid=(t // tm,),
        in_specs=[row(dm), _full(g.shape), _full(w.shape), _full(wck.shape), _full(wcv.shape)]
        + [pl.BlockSpec(memory_space=pl.ANY)] * n_alias,
        out_specs=out_specs, out_shape=out_shape,
        input_output_aliases={5 + k: 8 + k for k in range(n_alias)},
        compiler_params=_cp("parallel"),
    )(x, g, w, wck, wcv, *stacks)


def _qkv_sample_body(x_ref, g_ref, w_ref, q_ref, kc_ref, vc_ref, ks_ref, vs_ref, kw_ref, vw_ref, gt_ref,
                     *, qw, kvw, dh):
    xn = _rms(x_ref[...], g_ref[...]).astype(BF16)
    z = _dot(xn, w_ref[...])
    for j, ref in enumerate((kc_ref, vc_ref, ks_ref, vs_ref, kw_ref, vw_ref)):
        for gi in range(kvw // dh):
            ref[:, gi, :] = z[:, qw + j * kvw + gi * dh:qw + j * kvw + (gi + 1) * dh]
    q_ref[...] = z[:, :qw].astype(BF16)
    gt_ref[...] = jax.nn.sigmoid(z[:, qw + 6 * kvw:])


def _qkv_sample_call(x, g, w, *, qw, kvw, dh):
    t, dm = x.shape
    ng = kvw // dh
    return pl.pallas_call(
        functools.partial(_qkv_sample_body, qw=qw, kvw=kvw, dh=dh), name="qkv_sample",
        grid=(1,),
        in_specs=[_full(x.shape), _full(g.shape), _full(w.shape)],
        out_specs=[_full((t, qw))] + [_full((t, ng, dh))] * 6 + [_full((t, LANES))],
        out_shape=[jax.ShapeDtypeStruct((t, qw), BF16)] + [jax.ShapeDtypeStruct((t, ng, dh), F32)] * 6
        + [jax.ShapeDtypeStruct((t, LANES), F32)],
        compiler_params=_cp("arbitrary"),
    )(x, g, w)


def _select_blocks(score, blk, nblk, count, axis):
    sel = jnp.zeros(score.shape, jnp.bool_)
    for _ in range(count):
        m = jnp.max(score, axis=axis, keepdims=True)
        idx = jnp.min(jnp.where(score == m, blk, nblk), axis=axis, keepdims=True)
        hit = blk == idx
        sel = sel | hit
        score = jnp.where(hit, M_INIT, score)
    return sel


def _nsa_cw_body(qt_ref, gt_ref, kc_ref, vct_ref, kw_ref, vwt_ref, ocw_ref, selb_ref,
                 *, seq, ng, rep, dh, s_scale):
    c = pl.program_id(1)
    qb = qt_ref.shape[1]
    nblk = kc_ref.shape[0]
    band = min(NSA_WINDOW + qb, seq)
    st = pl.multiple_of(jnp.maximum(c * qb - NSA_WINDOW, 0), qb)
    cols = rep * qb
    t_q = c * qb + lax.broadcasted_iota(jnp.int32, (1, qb), 1)
    t_c = jnp.concatenate([t_q] * rep, axis=1)
    ok_c = (lax.broadcasted_iota(jnp.int32, (nblk, cols), 0) + 1) * NSA_BLOCK - 1 <= t_c
    kp = st + lax.broadcasted_iota(jnp.int32, (band, cols), 0)
    ok_w = (kp <= t_c) & (kp > t_c - NSA_WINDOW)
    blk = lax.broadcasted_iota(jnp.int32, (nblk, qb), 0)
    cur = t_q // NSA_BLOCK
    for gi in range(ng):
        heads = [gi * rep + r for r in range(rep)]
        lanes = slice(gi * dh, (gi + 1) * dh)
        qg = jnp.concatenate([qt_ref[h * dh:(h + 1) * dh, :] for h in heads], axis=1)
        s_c = _dot(kc_ref[:, lanes].astype(BF16), qg)
        if s_scale != 1.0:
            s_c = s_c * s_scale
        p_c = _msoftmax(s_c, ok_c, axis=0)
        o_c = _dot(vct_ref[lanes, :].astype(BF16), p_c.astype(BF16))
        imp = p_c[:, 0:qb]
        for r in range(1, rep):
            imp = imp + p_c[:, r * qb:(r + 1) * qb]
        score = jnp.where(blk == cur, SEL_FORCE, jnp.where(blk < cur, imp, -1.0))
        sel = _select_blocks(score, blk, nblk, min(NSA_TOPN, nblk), axis=0)
        selb_ref[gi] = jnp.where(sel, 0.0, NEG).astype(BF16)
        s_w = _dot(kw_ref[gi, pl.ds(st, band), :], qg)
        if s_scale != 1.0:
            s_w = s_w * s_scale
        p_w = _msoftmax(s_w, ok_w, axis=0)
        o_w = _dot(vwt_ref[lanes, pl.ds(st, band)], p_w.astype(BF16))
        g_c = jnp.concatenate([gt_ref[3 * h:3 * h + 1, :] for h in heads], axis=1)
        g_w = jnp.concatenate([gt_ref[3 * h + 2:3 * h + 3, :] for h in heads], axis=1)
        ocw = g_c * o_c + g_w * o_w
        for r, h in enumerate(heads):
            ocw_ref[h * dh:(h + 1) * dh, :] = ocw[:, r * qb:(r + 1) * qb]


def _nsa_cw_call(qt, gates_t, kcmp, vcmp_t, kwg, vwt, *, nbatch, seq, ng, dh, s_scale):
    qw, t = qt.shape
    kvw = ng * dh
    rep = qw // kvw
    qb = min(Q_BLOCK, seq)
    nq = seq // qb
    nblk = seq // NSA_BLOCK
    body = functools.partial(_nsa_cw_body, seq=seq, ng=ng, rep=rep, dh=dh, s_scale=s_scale)
    qcol = lambda n: pl.BlockSpec((n, qb), lambda b, c: (0, b * nq + c))
    return pl.pallas_call(
        body, name="nsa_cmp_win",
        grid=(nbatch, nq),
        in_specs=[qcol(qw), qcol(LANES),
                  pl.BlockSpec((nblk, kvw), lambda b, c: (b, 0)), pl.BlockSpec((None, kvw, nblk), lambda b, c: (b, 0, 0)),
                  pl.BlockSpec((ng, seq, dh), lambda b, c: (0, b, 0)), pl.BlockSpec((kvw, seq), lambda b, c: (0, b))],
        out_specs=[qcol(qw), pl.BlockSpec((ng, None, nblk, qb), lambda b, c: (0, b * nq + c, 0, 0))],
        out_shape=[jax.ShapeDtypeStruct((qw, t), F32), jax.ShapeDtypeStruct((ng, nbatch * nq, nblk, qb), BF16)],
        compiler_params=_cp("parallel", "arbitrary"),
    )(qt, gates_t, kcmp, vcmp_t, kwg, vwt)


def _nsa_sel_body(qt_ref, selb_ref, ka_ref, vst_ref, ocw_ref, gt_ref, o_ref, m_sc, acc_sc,
                  *, rep, dh, kt, s_scale):
    g = pl.program_id(1)
    c = pl.program_id(2)
    qb = qt_ref.shape[1]
    cols = rep * qb
    bpt = kt // NSA_BLOCK
    qg = jnp.concatenate([qt_ref[r * dh:(r + 1) * dh, :] for r in range(rep)], axis=1)
    m_sc[...] = jnp.full(m_sc.shape, M_INIT, F32)
    acc_sc[...] = jnp.zeros(acc_sc.shape, F32)

    def tile(j, causal):
        st = pl.multiple_of(j * kt, kt)
        sb = selb_ref[pl.ds(pl.multiple_of(j * bpt, bpt), bpt), :]
        qa = jnp.concatenate([qg, jnp.concatenate([sb] * rep, axis=1)], axis=0)
        s = _dot(ka_ref[pl.ds(st, kt), :], qa)
        if s_scale != 1.0:
            s = s * s_scale
        if causal:
            kp = st + lax.broadcasted_iota(jnp.int32, (kt, cols), 0)
            t_c = c * qb + (lax.broadcasted_iota(jnp.int32, (kt, cols), 1) & (qb - 1))
            s = jnp.where(kp <= t_c, s, NEG)
        m_old = m_sc[...]
        m_new = jnp.maximum(m_old, jnp.max(s, axis=0, keepdims=True))
        p = jnp.exp(s - m_new)
        acc_sc[...] = jnp.exp(m_old - m_new) * acc_sc[...] + _dot(vst_ref[:, pl.ds(st, kt)], p.astype(BF16))
        m_sc[...] = m_new

    n_full = (c * qb) // kt

    def step(j, carry):
        tile(j, False)
        return carry

    lax.fori_loop(0, n_full, step, 0)
    tile(n_full, True)
    acc = acc_sc[...]
    o_s = acc[0:dh] / acc[dh:dh + 1]
    out = []
    for r in range(rep):
        g_s = gt_ref[pl.ds(3 * (g * rep + r) + 1, 1), :]
        out.append(ocw_ref[r * dh:(r + 1) * dh, :] + g_s * o_s[:, r * qb:(r + 1) * qb])
    o_ref[...] = jnp.concatenate(out, axis=0).T.astype(o_ref.dtype)


def _nsa_sel_call(qt, selb, ka, vst, ocw_t, gates_t, *, nbatch, seq, ng, dh, s_scale):
    qw, t = qt.shape
    rep = qw // (ng * dh)
    qb = min(Q_BLOCK, seq)
    assert qb & (qb - 1) == 0
    nq = seq // qb
    nblk = selb.shape[2]
    kt = min(KV_TILE, seq)
    bpt = kt // NSA_BLOCK
    assert seq % kt == 0 and kt % qb == 0 and bpt % BF16_ROWS == 0 and ka.shape[-1] == dh + bpt
    body = functools.partial(_nsa_sel_body, rep=rep, dh=dh, kt=kt, s_scale=s_scale)
    qcol = pl.BlockSpec((rep * dh, qb), lambda b, g, c: (g, b * nq + c))
    return pl.pallas_call(
        body, name="nsa_selected",
        grid=(nbatch, ng, nq),
        in_specs=[qcol,
                  pl.BlockSpec((None, None, nblk, qb), lambda b, g, c: (g, b * nq + c, 0, 0)),
                  pl.BlockSpec((None, seq, dh + bpt), lambda b, g, c: (g, b, 0)),
                  pl.BlockSpec((None, dh + BF16_ROWS, seq), lambda b, g, c: (g, 0, b)),
                  qcol,
                  pl.BlockSpec((LANES, qb), lambda b, g, c: (0, b * nq + c))],
        out_specs=pl.BlockSpec((qb, rep * dh), lambda b, g, c: (b * nq + c, g)),
        out_shape=jax.ShapeDtypeStruct((t, qw), BF16),
        scratch_shapes=[pltpu.VMEM((1, rep * qb), F32), pltpu.VMEM((dh + BF16_ROWS, rep * qb), F32)],
        compiler_params=_cp("parallel", "parallel", "arbitrary"),
    )(qt, selb, ka, vst, ocw_t, gates_t)


def _page_specs(pps, kvw, page, layer):
    return [pl.BlockSpec((None, None, kvw, page), lambda b, j, pt, i=i: (layer, pt[b, j * pps + i], 0, 0))
            for i in range(pps)]


def _cmp_pages_body(pt_ref, *refs, pps):
    del pt_ref
    k_refs, v_refs = refs[:pps], refs[pps:2 * pps]
    wk_ref, wv_ref, kc_ref, vc_ref = refs[2 * pps:]
    j = pl.program_id(1)
    kvw, page = k_refs[0].shape
    nblk = kc_ref.shape[1]
    bpp = page // NSA_BLOCK

    @pl.when(j == 0)
    def _():
        kc_ref[...] = jnp.zeros(kc_ref.shape, F32)
        vc_ref[...] = jnp.zeros(vc_ref.shape, F32)

    row_blk = lax.broadcasted_iota(jnp.int32, (page, nblk), 0) // NSA_BLOCK
    col = lax.broadcasted_iota(jnp.int32, (page, nblk), 1)
    for page_refs, w_ref, o_ref in ((k_refs, wk_ref, kc_ref), (v_refs, wv_ref, vc_ref)):
        acc = jnp.zeros((kvw, nblk), F32)
        for i, r in enumerate(page_refs):
            y = r[...] * w_ref[...]
            hi = y.astype(BF16)
            lo = (y - hi.astype(F32)).astype(BF16)
            onehot = (col == (j * pps + i) * bpp + row_blk).astype(BF16)
            acc = acc + _dot(jnp.concatenate([hi, lo], axis=1), jnp.concatenate([onehot, onehot], axis=0))
        o_ref[...] += acc


def _cmp_pages_call(page_table, pool_k, pool_v, wk_t, wv_t, *, layer, pps):
    nbs, npg = page_table.shape
    _, _, kvw, page = pool_k.shape
    nblk = npg * (page // NSA_BLOCK)
    assert npg % pps == 0
    wspec = pl.BlockSpec((kvw, page), lambda b, j, pt: (0, 0))
    grid_spec = pltpu.PrefetchScalarGridSpec(
        num_scalar_prefetch=1, grid=(nbs, npg // pps),
        in_specs=_page_specs(pps, kvw, page, layer) * 2 + [wspec] * 2,
        out_specs=[pl.BlockSpec((None, kvw, nblk), lambda b, j, pt: (b, 0, 0))] * 2)
    return pl.pallas_call(
        functools.partial(_cmp_pages_body, pps=pps), name="sample_cmp_pages",
        grid_spec=grid_spec,
        out_shape=[jax.ShapeDtypeStruct((nbs, kvw, nblk), F32)] * 2,
        compiler_params=_cp("parallel", "arbitrary"),
    )(page_table, *([pool_k] * pps), *([pool_v] * pps), wk_t, wv_t)


def _samp_cw_body(q_ref, g_ref, kct_ref, vct_ref, wk_ref, wv_ref, kn_ref, vn_ref, *refs,
                  ts, past, rep, scale, n_alias):
    ocw_ref, selb_ref, wko_ref, wvo_ref = refs[n_alias:]
    rows = q_ref.shape[0]
    kvw, nblk = kct_ref.shape
    wb = wk_ref.shape[1]
    npad = kn_ref.shape[1]
    gt = rows // rep
    q = q_ref[...]
    ti = lax.rem(lax.broadcasted_iota(jnp.int32, (rows, 1), 0), ts)
    t = past + ti
    blk_r = lax.broadcasted_iota(jnp.int32, (rows, nblk), 1)
    p_c = _msoftmax(_dot(q, kct_ref[...].astype(BF16)) * scale, (blk_r + 1) * NSA_BLOCK - 1 <= t)
    o_c = _dot_nt(p_c.astype(BF16), vct_ref[...].astype(BF16))
    imp = p_c[0:gt]
    for r in range(1, rep):
        imp = imp + p_c[r * gt:(r + 1) * gt]
    blk = lax.broadcasted_iota(jnp.int32, (gt, nblk), 1)
    sel = _select_blocks(imp, blk, nblk, min(NSA_TOPN - 1, nblk), axis=1)
    selb_ref[...] = jnp.concatenate([jnp.where(sel, 0.0, NEG)] * rep, axis=0)
    wk, wv, kn, vn = wk_ref[...], wv_ref[...], kn_ref[...], vn_ref[...]
    kp1 = past - wb + lax.broadcasted_iota(jnp.int32, (rows, wb), 1)
    j2 = lax.broadcasted_iota(jnp.int32, (rows, npad), 1)
    ok1 = (kp1 <= t) & (kp1 > t - NSA_WINDOW) & (kp1 >= 0)
    ok2 = (j2 <= ti) & (j2 < ts)
    s1 = jnp.where(ok1, _dot(q, wk.astype(BF16)) * scale, NEG)
    s2 = jnp.where(ok2, _dot(q, kn.astype(BF16)) * scale, NEG)
    m = jnp.maximum(jnp.max(s1, axis=1, keepdims=True), jnp.max(s2, axis=1, keepdims=True))
    e1 = jnp.where(ok1, jnp.exp(s1 - m), 0.0)
    e2 = jnp.where(ok2, jnp.exp(s2 - m), 0.0)
    den = jnp.maximum(jnp.sum(e1, axis=1, keepdims=True) + jnp.sum(e2, axis=1, keepdims=True), 1e-30)
    o_w = _dot_nt((e1 / den).astype(BF16), wv.astype(BF16)) + _dot_nt((e2 / den).astype(BF16), vn.astype(BF16))
    g = g_ref[...]
    ocw_ref[...] = g[:, 0:1] * o_c + g[:, 2:3] * o_w
    lane = lax.broadcasted_iota(jnp.int32, (kvw, wb), 1)

    def advance(old, new):
        tail = jnp.concatenate([jnp.zeros((kvw, wb - npad), F32), pltpu.roll(new, npad - ts, axis=1)], axis=1)
        return jnp.where(lane >= wb - ts, tail, pltpu.roll(old, wb - ts, axis=1))

    wko_ref[...] = advance(wk, kn)
    wvo_ref[...] = advance(wv, vn)


def _samp_cw_call(q_bd, gall, kct, vct, win_k, win_v, kn_t, vn_t, stacks, *, layer, depth, ts, past, rep, dh):
    nbs, rows, kvw = q_bd.shape
    nblk = kct.shape[2]
    wb = win_k.shape[3]
    npad = kn_t.shape[2]
    assert wb >= npad
    per_b = lambda *s: pl.BlockSpec((None,) + s, lambda b: (b,) + (0,) * len(s))
    win = pl.BlockSpec((None, None, kvw, wb), lambda b: (layer, b, 0, 0))
    n_alias = len(stacks)
    body = functools.partial(_samp_cw_body, ts=ts, past=past, rep=rep, scale=dh ** -0.5, n_alias=n_alias)
    return pl.pallas_call(
        body, name="sample_cmp_win",
        grid=(nbs,),
        in_specs=[per_b(rows, kvw), per_b(rows, LANES), per_b(kvw, nblk), per_b(kvw, nblk), win, win,
                  per_b(kvw, npad), per_b(kvw, npad)] + [pl.BlockSpec(memory_space=pl.ANY)] * n_alias,
        out_specs=[per_b(rows, kvw), per_b(rows, nblk), win, win],
        out_shape=[jax.ShapeDtypeStruct((nbs, rows, kvw), F32), jax.ShapeDtypeStruct((nbs, rows, nblk), F32),
                   jax.ShapeDtypeStruct((depth, nbs, kvw, wb), F32), jax.ShapeDtypeStruct((depth, nbs, kvw, wb), F32)],
        input_output_aliases={8 + k: 2 + k for k in range(n_alias)},
        compiler_params=_cp("parallel"),
    )(q_bd, gall, kct, vct, win_k, win_v, kn_t, vn_t, *stacks)


def _samp_sel_body(pt_ref, *refs, pps, ts, scale):
    del pt_ref
    k_refs, v_refs = refs[:pps], refs[pps:2 * pps]
    q_ref, selb_ref, kn_ref, vn_ref, ocw_ref, g_ref, o_ref, m_sc, l_sc, acc_sc = refs[2 * pps:]
    j = pl.program_id(1)
    rows = q_ref.shape[0]
    nblk = selb_ref.shape[1]
    page = k_refs[0].shape[1]
    kt = pps * page
    q = q_ref[...]

    @pl.when(j == 0)
    def _():
        m_sc[...] = jnp.full(m_sc.shape, M_INIT, F32)
        l_sc[...] = jnp.zeros(l_sc.shape, F32)
        acc_sc[...] = jnp.zeros(acc_sc.shape, F32)

    def update(s, v_t):
        m_old = m_sc[...]
        m_new = jnp.maximum(m_old, jnp.max(s, axis=1, keepdims=True))
        alpha = jnp.exp(m_old - m_new)
        p = jnp.exp(s - m_new)
        l_sc[...] = alpha * l_sc[...] + jnp.sum(p, axis=1, keepdims=True)
        acc_sc[...] = alpha * acc_sc[...] + _dot_nt(p.astype(BF16), v_t)
        m_sc[...] = m_new

    k_t = jnp.concatenate([r[...] for r in k_refs], axis=1).astype(BF16)
    v_t = jnp.concatenate([r[...] for r in v_refs], axis=1).astype(BF16)
    key_blk = j * (kt // NSA_BLOCK) + lax.broadcasted_iota(jnp.int32, (nblk, kt), 1) // NSA_BLOCK
    onehot = (key_blk == lax.broadcasted_iota(jnp.int32, (nblk, kt), 0)).astype(BF16)
    update(_dot(q, k_t) * scale + _dot(selb_ref[...].astype(BF16), onehot), v_t)

    @pl.when(j == pl.num_programs(1) - 1)
    def _():
        npad = kn_ref.shape[1]
        ti = lax.rem(lax.broadcasted_iota(jnp.int32, (rows, npad), 0), ts)
        j2 = lax.broadcasted_iota(jnp.int32, (rows, npad), 1)
        s = jnp.where((j2 <= ti) & (j2 < ts), _dot(q, kn_ref[...].astype(BF16)) * scale, NEG)
        update(s, vn_ref[...].astype(BF16))
        o_ref[...] = ocw_ref[...] + g_ref[:, 1:2] * (acc_sc[...] / l_sc[...])


def _samp_sel_call(page_table, pool_k, pool_v, q_bd, selb, kn_t, vn_t, ocw, gall, *, layer, pps, ts, dh):
    nbs, npg = page_table.shape
    _, _, kvw, page = pool_k.shape
    rows = q_bd.shape[1]
    per_b = lambda *s: pl.BlockSpec((None,) + s, lambda b, j, pt: (b,) + (0,) * len(s))
    npad = kn_t.shape[2]
    grid_spec = pltpu.PrefetchScalarGridSpec(
        num_scalar_prefetch=1, grid=(nbs, npg // pps),
        in_specs=_page_specs(pps, kvw, page, layer) * 2 + [
            per_b(rows, kvw), per_b(rows, selb.shape[2]), per_b(kvw, npad), per_b(kvw, npad),
            per_b(rows, kvw), per_b(rows, LANES)],
        out_specs=per_b(rows, kvw),
        scratch_shapes=[pltpu.VMEM((rows, 1), F32), pltpu.VMEM((rows, 1), F32), pltpu.VMEM((rows, kvw), F32)])
    body = functools.partial(_samp_sel_body, pps=pps, ts=ts, scale=dh ** -0.5)
    return pl.pallas_call(
        body, name="sample_selected",
        grid_spec=grid_spec,
        out_shape=jax.ShapeDtypeStruct((nbs, rows, kvw), F32),
        compiler_params=_cp("parallel", "arbitrary"),
    )(page_table, *([pool_k] * pps), *([pool_v] * pps), q_bd, selb, kn_t, vn_t, ocw, gall)


def _merge_body(x_ref, g_ref, wmg_ref, oa_ref, ob_ref, oc_ref, wb_ref, wo_ref, o_ref):
    x = x_ref[...]
    dm = x.shape[1]
    xn = _rms(x, g_ref[...]).astype(BF16)
    mg = jax.nn.sigmoid(_dot(xn, wmg_ref[...]))
    y = mg[:, 0:dm] * _dot(oa_ref[...], wb_ref[0])
    y = y + mg[:, dm:2 * dm] * _dot(ob_ref[...], wb_ref[1])
    y = y + mg[:, 2 * dm:3 * dm] * _dot(oc_ref[...], wb_ref[2])
    o_ref[...] = x + _dot(y.astype(BF16), wo_ref[...])


def _merge_call(x, g, wmg, oa, ob, oc, wb, wo, *, tm):
    t, dm = x.shape
    row = pl.BlockSpec((tm, dm), lambda i: (i, 0))
    return pl.pallas_call(
        _merge_body, name="merge",
        grid=(t // tm,),
        in_specs=[row, _full(g.shape), _full(wmg.shape), row, row, row, _full(wb.shape), _full(wo.shape)],
        out_specs=row,
        out_shape=jax.ShapeDtypeStruct((t, dm), F32),
        compiler_params=_cp("parallel"),
    )(x, g, wmg, oa, ob, oc, wb, wo)


def _ffn_body(x_ref, g_ref, w1_ref, w3_ref, w2_ref, fg_ref, o_ref, xn_sc, acc_sc, *, final):
    f = pl.program_id(1)

    @pl.when(f == 0)
    def _():
        xn_sc[...] = _rms(x_ref[...], g_ref[...]).astype(BF16)
        acc_sc[...] = jnp.zeros(acc_sc.shape, F32)

    xn = xn_sc[...]
    h = jax.nn.silu(_dot(xn, w1_ref[...])) * _dot(xn, w3_ref[...])
    acc_sc[...] += _dot(h.astype(BF16), w2_ref[...])

    @pl.when(f == pl.num_programs(1) - 1)
    def _():
        y = x_ref[...] + acc_sc[...]
        o_ref[...] = _rms(y, fg_ref[...]) if final else y


def _ffn_tile(f_dim):
    half = f_dim // 2
    return half if f_dim % 2 == 0 and half % LANES == 0 else f_dim


def _ffn_call(x, g, w1, w3, w2, fg, *, tm, final):
    t, dm = x.shape
    fd = w1.shape[1]
    ft = _ffn_tile(fd)
    row = pl.BlockSpec((tm, dm), lambda i, f: (i, 0))
    vec = pl.BlockSpec((1, dm), lambda i, f: (0, 0))
    return pl.pallas_call(
        functools.partial(_ffn_body, final=final), name="ffn",
        grid=(t // tm, fd // ft),
        in_specs=[row, vec, pl.BlockSpec((dm, ft), lambda i, f: (0, f)), pl.BlockSpec((dm, ft), lambda i, f: (0, f)),
                  pl.BlockSpec((ft, dm), lambda i, f: (f, 0)), vec],
        out_specs=row,
        out_shape=jax.ShapeDtypeStruct((t, dm), F32),
        scratch_shapes=[pltpu.VMEM((tm, dm), BF16), pltpu.VMEM((tm, dm), F32)],
        compiler_params=_cp("parallel", "arbitrary"),
    )(x, g, w1, w3, w2, fg)


def _moe_body(x_ref, g_ref, wr_ref, br_ref, w1_ref, w3_ref, w2_ref, fg_ref, o_ref, xn_sc, comb_sc, acc_sc, *, final):
    e = pl.program_id(1)
    f = pl.program_id(2)

    @pl.when((e == 0) & (f == 0))
    def _():
        xn = _rms(x_ref[...], g_ref[...])
        xn_sc[...] = xn.astype(BF16)
        acc_sc[...] = jnp.zeros(acc_sc.shape, F32)
        logits = jnp.dot(xn, wr_ref[...], preferred_element_type=F32, precision=lax.Precision.HIGHEST) + br_ref[...]
        lane = lax.broadcasted_iota(jnp.int32, logits.shape, 1)
        nl = logits.shape[1]
        m1 = jnp.max(logits, axis=1, keepdims=True)
        i1 = jnp.min(jnp.where(logits == m1, lane, nl), axis=1, keepdims=True)
        rest = jnp.where(lane == i1, M_INIT, logits)
        m2 = jnp.max(rest, axis=1, keepdims=True)
        i2 = jnp.min(jnp.where(rest == m2, lane, nl), axis=1, keepdims=True)
        e2 = jnp.exp(m2 - m1)
        den = 1.0 + e2
        comb_sc[...] = jnp.where(lane == i1, 1.0 / den, 0.0) + jnp.where(lane == i2, e2 / den, 0.0)

    xn = xn_sc[...]
    h = jax.nn.silu(_dot(xn, w1_ref[...])) * _dot(xn, w3_ref[...])
    comb = comb_sc[...]
    lane = lax.broadcasted_iota(jnp.int32, comb.shape, 1)
    ce = jnp.sum(jnp.where(lane == e, comb, 0.0), axis=1, keepdims=True)
    acc_sc[...] += ce * _dot(h.astype(BF16), w2_ref[...])

    @pl.when((e == pl.num_programs(1) - 1) & (f == pl.num_programs(2) - 1))
    def _():
        y = x_ref[...] + acc_sc[...]
        o_ref[...] = _rms(y, fg_ref[...]) if final else y


def _moe_call(x, g, wr, br, w1, w3, w2, fg, *, tm, final):
    t, dm = x.shape
    ne, _, fd = w1.shape
    ft = _ffn_tile(fd)
    row = pl.BlockSpec((tm, dm), lambda i, e, f: (i, 0))
    vec = lambda n: pl.BlockSpec((1, n), lambda i, e, f: (0, 0))
    return pl.pallas_call(
        functools.partial(_moe_body, final=final), name="moe",
        grid=(t // tm, ne, fd // ft),
        in_specs=[row, vec(dm), pl.BlockSpec(wr.shape, lambda i, e, f: (0, 0)), vec(wr.shape[1]),
                  pl.BlockSpec((None, dm, ft), lambda i, e, f: (e, 0, f)),
                  pl.BlockSpec((None, dm, ft), lambda i, e, f: (e, 0, f)),
                  pl.BlockSpec((None, ft, dm), lambda i, e, f: (e, f, 0)), vec(dm)],
        out_specs=row,
        out_shape=jax.ShapeDtypeStruct((t, dm), F32),
        scratch_shapes=[pltpu.VMEM((tm, dm), BF16), pltpu.VMEM((tm, wr.shape[1]), F32), pltpu.VMEM((tm, dm), F32)],
        compiler_params=_cp("parallel", "arbitrary", "arbitrary"),
    )(x, g, wr, br, w1, w3, w2, fg)


def kernel(x_prompt, x_sample, cache_cmp_k, cache_cmp_v, cache_slc_k, cache_slc_v, cache_win_k, cache_win_v, state_conv, state_lru, page_table, norm_mix_g, w_in, lru_conv_w, lru_conv_b, lru_wa, lru_ba, lru_wx, lru_bx, lru_lambda, nsa_cmp_wk, nsa_cmp_wv, gm_norm_g, gm_ws, gm_bs, w_branch, w_out, norm_ffn_g, ffn_w1, ffn_w3, ffn_w2, moe_wr, moe_br, moe_w1, moe_w3, moe_w2, final_norm_g):
    depth, dm, _ = w_in.shape
    nbp, seq, _ = x_prompt.shape
    nbs, ts, _ = x_sample.shape
    _, n_pool, page, ng, dh = cache_cmp_k.shape
    wd = state_lru.shape[-1]
    conv_w = lru_conv_w.shape[1]
    gmw = gm_norm_g.shape[-1]
    n_gm, chunk = gm_ws.shape[1], gm_ws.shape[2]
    kvw = ng * dh
    qw = NSA_HEADS * dh
    rep = NSA_HEADS // ng
    npg = page_table.shape[1]
    past = npg * page
    wbuf = cache_win_k.shape[2]
    tp, tsm = nbp * seq, nbs * ts
    ne = moe_wr.shape[-1]
    nblk = seq // NSA_BLOCK
    scale = dh ** -0.5
    s_scale = 1.0 if _is_power_of_two(scale) else scale

    assert seq % chunk == 0 and seq % LRU_CHUNK == 0 and seq >= NSA_WINDOW
    assert past % NSA_BLOCK == 0 and ts < NSA_BLOCK and ts < chunk and wbuf == NSA_WINDOW and nbs % 8 == 0
    assert past // NSA_BLOCK >= NSA_TOPN - 1 and page % LANES == 0

    offs = [0, wd, 2 * wd, 2 * wd + qw, 2 * wd + qw + 6 * kvw, 2 * wd + qw + 6 * kvw + 3 * NSA_HEADS]
    offs += [offs[-1] + 2 * gmw, offs[-1] + 2 * gmw + 3 * dm]
    w_in_b = w_in.astype(BF16)
    w_lru = w_in_b[:, :, 0:offs[2]]
    w_nsa = jnp.pad(w_in_b[:, :, offs[2]:offs[5]], ((0, 0), (0, 0), (0, LANES - 3 * NSA_HEADS)))
    w_uv = w_in_b[:, :, offs[5]:offs[6]]
    w_mg = w_in_b[:, :, offs[6]:offs[7]]
    wa_b, wx_b = lru_wa.astype(BF16), lru_wx.astype(BF16)
    wb_b, wo_b = w_branch.astype(BF16), w_out.astype(BF16)
    f1_b, f3_b, f2_b = ffn_w1.astype(BF16), ffn_w3.astype(BF16), ffn_w2.astype(BF16)
    m1_b, m3_b, m2_b = moe_w1.astype(BF16), moe_w3.astype(BF16), moe_w2.astype(BF16)
    wr_pad = jnp.pad(moe_wr, ((0, 0), (0, 0), (0, LANES - ne)))
    br_pad = jnp.pad(moe_br, ((0, 0), (0, LANES - ne)), constant_values=NEG)[:, None, :]
    wck = nsa_cmp_wk.reshape(depth, NSA_BLOCK, kvw)
    wcv = nsa_cmp_wv.reshape(depth, NSA_BLOCK, kvw)
    cmp_w_t = lambda w: jnp.tile(jnp.transpose(w, (0, 2, 3, 1)).reshape(depth, kvw, NSA_BLOCK), (1, 1, page // NSA_BLOCK))
    wck_t, wcv_t = cmp_w_t(nsa_cmp_wk), cmp_w_t(nsa_cmp_wv)
    gc = gmw // n_gm
    bs_full = jnp.repeat(jnp.swapaxes(gm_bs, 1, 2), gc, axis=2)
    ws_short = jnp.repeat(jnp.transpose(gm_ws[:, :, :ts, :ts], (0, 2, 3, 1)).reshape(depth, ts * ts, n_gm), gc, axis=2)
    vec = lambda a, l: a[l][None, :]
    pps = min(PAGES_PER_STEP, npg)
    feature_major = lambda a: jnp.transpose(a, (0, 1, 3, 4, 2)).reshape(a.shape[0], a.shape[1], kvw, a.shape[2])
    pool_ck, pool_cv = feature_major(cache_cmp_k), feature_major(cache_cmp_v)
    pool_sk, pool_sv = feature_major(cache_slc_k), feature_major(cache_slc_v)
    win_k, win_v = feature_major(cache_win_k), feature_major(cache_win_v)
    eye_g = jnp.eye(ng, dtype=BF16)

    tm_p = min(ROW_TILE, tp)
    xp = x_prompt.reshape(tp, dm)
    xs = x_sample.reshape(tsm, dm)
    zeros_cs = jnp.zeros((nbp, conv_w - 1, wd), F32)
    zeros_h = jnp.zeros((nbp, 1, wd), F32)
    to_tm = lambda a, n: jnp.swapaxes(a.reshape(nbs, n, -1), 0, 1).reshape(1, n * nbs, -1)
    from_tm = lambda a, n: jnp.swapaxes(a.reshape(n, nbs, -1), 0, 1)
    new_rows_t = lambda a: jnp.pad(jnp.swapaxes(a.reshape(nbs, ts, kvw), 1, 2), ((0, 0), (0, 0), (0, LANES - ts)))

    p_stacks, w_stacks = (), ()
    p_small = [[] for _ in range(2)]
    s_out = [[] for _ in range(7)]
    for l in range(depth):
        last = l == depth - 1
        gmix = vec(norm_mix_g, l)
        lru_w = (lru_conv_w[l], vec(lru_conv_b, l), wa_b[l], vec(lru_ba, l), wx_b[l], vec(lru_bx, l), vec(lru_lambda, l))

        o_a, conv_new, h_last = _lru_call(xp.reshape(nbp, seq, dm), gmix, w_lru[l], zeros_cs, zeros_h, *lru_w,
                                          nb=1, tch=min(LRU_CHUNK, seq), starts_at_zero=True)
        o_c = _gmlp_call(xp, gmix, w_uv[l], vec(gm_norm_g, l), gm_ws[l], bs_full[l], tm=tm_p)
        qt, gates_t, kcmp, vcmp, ka, vst, kwg, vwt, *p_stacks = _qkv_prompt_call(
            xp, gmix, w_nsa[l], wck[l], wcv[l], p_stacks, layer=l, depth=depth, nbatch=nbp, seq=seq, tm=tm_p,
            qw=qw, kvw=kvw, dh=dh)
        vcmp_t = jnp.swapaxes(vcmp.reshape(nbp, nblk, kvw), 1, 2)
        ocw_t, selb = _nsa_cw_call(qt, gates_t, kcmp, vcmp_t, kwg, vwt, nbatch=nbp, seq=seq, ng=ng, dh=dh,
                                   s_scale=s_scale)
        o_b = _nsa_sel_call(qt, selb, ka, vst, ocw_t, gates_t, nbatch=nbp, seq=seq, ng=ng, dh=dh, s_scale=s_scale)
        xp = _merge_call(xp, gmix, w_mg[l], o_a.reshape(tp, wd), o_b, o_c, wb_b[l], wo_b[l], tm=tm_p)
        if l % 2 == 0:
            xp = _ffn_call(xp, vec(norm_ffn_g, l), f1_b[l // 2], f3_b[l // 2], f2_b[l // 2], final_norm_g[None, :],
                           tm=tm_p, final=last)
        else:
            xp = _moe_call(xp, vec(norm_ffn_g, l), wr_pad[l // 2], br_pad[l // 2], m1_b[l // 2], m3_b[l // 2],
                           m2_b[l // 2], final_norm_g[None, :], tm=tm_p, final=last)
        p_small[0].append(conv_new)
        p_small[1].append(h_last.reshape(nbp, wd))

        xs_tm = to_tm(xs, ts)
        o_a, conv_new, h_last = _lru_call(xs_tm, gmix, w_lru[l], to_tm(state_conv[l], conv_w - 1),
                                          state_lru[l][None], *lru_w, nb=nbs, tch=ts, starts_at_zero=False)
        o_a = from_tm(o_a, ts).reshape(tsm, wd)
        o_c, gm_v = _gmlp_short_call(xs_tm[0], gmix, w_uv[l], vec(gm_norm_g, l), ws_short[l], bs_full[l][:ts],
                                     n=ts, nb=nbs)
        o_c = from_tm(o_c, ts).reshape(tsm, gmw)
        q, k_c, v_c, k_s, v_s, k_w, v_w, gates = _qkv_sample_call(xs, gmix, w_nsa[l], qw=qw, kvw=kvw, dh=dh)
        q5 = jnp.transpose(q.reshape(nbs, ts, ng, rep, dh), (0, 3, 2, 1, 4))
        q_bd = (q5[:, :, :, :, None, :] * eye_g[None, None, :, None, :, None]).reshape(nbs, rep * ng * ts, kvw)
        g5 = jnp.transpose(gates[:, :3 * NSA_HEADS].reshape(nbs, ts, ng, rep, 3), (0, 3, 2, 1, 4))
        gall = jnp.pad(g5.reshape(nbs, rep * ng * ts, 3), ((0, 0), (0, 0), (0, LANES - 3)))
        kct, vct = _cmp_pages_call(page_table, pool_ck, pool_cv, wck_t[l], wcv_t[l], layer=l, pps=pps)
        ocw, selb, *w_stacks = _samp_cw_call(
            q_bd, gall, kct, vct, win_k, win_v, new_rows_t(k_w), new_rows_t(v_w), w_stacks,
            layer=l, depth=depth, ts=ts, past=past, rep=rep, dh=dh)
        o_all = _samp_sel_call(page_table, pool_sk, pool_sv, q_bd, selb, new_rows_t(k_s), new_rows_t(v_s),
                               ocw, gall, layer=l, pps=pps, ts=ts, dh=dh)
        o6 = o_all.reshape(nbs, rep, ng, ts, ng, dh)
        o_b = jnp.stack([o6[:, :, gi, :, gi, :] for gi in range(ng)], axis=2)
        o_b = jnp.transpose(o_b, (0, 3, 2, 1, 4)).reshape(tsm, qw).astype(BF16)
        xs = _merge_call(xs, gmix, w_mg[l], o_a, o_b, o_c, wb_b[l], wo_b[l], tm=tsm)
        if l % 2 == 0:
            xs = _ffn_call(xs, vec(norm_ffn_g, l), f1_b[l // 2], f3_b[l // 2], f2_b[l // 2], final_norm_g[None, :],
                           tm=tsm, final=last)
        else:
            xs = _moe_call(xs, vec(norm_ffn_g, l), wr_pad[l // 2], br_pad[l // 2], m1_b[l // 2], m3_b[l // 2],
                           m2_b[l // 2], final_norm_g[None, :], tm=tsm, final=last)
        shs = (nbs, ts, ng, dh)
        for lst, a in zip(s_out, (k_c.reshape(shs), v_c.reshape(shs), k_s.reshape(shs), v_s.reshape(shs),
                                  from_tm(conv_new[0], conv_w - 1), h_last[0], from_tm(gm_v, ts))):
            lst.append(a)

    row_major = lambda a: jnp.transpose(a.reshape(a.shape[0], a.shape[1], ng, dh, a.shape[3]), (0, 1, 4, 2, 3))
    keep = min(NSA_WINDOW, seq)
    kc_st, vc_st, ks_st, vs_st, kw_st, vw_st = p_stacks
    s_small = [jnp.stack(a) for a in s_out]
    return (xp.reshape(nbp, seq, dm), xs.reshape(nbs, ts, dm),
            row_major(kc_st), row_major(vc_st), row_major(ks_st), row_major(vs_st),
            row_major(kw_st[..., seq - keep:]), row_major(vw_st[..., seq - keep:]),
            jnp.stack(p_small[0]), jnp.stack(p_small[1]),
            s_small[0], s_small[1], s_small[2], s_small[3],
            row_major(w_stacks[0]), row_major(w_stacks[1]),
            s_small[4], s_small[5], s_small[6])
```

```python
import functools
import math

import jax
import jax.numpy as jnp
from jax import lax
from jax.experimental import pallas as pl
from jax.experimental.pallas import tpu as pltpu

F32 = jnp.float32
BF16 = jnp.bfloat16

NSA_HEADS = 16
NSA_BLOCK = 64
NSA_TOPN = 8
NSA_WINDOW = 512
SEL_FORCE = 1.0e4
LRU_C = 8.0
TOP_K = 2
RMS_EPS = 1e-6
NEG = -1e30
M_INIT = -3.0e38

Q_BLOCK = 128
KV_TILE = 1024
LRU_CHUNK = 256
ROW_TILE = 512
MOE_TILE = 1024
MOE_BATCH = 320
PAGES_PER_STEP = 8
BF16_ROWS = 16
LANES = 128
VMEM_LIMIT = 56 * 2 ** 20


def _cp(*sem):
    return pltpu.CompilerParams(dimension_semantics=sem, vmem_limit_bytes=VMEM_LIMIT)


def _dot(a, b):
    return jnp.dot(a, b, preferred_element_type=F32)


def _dot_nt(a, b):
    return lax.dot_general(a, b, (((1,), (1,)), ((), ())), preferred_element_type=F32)


def _rms(x, g):
    return x * lax.rsqrt(jnp.mean(x * x, axis=-1, keepdims=True) + RMS_EPS) * g


def _msoftmax(s, mask, axis=-1):
    s = jnp.where(mask, s, NEG)
    e = jnp.where(mask, jnp.exp(s - jnp.max(s, axis=axis, keepdims=True)), 0.0)
    return e / jnp.maximum(jnp.sum(e, axis=axis, keepdims=True), 1e-30)


def _is_power_of_two(x):
    return math.frexp(x)[0] == 0.5


def _full(shape):
    n = len(shape)
    return pl.BlockSpec(shape, lambda *_: (0,) * n)


def _lru_body(x_ref, g_ref, w_ref, cs_ref, h0_ref, cw_ref, cb_ref, wa_ref, ba_ref, wx_ref, bx_ref, lam_ref,
              oa_ref, cn_ref, hl_ref, xbuf, hc, *, nb, tch, conv_w, starts_at_zero):
    c = pl.program_id(1)
    rows = nb * tch
    wd = hc.shape[-1]
    nst = (conv_w - 1) * nb
    cr = xbuf.shape[0] - rows
    nh = wa_ref.shape[0]
    bw = wd // nh

    @pl.when(c == 0)
    def _():
        if cr > nst:
            xbuf[0:cr - nst, :] = jnp.zeros((cr - nst, wd), F32)
        xbuf[cr - nst:cr, :] = cs_ref[...]
        hc[...] = h0_ref[...]

    xn = _rms(x_ref[...], g_ref[...]).astype(BF16)
    z = _dot(xn, w_ref[...])
    gate = z[:, wd:]
    xbuf[cr:cr + rows, :] = z[:, :wd]
    y = cb_ref[...] + xbuf[cr - nst:cr - nst + rows, :] * cw_ref[0:1, :]
    for k in range(1, conv_w):
        st = cr - nst + k * nb
        y = y + xbuf[st:st + rows, :] * cw_ref[k:k + 1, :]
    cn_ref[...] = xbuf[cr + rows - nst:cr + rows, :]
    xbuf[0:cr, :] = xbuf[rows:rows + cr, :]

    yb = y.astype(BF16)

    def block_diag(wr):
        return jnp.concatenate([_dot(yb[:, j * bw:(j + 1) * bw], wr[j]) for j in range(nh)], axis=1)

    r = jax.nn.sigmoid(block_diag(wa_ref) + ba_ref[...])
    i = jax.nn.sigmoid(block_diag(wx_ref) + bx_ref[...])
    lam = lam_ref[...]
    softplus_neg_lam = jnp.maximum(-lam, 0.0) + jnp.log1p(jnp.exp(-jnp.abs(lam)))
    a = jnp.exp((-LRU_C) * r * softplus_neg_lam)
    mult = jnp.sqrt(1.0 - a * a)
    row = lax.broadcasted_iota(jnp.int32, (rows, wd), 0)
    if starts_at_zero:
        mult = jnp.where((row < nb) & (c == 0), 1.0, mult)
    u = mult * (i * y)
    if nb == 1:
        u = u + jnp.where(row < 1, a * hc[...], 0.0)
    else:
        u = jnp.concatenate([u[:nb] + a[:nb] * hc[...], u[nb:]], axis=0)
    d = nb
    while d < rows:
        a_sh = jnp.where(row < d, 1.0, pltpu.roll(a, d, axis=0))
        u_sh = jnp.where(row < d, 0.0, pltpu.roll(u, d, axis=0))
        u = a * u_sh + u
        a = a * a_sh
        d *= 2
    hc[...] = u[rows - nb:rows, :]
    hl_ref[...] = u[rows - nb:rows, :]
    oa_ref[...] = (jax.nn.gelu(gate) * u).astype(oa_ref.dtype)


def _lru_call(x, g, w, cs, h0, cw, cb, wa, ba, wx, bx, lam, *, nb, tch, starts_at_zero):
    nbatch, rows_total, dm = x.shape
    wd = h0.shape[-1]
    conv_w = cw.shape[0]
    nst = (conv_w - 1) * nb
    rows = nb * tch
    cr = -(-nst // 8) * 8
    assert rows_total % rows == 0 and rows >= cr
    body = functools.partial(_lru_body, nb=nb, tch=tch, conv_w=conv_w, starts_at_zero=starts_at_zero)
    return pl.pallas_call(
        body, name="lru",
        grid=(nbatch, rows_total // rows),
        in_specs=[pl.BlockSpec((None, rows, dm), lambda b, c: (b, c, 0)),
                  _full(g.shape), _full(w.shape),
                  pl.BlockSpec((None, nst, wd), lambda b, c: (b, 0, 0)),
                  pl.BlockSpec((None, nb, wd), lambda b, c: (b, 0, 0)),
                  _full(cw.shape), _full(cb.shape), _full(wa.shape), _full(ba.shape), _full(wx.shape),
                  _full(bx.shape), _full(lam.shape)],
        out_specs=[pl.BlockSpec((None, rows, wd), lambda b, c: (b, c, 0)),
                   pl.BlockSpec((None, nst, wd), lambda b, c: (b, 0, 0)),
                   pl.BlockSpec((None, nb, wd), lambda b, c: (b, 0, 0))],
        out_shape=[jax.ShapeDtypeStruct((nbatch, rows_total, wd), BF16),
                   jax.ShapeDtypeStruct((nbatch, nst, wd), F32),
                   jax.ShapeDtypeStruct((nbatch, nb, wd), F32)],
        scratch_shapes=[pltpu.VMEM((cr + rows, wd), F32), pltpu.VMEM((nb, wd), F32)],
        compiler_params=_cp("arbitrary", "arbitrary"),
    )(x, g, w, cs, h0, cw, cb, wa, ba, wx, bx, lam)


def _gmlp_body(x_ref, g_ref, w_ref, gg_ref, ws_ref, bs_ref, o_ref, *, chunk):
    tm = x_ref.shape[0]
    xn = _rms(x_ref[...], g_ref[...]).astype(BF16)
    ge = jax.nn.gelu(_dot(xn, w_ref[...]))
    wd = ge.shape[1] // 2
    u = ge[:, :wd]
    vb = _rms(ge[:, wd:], gg_ref[...]).astype(BF16)
    ng = ws_ref.shape[0]
    gc = wd // ng
    tril = lax.broadcasted_iota(jnp.int32, (chunk, chunk), 0) >= lax.broadcasted_iota(jnp.int32, (chunk, chunk), 1)
    wsm = [jnp.where(tril, ws_ref[gi], 0.0).astype(BF16) for gi in range(ng)]
    for ch in range(tm // chunk):
        lo, hi = ch * chunk, (ch + 1) * chunk
        mixed = jnp.concatenate([_dot(wsm[gi], vb[lo:hi, gi * gc:(gi + 1) * gc]) for gi in range(ng)], axis=1)
        o_ref[lo:hi, :] = (u[lo:hi] * (mixed + bs_ref[...])).astype(o_ref.dtype)


def _gmlp_call(x, g, w, gg, ws, bs_full, *, tm):
    t, dm = x.shape
    wd = gg.shape[-1]
    chunk = ws.shape[-1]
    assert t % tm == 0 and tm % chunk == 0
    return pl.pallas_call(
        functools.partial(_gmlp_body, chunk=chunk), name="gmlp",
        grid=(t // tm,),
        in_specs=[pl.BlockSpec((tm, dm), lambda i: (i, 0)), _full(g.shape), _full(w.shape), _full(gg.shape),
                  _full(ws.shape), _full(bs_full.shape)],
        out_specs=pl.BlockSpec((tm, wd), lambda i: (i, 0)),
        out_shape=jax.ShapeDtypeStruct((t, wd), BF16),
        compiler_params=_cp("parallel"),
    )(x, g, w, gg, ws, bs_full)


def _gmlp_short_body(x_ref, g_ref, w_ref, gg_ref, wexp_ref, bexp_ref, o_ref, v_ref, *, n, nb):
    xn = _rms(x_ref[...], g_ref[...]).astype(BF16)
    ge = jax.nn.gelu(_dot(xn, w_ref[...]))
    wd = ge.shape[1] // 2
    u = ge[:, :wd]
    v = _rms(ge[:, wd:], gg_ref[...])
    v_ref[...] = v
    for t in range(n):
        m = bexp_ref[t:t + 1, :] + wexp_ref[t * n:t * n + 1, :] * v[0:nb]
        for s in range(1, t + 1):
            m = m + wexp_ref[t * n + s:t * n + s + 1, :] * v[s * nb:(s + 1) * nb]
        o_ref[t * nb:(t + 1) * nb, :] = (u[t * nb:(t + 1) * nb] * m).astype(o_ref.dtype)


def _gmlp_short_call(x, g, w, gg, wexp, bexp, *, n, nb):
    t, dm = x.shape
    wd = gg.shape[-1]
    return pl.pallas_call(
        functools.partial(_gmlp_short_body, n=n, nb=nb), name="gmlp_short",
        grid=(1,),
        in_specs=[_full(x.shape), _full(g.shape), _full(w.shape), _full(gg.shape), _full(wexp.shape),
                  _full(bexp.shape)],
        out_specs=[_full((t, wd)), _full((t, wd))],
        out_shape=[jax.ShapeDtypeStruct((t, wd), BF16), jax.ShapeDtypeStruct((t, wd), F32)],
        compiler_params=_cp("arbitrary"),
    )(x, g, w, gg, wexp, bexp)


def _qkv_prompt_body(x_ref, g_ref, w_ref, wck_ref, wcv_ref, *refs, qw, kvw, dh, kt, q_scale, n_alias):
    (q_ref, gt_ref, kcmp_ref, vcmp_ref, ka_ref, vst_ref, kwg_ref, vwt_ref) = refs[n_alias:n_alias + 8]
    stack_refs = refs[n_alias + 8:]
    tm = x_ref.shape[0]
    ng = kvw // dh
    xn = _rms(x_ref[...], g_ref[...]).astype(BF16)
    z = _dot(xn, w_ref[...])
    parts = [z[:, qw + j * kvw:qw + (j + 1) * kvw] for j in range(6)]
    parts_t = [p.T for p in parts]
    for ref, p in zip(stack_refs, parts_t):
        ref[...] = p
    q_ref[...] = (z[:, :qw] * q_scale).T.astype(BF16)
    gt_ref[...] = jax.nn.sigmoid(z[:, qw + 6 * kvw:]).T
    nbt = tm // NSA_BLOCK
    kcmp_ref[...] = jnp.sum(parts[0].reshape(nbt, NSA_BLOCK, kvw) * wck_ref[...][None], axis=1)
    vcmp_ref[...] = jnp.sum(parts[1].reshape(nbt, NSA_BLOCK, kvw) * wcv_ref[...][None], axis=1)
    bpt = kt // NSA_BLOCK
    pos0 = lax.rem(pl.program_id(0) * tm, kt)
    blk = (pos0 + lax.broadcasted_iota(jnp.int32, (tm, bpt), 0)) // NSA_BLOCK
    onehot = (blk == lax.broadcasted_iota(jnp.int32, (tm, bpt), 1)).astype(BF16)
    ones_row = (lax.broadcasted_iota(jnp.int32, (BF16_ROWS, tm), 0) == 0).astype(BF16)
    ksb = parts[2].astype(BF16)
    kwb = parts[4].astype(BF16)
    vst = parts_t[3].astype(BF16)
    vwt_ref[...] = parts_t[5].astype(BF16)
    for gi in range(ng):
        ka_ref[gi, :, 0:dh] = ksb[:, gi * dh:(gi + 1) * dh]
        ka_ref[gi, :, dh:dh + bpt] = onehot
        kwg_ref[gi] = kwb[:, gi * dh:(gi + 1) * dh]
        vst_ref[gi, 0:dh, :] = vst[gi * dh:(gi + 1) * dh, :]
        vst_ref[gi, dh:dh + BF16_ROWS, :] = ones_row


def _qkv_prompt_call(x, g, w, wck, wcv, stacks, *, layer, depth, nbatch, seq, tm, qw, kvw, dh):
    t, dm = x.shape
    ng = kvw // dh
    kt = min(KV_TILE, seq)
    bpt = kt // NSA_BLOCK
    nbt = tm // NSA_BLOCK
    nst = seq // tm
    assert t % tm == 0 and tm % (8 * NSA_BLOCK) == 0 and seq % tm == 0 and kt % tm == 0 and tm % LANES == 0
    row = lambda n: pl.BlockSpec((tm, n), lambda i: (i, 0))
    col = lambda n: pl.BlockSpec((n, tm), lambda i: (0, i))
    stack_spec = pl.BlockSpec((None, None, kvw, tm), lambda i: (layer, i // nst, 0, i % nst))
    stack_shape = jax.ShapeDtypeStruct((depth, nbatch, kvw, seq), F32)
    out_specs = [col(qw), col(LANES), pl.BlockSpec((nbt, kvw), lambda i: (i, 0)), pl.BlockSpec((nbt, kvw), lambda i: (i, 0)),
                 pl.BlockSpec((ng, tm, dh + bpt), lambda i: (0, i, 0)),
                 pl.BlockSpec((ng, dh + BF16_ROWS, tm), lambda i: (0, 0, i)),
                 pl.BlockSpec((ng, tm, dh), lambda i: (0, i, 0)), col(kvw)] + [stack_spec] * 6
    out_shape = [jax.ShapeDtypeStruct((qw, t), BF16), jax.ShapeDtypeStruct((LANES, t), F32),
                 jax.ShapeDtypeStruct((t // NSA_BLOCK, kvw), F32), jax.ShapeDtypeStruct((t // NSA_BLOCK, kvw), F32),
                 jax.ShapeDtypeStruct((ng, t, dh + bpt), BF16), jax.ShapeDtypeStruct((ng, dh + BF16_ROWS, t), BF16),
                 jax.ShapeDtypeStruct((ng, t, dh), BF16), jax.ShapeDtypeStruct((kvw, t), BF16)] + [stack_shape] * 6
    scale = dh ** -0.5
    q_scale = scale if _is_power_of_two(scale) else 1.0
    n_alias = len(stacks)
    body = functools.partial(_qkv_prompt_body, qw=qw, kvw=kvw, dh=dh, kt=kt, q_scale=q_scale, n_alias=n_alias)
    return pl.pallas_call(
        body, name="qkv_prompt",
        grid=(t // tm,),
        in_specs=[row(dm), _full(g.shape), _full(w.shape), _full(wck.shape), _full(wcv.shape)]
        + [pl.BlockSpec(memory_space=pl.ANY)] * n_alias,
        out_specs=out_specs, out_shape=out_shape,
        input_output_aliases={5 + k: 8 + k for k in range(n_alias)},
        compiler_params=_cp("parallel"),
    )(x, g, w, wck, wcv, *stacks)


def _qkv_sample_body(x_ref, g_ref, w_ref, q_ref, kc_ref, vc_ref, ks_ref, vs_ref, kw_ref, vw_ref, gt_ref,
                     *, qw, kvw, dh):
    xn = _rms(x_ref[...], g_ref[...]).astype(BF16)
    z = _dot(xn, w_ref[...])
    for j, ref in enumerate((kc_ref, vc_ref, ks_ref, vs_ref, kw_ref, vw_ref)):
        for gi in range(kvw // dh):
            ref[:, gi, :] = z[:, qw + j * kvw + gi * dh:qw + j * kvw + (gi + 1) * dh]
    q_ref[...] = z[:, :qw].astype(BF16)
    gt_ref[...] = jax.nn.sigmoid(z[:, qw + 6 * kvw:])


def _qkv_sample_call(x, g, w, *, qw, kvw, dh):
    t, dm = x.shape
    ng = kvw // dh
    return pl.pallas_call(
        functools.partial(_qkv_sample_body, qw=qw, kvw=kvw, dh=dh), name="qkv_sample",
        grid=(1,),
        in_specs=[_full(x.shape), _full(g.shape), _full(w.shape)],
        out_specs=[_full((t, qw))] + [_full((t, ng, dh))] * 6 + [_full((t, LANES))],
        out_shape=[jax.ShapeDtypeStruct((t, qw), BF16)] + [jax.ShapeDtypeStruct((t, ng, dh), F32)] * 6
        + [jax.ShapeDtypeStruct((t, LANES), F32)],
        compiler_params=_cp("arbitrary"),
    )(x, g, w)


def _select_blocks(score, blk, nblk, count, axis):
    sel = jnp.zeros(score.shape, jnp.bool_)
    for _ in range(count):
        m = jnp.max(score, axis=axis, keepdims=True)
        idx = jnp.min(jnp.where(score == m, blk, nblk), axis=axis, keepdims=True)
        hit = blk == idx
        sel = sel | hit
        score = jnp.where(hit, M_INIT, score)
    return sel


def _nsa_cw_body(qt_ref, gt_ref, kc_ref, vct_ref, kw_ref, vwt_ref, ocw_ref, selb_ref,
                 *, seq, ng, rep, dh, s_scale):
    c = pl.program_id(1)
    qb = qt_ref.shape[1]
    nblk = kc_ref.shape[0]
    band = min(NSA_WINDOW + qb, seq)
    st = pl.multiple_of(jnp.maximum(c * qb - NSA_WINDOW, 0), qb)
    cols = rep * qb
    t_q = c * qb + lax.broadcasted_iota(jnp.int32, (1, qb), 1)
    t_c = jnp.concatenate([t_q] * rep, axis=1)
    ok_c = (lax.broadcasted_iota(jnp.int32, (nblk, cols), 0) + 1) * NSA_BLOCK - 1 <= t_c
    kp = st + lax.broadcasted_iota(jnp.int32, (band, cols), 0)
    ok_w = (kp <= t_c) & (kp > t_c - NSA_WINDOW)
    blk = lax.broadcasted_iota(jnp.int32, (nblk, qb), 0)
    cur = t_q // NSA_BLOCK
    for gi in range(ng):
        heads = [gi * rep + r for r in range(rep)]
        lanes = slice(gi * dh, (gi + 1) * dh)
        qg = jnp.concatenate([qt_ref[h * dh:(h + 1) * dh, :] for h in heads], axis=1)
        s_c = _dot(kc_ref[:, lanes].astype(BF16), qg)
        if s_scale != 1.0:
            s_c = s_c * s_scale
        p_c = _msoftmax(s_c, ok_c, axis=0)
        o_c = _dot(vct_ref[lanes, :].astype(BF16), p_c.astype(BF16))
        imp = p_c[:, 0:qb]
        for r in range(1, rep):
            imp = imp + p_c[:, r * qb:(r + 1) * qb]
        score = jnp.where(blk == cur, SEL_FORCE, jnp.where(blk < cur, imp, -1.0))
        sel = _select_blocks(score, blk, nblk, min(NSA_TOPN, nblk), axis=0)
        selb_ref[gi] = jnp.where(sel, 0.0, NEG).astype(BF16)
        s_w = _dot(kw_ref[gi, pl.ds(st, band), :], qg)
        if s_scale != 1.0:
            s_w = s_w * s_scale
        s_w = jnp.where(ok_w, s_w, NEG)
        e_w = jnp.exp(s_w - jnp.max(s_w, axis=0, keepdims=True))
        o_w = _dot(vwt_ref[lanes, pl.ds(st, band)], e_w.astype(BF16)) / jnp.sum(e_w, axis=0, keepdims=True)
        g_c = jnp.concatenate([gt_ref[3 * h:3 * h + 1, :] for h in heads], axis=1)
        g_w = jnp.concatenate([gt_ref[3 * h + 2:3 * h + 3, :] for h in heads], axis=1)
        ocw = g_c * o_c + g_w * o_w
        for r, h in enumerate(heads):
            ocw_ref[h * dh:(h + 1) * dh, :] = ocw[:, r * qb:(r + 1) * qb]


def _nsa_cw_call(qt, gates_t, kcmp, vcmp_t, kwg, vwt, *, nbatch, seq, ng, dh, s_scale):
    qw, t = qt.shape
    kvw = ng * dh
    rep = qw // kvw
    qb = min(Q_BLOCK, seq)
    nq = seq // qb
    nblk = seq // NSA_BLOCK
    body = functools.partial(_nsa_cw_body, seq=seq, ng=ng, rep=rep, dh=dh, s_scale=s_scale)
    qcol = lambda n: pl.BlockSpec((n, qb), lambda b, c: (0, b * nq + c))
    return pl.pallas_call(
        body, name="nsa_cmp_win",
        grid=(nbatch, nq),
        in_specs=[qcol(qw), qcol(LANES),
                  pl.BlockSpec((nblk, kvw), lambda b, c: (b, 0)), pl.BlockSpec((None, kvw, nblk), lambda b, c: (b, 0, 0)),
                  pl.BlockSpec((ng, seq, dh), lambda b, c: (0, b, 0)), pl.BlockSpec((kvw, seq), lambda b, c: (0, b))],
        out_specs=[qcol(qw), pl.BlockSpec((ng, None, nblk, qb), lambda b, c: (0, b * nq + c, 0, 0))],
        out_shape=[jax.ShapeDtypeStruct((qw, t), F32), jax.ShapeDtypeStruct((ng, nbatch * nq, nblk, qb), BF16)],
        compiler_params=_cp("parallel", "arbitrary"),
    )(qt, gates_t, kcmp, vcmp_t, kwg, vwt)


def _nsa_sel_body(qt_ref, selb_ref, ka_ref, vst_ref, ocw_ref, gt_ref, o_ref, m_sc, acc_sc, sa_sc, sb_sc, ma_sc, mb_sc,
                  *, rep, dh, kt, s_scale):
    g = pl.program_id(1)
    c = pl.program_id(2)
    qb = qt_ref.shape[1]
    cols = rep * qb
    bpt = kt // NSA_BLOCK
    qg = jnp.concatenate([qt_ref[r * dh:(r + 1) * dh, :] for r in range(rep)], axis=1)
    m_sc[...] = jnp.full(m_sc.shape, M_INIT, F32)
    acc_sc[...] = jnp.zeros(acc_sc.shape, F32)

    def scores(j, s_ref, mt_ref):
        st = pl.multiple_of(j * kt, kt)
        sb = selb_ref[pl.ds(pl.multiple_of(j * bpt, bpt), bpt), :]
        qa = jnp.concatenate([qg, jnp.concatenate([sb] * rep, axis=1)], axis=0)
        s = _dot(ka_ref[pl.ds(st, kt), :], qa)
        if s_scale != 1.0:
            s = s * s_scale
        s_ref[...] = s
        mt_ref[...] = jnp.max(s, axis=0, keepdims=True)

    def absorb(j, s, m_tile):
        st = pl.multiple_of(j * kt, kt)
        m_old = m_sc[...]
        m_new = jnp.maximum(m_old, m_tile)
        p = jnp.exp(s - m_new)
        acc_sc[...] = jnp.exp(m_old - m_new) * acc_sc[...] + _dot(vst_ref[:, pl.ds(st, kt)], p.astype(BF16))
        m_sc[...] = m_new

    def finish(s_ref):
        st = pl.multiple_of(n_full * kt, kt)
        kp = st + lax.broadcasted_iota(jnp.int32, (kt, cols), 0)
        t_c = c * qb + (lax.broadcasted_iota(jnp.int32, (kt, cols), 1) & (qb - 1))
        s = jnp.where(kp <= t_c, s_ref[...], NEG)
        absorb(n_full, s, jnp.max(s, axis=0, keepdims=True))
        acc = acc_sc[...]
        o_s = acc[0:dh] / acc[dh:dh + 1]
        out = []
        for r in range(rep):
            g_s = gt_ref[pl.ds(3 * (g * rep + r) + 1, 1), :]
            out.append(ocw_ref[r * dh:(r + 1) * dh, :] + g_s * o_s[:, r * qb:(r + 1) * qb])
        o_ref[...] = jnp.concatenate(out, axis=0).T.astype(o_ref.dtype)

    n_full = (c * qb) // kt
    scores(0, sa_sc, ma_sc)

    def pair(k, carry):
        scores(2 * k + 1, sb_sc, mb_sc)
        absorb(2 * k, sa_sc[...], ma_sc[...])
        scores(2 * k + 2, sa_sc, ma_sc)
        absorb(2 * k + 1, sb_sc[...], mb_sc[...])
        return carry

    lax.fori_loop(0, n_full // 2, pair, 0)

    @pl.when(n_full % 2 == 1)
    def _():
        scores(n_full, sb_sc, mb_sc)
        absorb(n_full - 1, sa_sc[...], ma_sc[...])
        finish(sb_sc)

    @pl.when(n_full % 2 == 0)
    def _():
        finish(sa_sc)


def _nsa_sel_call(qt, selb, ka, vst, ocw_t, gates_t, *, nbatch, seq, ng, dh, s_scale):
    qw, t = qt.shape
    rep = qw // (ng * dh)
    qb = min(Q_BLOCK, seq)
    assert qb & (qb - 1) == 0
    nq = seq // qb
    nblk = selb.shape[2]
    kt = min(KV_TILE, seq)
    bpt = kt // NSA_BLOCK
    assert seq % kt == 0 and kt % qb == 0 and bpt % BF16_ROWS == 0 and ka.shape[-1] == dh + bpt
    body = functools.partial(_nsa_sel_body, rep=rep, dh=dh, kt=kt, s_scale=s_scale)
    qcol = pl.BlockSpec((rep * dh, qb), lambda b, g, c: (g, b * nq + c))
    return pl.pallas_call(
        body, name="nsa_selected",
        grid=(nbatch, ng, nq),
        in_specs=[qcol,
                  pl.BlockSpec((None, None, nblk, qb), lambda b, g, c: (g, b * nq + c, 0, 0)),
                  pl.BlockSpec((None, seq, dh + bpt), lambda b, g, c: (g, b, 0)),
                  pl.BlockSpec((None, dh + BF16_ROWS, seq), lambda b, g, c: (g, 0, b)),
                  qcol,
                  pl.BlockSpec((LANES, qb), lambda b, g, c: (0, b * nq + c))],
        out_specs=pl.BlockSpec((qb, rep * dh), lambda b, g, c: (b * nq + c, g)),
        out_shape=jax.ShapeDtypeStruct((t, qw), BF16),
        scratch_shapes=[pltpu.VMEM((1, rep * qb), F32), pltpu.VMEM((dh + BF16_ROWS, rep * qb), F32),
                        pltpu.VMEM((kt, rep * qb), F32), pltpu.VMEM((kt, rep * qb), F32),
                        pltpu.VMEM((1, rep * qb), F32), pltpu.VMEM((1, rep * qb), F32)],
        compiler_params=_cp("parallel", "parallel", "arbitrary"),
    )(qt, selb, ka, vst, ocw_t, gates_t)


def _page_specs(pps, kvw, page, layer):
    return [pl.BlockSpec((None, None, kvw, page), lambda b, j, pt, i=i: (layer, pt[b, j * pps + i], 0, 0))
            for i in range(pps)]


def _cmp_pages_body(pt_ref, *refs, pps):
    del pt_ref
    k_refs, v_refs = refs[:pps], refs[pps:2 * pps]
    wk_ref, wv_ref, kc_ref, vc_ref = refs[2 * pps:]
    j = pl.program_id(1)
    kvw, page = k_refs[0].shape
    nblk = kc_ref.shape[1]
    bpp = page // NSA_BLOCK

    @pl.when(j == 0)
    def _():
        kc_ref[...] = jnp.zeros(kc_ref.shape, F32)
        vc_ref[...] = jnp.zeros(vc_ref.shape, F32)

    row_blk = lax.broadcasted_iota(jnp.int32, (page, nblk), 0) // NSA_BLOCK
    col = lax.broadcasted_iota(jnp.int32, (page, nblk), 1)
    for page_refs, w_ref, o_ref in ((k_refs, wk_ref, kc_ref), (v_refs, wv_ref, vc_ref)):
        acc = jnp.zeros((kvw, nblk), F32)
        for i, r in enumerate(page_refs):
            y = r[...] * w_ref[...]
            hi = y.astype(BF16)
            lo = (y - hi.astype(F32)).astype(BF16)
            onehot = (col == (j * pps + i) * bpp + row_blk).astype(BF16)
            acc = acc + _dot(jnp.concatenate([hi, lo], axis=1), jnp.concatenate([onehot, onehot], axis=0))
        o_ref[...] += acc


def _cmp_pages_call(page_table, pool_k, pool_v, wk_t, wv_t, *, layer, pps):
    nbs, npg = page_table.shape
    _, _, kvw, page = pool_k.shape
    nblk = npg * (page // NSA_BLOCK)
    assert npg % pps == 0
    wspec = pl.BlockSpec((kvw, page), lambda b, j, pt: (0, 0))
    grid_spec = pltpu.PrefetchScalarGridSpec(
        num_scalar_prefetch=1, grid=(nbs, npg // pps),
        in_specs=_page_specs(pps, kvw, page, layer) * 2 + [wspec] * 2,
        out_specs=[pl.BlockSpec((None, kvw, nblk), lambda b, j, pt: (b, 0, 0))] * 2)
    return pl.pallas_call(
        functools.partial(_cmp_pages_body, pps=pps), name="sample_cmp_pages",
        grid_spec=grid_spec,
        out_shape=[jax.ShapeDtypeStruct((nbs, kvw, nblk), F32)] * 2,
        compiler_params=_cp("parallel", "arbitrary"),
    )(page_table, *([pool_k] * pps), *([pool_v] * pps), wk_t, wv_t)


def _samp_cw_body(q_ref, g_ref, kct_ref, vct_ref, wk_ref, wv_ref, kn_ref, vn_ref, *refs,
                  ts, past, rep, scale, n_alias):
    ocw_ref, selb_ref, wko_ref, wvo_ref = refs[n_alias:]
    rows = q_ref.shape[0]
    kvw, nblk = kct_ref.shape
    wb = wk_ref.shape[1]
    npad = kn_ref.shape[1]
    gt = rows // rep
    q = q_ref[...]
    ti = lax.rem(lax.broadcasted_iota(jnp.int32, (rows, 1), 0), ts)
    t = past + ti
    blk_r = lax.broadcasted_iota(jnp.int32, (rows, nblk), 1)
    p_c = _msoftmax(_dot(q, kct_ref[...].astype(BF16)) * scale, (blk_r + 1) * NSA_BLOCK - 1 <= t)
    o_c = _dot_nt(p_c.astype(BF16), vct_ref[...].astype(BF16))
    imp = p_c[0:gt]
    for r in range(1, rep):
        imp = imp + p_c[r * gt:(r + 1) * gt]
    blk = lax.broadcasted_iota(jnp.int32, (gt, nblk), 1)
    sel = _select_blocks(imp, blk, nblk, min(NSA_TOPN - 1, nblk), axis=1)
    selb_ref[...] = jnp.concatenate([jnp.where(sel, 0.0, NEG)] * rep, axis=0)
    wk, wv, kn, vn = wk_ref[...], wv_ref[...], kn_ref[...], vn_ref[...]
    kp1 = past - wb + lax.broadcasted_iota(jnp.int32, (rows, wb), 1)
    j2 = lax.broadcasted_iota(jnp.int32, (rows, npad), 1)
    ok1 = (kp1 <= t) & (kp1 > t - NSA_WINDOW) & (kp1 >= 0)
    ok2 = (j2 <= ti) & (j2 < ts)
    s1 = jnp.where(ok1, _dot(q, wk.astype(BF16)) * scale, NEG)
    s2 = jnp.where(ok2, _dot(q, kn.astype(BF16)) * scale, NEG)
    m = jnp.maximum(jnp.max(s1, axis=1, keepdims=True), jnp.max(s2, axis=1, keepdims=True))
    e1 = jnp.where(ok1, jnp.exp(s1 - m), 0.0)
    e2 = jnp.where(ok2, jnp.exp(s2 - m), 0.0)
    den = jnp.maximum(jnp.sum(e1, axis=1, keepdims=True) + jnp.sum(e2, axis=1, keepdims=True), 1e-30)
    o_w = _dot_nt((e1 / den).astype(BF16), wv.astype(BF16)) + _dot_nt((e2 / den).astype(BF16), vn.astype(BF16))
    g = g_ref[...]
    ocw_ref[...] = g[:, 0:1] * o_c + g[:, 2:3] * o_w
    lane = lax.broadcasted_iota(jnp.int32, (kvw, wb), 1)

    def advance(old, new):
        tail = jnp.concatenate([jnp.zeros((kvw, wb - npad), F32), pltpu.roll(new, npad - ts, axis=1)], axis=1)
        return jnp.where(lane >= wb - ts, tail, pltpu.roll(old, wb - ts, axis=1))

    wko_ref[...] = advance(wk, kn)
    wvo_ref[...] = advance(wv, vn)


def _samp_cw_call(q_bd, gall, kct, vct, win_k, win_v, kn_t, vn_t, stacks, *, layer, depth, ts, past, rep, dh):
    nbs, rows, kvw = q_bd.shape
    nblk = kct.shape[2]
    wb = win_k.shape[3]
    npad = kn_t.shape[2]
    assert wb >= npad
    per_b = lambda *s: pl.BlockSpec((None,) + s, lambda b: (b,) + (0,) * len(s))
    win = pl.BlockSpec((None, None, kvw, wb), lambda b: (layer, b, 0, 0))
    n_alias = len(stacks)
    body = functools.partial(_samp_cw_body, ts=ts, past=past, rep=rep, scale=dh ** -0.5, n_alias=n_alias)
    return pl.pallas_call(
        body, name="sample_cmp_win",
        grid=(nbs,),
        in_specs=[per_b(rows, kvw), per_b(rows, LANES), per_b(kvw, nblk), per_b(kvw, nblk), win, win,
                  per_b(kvw, npad), per_b(kvw, npad)] + [pl.BlockSpec(memory_space=pl.ANY)] * n_alias,
        out_specs=[per_b(rows, kvw), per_b(rows, nblk), win, win],
        out_shape=[jax.ShapeDtypeStruct((nbs, rows, kvw), F32), jax.ShapeDtypeStruct((nbs, rows, nblk), F32),
                   jax.ShapeDtypeStruct((depth, nbs, kvw, wb), F32), jax.ShapeDtypeStruct((depth, nbs, kvw, wb), F32)],
        input_output_aliases={8 + k: 2 + k for k in range(n_alias)},
        compiler_params=_cp("parallel"),
    )(q_bd, gall, kct, vct, win_k, win_v, kn_t, vn_t, *stacks)


def _samp_sel_body(pt_ref, *refs, pps, ts, scale):
    del pt_ref
    k_refs, v_refs = refs[:pps], refs[pps:2 * pps]
    q_ref, selb_ref, kn_ref, vn_ref, ocw_ref, g_ref, o_ref, m_sc, l_sc, acc_sc = refs[2 * pps:]
    j = pl.program_id(1)
    rows = q_ref.shape[0]
    nblk = selb_ref.shape[1]
    page = k_refs[0].shape[1]
    kt = pps * page
    q = q_ref[...]

    @pl.when(j == 0)
    def _():
        m_sc[...] = jnp.full(m_sc.shape, M_INIT, F32)
        l_sc[...] = jnp.zeros(l_sc.shape, F32)
        acc_sc[...] = jnp.zeros(acc_sc.shape, F32)

    def update(s, v_t):
        m_old = m_sc[...]
        m_new = jnp.maximum(m_old, jnp.max(s, axis=1, keepdims=True))
        alpha = jnp.exp(m_old - m_new)
        p = jnp.exp(s - m_new)
        l_sc[...] = alpha * l_sc[...] + jnp.sum(p, axis=1, keepdims=True)
        acc_sc[...] = alpha * acc_sc[...] + _dot_nt(p.astype(BF16), v_t)
        m_sc[...] = m_new

    k_t = jnp.concatenate([r[...] for r in k_refs], axis=1).astype(BF16)
    v_t = jnp.concatenate([r[...] for r in v_refs], axis=1).astype(BF16)
    key_blk = j * (kt // NSA_BLOCK) + lax.broadcasted_iota(jnp.int32, (nblk, kt), 1) // NSA_BLOCK
    onehot = (key_blk == lax.broadcasted_iota(jnp.int32, (nblk, kt), 0)).astype(BF16)
    update(_dot(q, k_t) * scale + _dot(selb_ref[...].astype(BF16), onehot), v_t)

    @pl.when(j == pl.num_programs(1) - 1)
    def _():
        npad = kn_ref.shape[1]
        ti = lax.rem(lax.broadcasted_iota(jnp.int32, (rows, npad), 0), ts)
        j2 = lax.broadcasted_iota(jnp.int32, (rows, npad), 1)
        s = jnp.where((j2 <= ti) & (j2 < ts), _dot(q, kn_ref[...].astype(BF16)) * scale, NEG)
        update(s, vn_ref[...].astype(BF16))
        o_ref[...] = ocw_ref[...] + g_ref[:, 1:2] * (acc_sc[...] / l_sc[...])


def _samp_sel_call(page_table, pool_k, pool_v, q_bd, selb, kn_t, vn_t, ocw, gall, *, layer, pps, ts, dh):
    nbs, npg = page_table.shape
    _, _, kvw, page = pool_k.shape
    rows = q_bd.shape[1]
    per_b = lambda *s: pl.BlockSpec((None,) + s, lambda b, j, pt: (b,) + (0,) * len(s))
    npad = kn_t.shape[2]
    grid_spec = pltpu.PrefetchScalarGridSpec(
        num_scalar_prefetch=1, grid=(nbs, npg // pps),
        in_specs=_page_specs(pps, kvw, page, layer) * 2 + [
            per_b(rows, kvw), per_b(rows, selb.shape[2]), per_b(kvw, npad), per_b(kvw, npad),
            per_b(rows, kvw), per_b(rows, LANES)],
        out_specs=per_b(rows, kvw),
        scratch_shapes=[pltpu.VMEM((rows, 1), F32), pltpu.VMEM((rows, 1), F32), pltpu.VMEM((rows, kvw), F32)])
    body = functools.partial(_samp_sel_body, pps=pps, ts=ts, scale=dh ** -0.5)
    return pl.pallas_call(
        body, name="sample_selected",
        grid_spec=grid_spec,
        out_shape=jax.ShapeDtypeStruct((nbs, rows, kvw), F32),
        compiler_params=_cp("parallel", "arbitrary"),
    )(page_table, *([pool_k] * pps), *([pool_v] * pps), q_bd, selb, kn_t, vn_t, ocw, gall)


def _merge_body(x_ref, g_ref, wmg_ref, oa_ref, ob_ref, oc_ref, wb_ref, wo_ref, o_ref):
    x = x_ref[...]
    dm = x.shape[1]
    xn = _rms(x, g_ref[...]).astype(BF16)
    mg = jax.nn.sigmoid(_dot(xn, wmg_ref[...]))
    y = mg[:, 0:dm] * _dot(oa_ref[...], wb_ref[0])
    y = y + mg[:, dm:2 * dm] * _dot(ob_ref[...], wb_ref[1])
    y = y + mg[:, 2 * dm:3 * dm] * _dot(oc_ref[...], wb_ref[2])
    o_ref[...] = x + _dot(y.astype(BF16), wo_ref[...])


def _merge_call(x, g, wmg, oa, ob, oc, wb, wo, *, tm):
    t, dm = x.shape
    row = pl.BlockSpec((tm, dm), lambda i: (i, 0))
    return pl.pallas_call(
        _merge_body, name="merge",
        grid=(t // tm,),
        in_specs=[row, _full(g.shape), _full(wmg.shape), row, row, row, _full(wb.shape), _full(wo.shape)],
        out_specs=row,
        out_shape=jax.ShapeDtypeStruct((t, dm), F32),
        compiler_params=_cp("parallel"),
    )(x, g, wmg, oa, ob, oc, wb, wo)


def _ffn_body(x_ref, g_ref, w1_ref, w3_ref, w2_ref, fg_ref, o_ref, xn_sc, acc_sc, *, final):
    f = pl.program_id(1)

    @pl.when(f == 0)
    def _():
        xn_sc[...] = _rms(x_ref[...], g_ref[...]).astype(BF16)
        acc_sc[...] = jnp.zeros(acc_sc.shape, F32)

    xn = xn_sc[...]
    h = jax.nn.silu(_dot(xn, w1_ref[...])) * _dot(xn, w3_ref[...])
    acc_sc[...] += _dot(h.astype(BF16), w2_ref[...])

    @pl.when(f == pl.num_programs(1) - 1)
    def _():
        y = x_ref[...] + acc_sc[...]
        o_ref[...] = _rms(y, fg_ref[...]) if final else y


def _ffn_tile(f_dim):
    half = f_dim // 2
    return half if f_dim % 2 == 0 and half % LANES == 0 else f_dim


def _ffn_call(x, g, w1, w3, w2, fg, *, tm, final):
    t, dm = x.shape
    fd = w1.shape[1]
    ft = _ffn_tile(fd)
    row = pl.BlockSpec((tm, dm), lambda i, f: (i, 0))
    vec = pl.BlockSpec((1, dm), lambda i, f: (0, 0))
    return pl.pallas_call(
        functools.partial(_ffn_body, final=final), name="ffn",
        grid=(t // tm, fd // ft),
        in_specs=[row, vec, pl.BlockSpec((dm, ft), lambda i, f: (0, f)), pl.BlockSpec((dm, ft), lambda i, f: (0, f)),
                  pl.BlockSpec((ft, dm), lambda i, f: (f, 0)), vec],
        out_specs=row,
        out_shape=jax.ShapeDtypeStruct((t, dm), F32),
        scratch_shapes=[pltpu.VMEM((tm, dm), BF16), pltpu.VMEM((tm, dm), F32)],
        compiler_params=_cp("parallel", "arbitrary"),
    )(x, g, w1, w3, w2, fg)


def _route_body(x_ref, g_ref, wr_ref, br_ref, xn_ref, comb_ref, key_ref, keyt_ref, cnt_ref):
    tm = x_ref.shape[0]
    xn = _rms(x_ref[...], g_ref[...])
    xn_ref[...] = xn.astype(BF16)
    logits = jnp.dot(xn, wr_ref[...], preferred_element_type=F32, precision=lax.Precision.HIGHEST) + br_ref[...]
    lane = lax.broadcasted_iota(jnp.int32, logits.shape, 1)
    nl = logits.shape[1]
    m1 = jnp.max(logits, axis=1, keepdims=True)
    i1 = jnp.min(jnp.where(logits == m1, lane, nl), axis=1, keepdims=True)
    rest = jnp.where(lane == i1, M_INIT, logits)
    m2 = jnp.max(rest, axis=1, keepdims=True)
    i2 = jnp.min(jnp.where(rest == m2, lane, nl), axis=1, keepdims=True)
    e2 = jnp.exp(m2 - m1)
    den = 1.0 + e2
    comb_ref[...] = jnp.where(lane == i1, 1.0 / den, 0.0) + jnp.where(lane == i2, e2 / den, 0.0)
    member = (lane == i1) | (lane == i2)
    earlier = lax.broadcasted_iota(jnp.int32, (tm, tm), 1) < lax.broadcasted_iota(jnp.int32, (tm, tm), 0)
    rank = _dot(earlier.astype(BF16), member.astype(BF16))
    key = jnp.where(member, rank, -1.0)
    key_ref[...] = key
    keyt_ref[...] = key.T
    count = jnp.sum(member.astype(F32), axis=0, keepdims=True).astype(jnp.int32)
    cnt_ref[...] = jnp.broadcast_to(count, cnt_ref.shape)


def _route_call(x, g, wr, br, *, tm):
    t, dm = x.shape
    nl = wr.shape[1]
    nt = t // tm
    row = lambda n: pl.BlockSpec((tm, n), lambda i: (i, 0))
    return pl.pallas_call(
        _route_body, name="moe_route",
        grid=(nt,),
        in_specs=[row(dm), _full(g.shape), _full(wr.shape), _full(br.shape)],
        out_specs=[row(dm), row(nl), row(nl), pl.BlockSpec((None, nl, tm), lambda i: (i, 0, 0)),
                   pl.BlockSpec((None, 8, nl), lambda i: (i, 0, 0))],
        out_shape=[jax.ShapeDtypeStruct((t, dm), BF16), jax.ShapeDtypeStruct((t, nl), F32),
                   jax.ShapeDtypeStruct((t, nl), F32), jax.ShapeDtypeStruct((nt, nl, tm), F32),
                   jax.ShapeDtypeStruct((nt, 8, nl), jnp.int32)],
        compiler_params=_cp("parallel"),
    )(x, g, wr, br)


def _moe_body(cnt_ref, x_ref, xn_ref, comb_ref, key_ref, keyt_ref, w1_ref, w3_ref, w2_ref, fg_ref, o_ref,
              xe_sc, ye_sc, acc_sc, *, cap, final):
    i = pl.program_id(0)
    e = pl.program_id(1)
    f = pl.program_id(2)
    tm = x_ref.shape[0]
    n_batches = (cnt_ref[i, e] + cap - 1) // cap
    rows_of = lambda b: pl.ds(pl.multiple_of(b * cap, cap), cap)

    @pl.when((e == 0) & (f == 0))
    def _():
        acc_sc[...] = jnp.zeros(acc_sc.shape, F32)

    @pl.when(f == 0)
    def _():
        key_row = keyt_ref[pl.ds(e, 1), :]

        def gather(b, carry):
            slot = (b * cap + lax.broadcasted_iota(jnp.int32, (cap, tm), 0)).astype(F32)
            xe_sc[rows_of(b), :] = _dot((key_row == slot).astype(BF16), xn_ref[...]).astype(BF16)
            return carry

        lax.fori_loop(0, n_batches, gather, 0)

    def expert(b, carry):
        xe = xe_sc[rows_of(b), :]
        h = jax.nn.silu(_dot(xe, w1_ref[...])) * _dot(xe, w3_ref[...])
        part = _dot(h.astype(BF16), w2_ref[...])

        @pl.when(f == 0)
        def _():
            ye_sc[rows_of(b), :] = part

        @pl.when(f != 0)
        def _():
            ye_sc[rows_of(b), :] += part

        return carry

    lax.fori_loop(0, n_batches, expert, 0)

    @pl.when(f == pl.num_programs(2) - 1)
    def _():
        lane = lax.broadcasted_iota(jnp.int32, key_ref.shape, 1)
        key_col = jnp.sum(jnp.where(lane == e, key_ref[...], 0.0), axis=1, keepdims=True)
        gate = jnp.sum(jnp.where(lane == e, comb_ref[...], 0.0), axis=1, keepdims=True)

        def scatter(b, carry):
            slot = (b * cap + lax.broadcasted_iota(jnp.int32, (tm, cap), 1)).astype(F32)
            onehot = (key_col == slot).astype(BF16)
            ye = ye_sc[rows_of(b), :]
            hi = ye.astype(BF16)
            lo = (ye - hi.astype(F32)).astype(BF16)
            back = _dot(jnp.concatenate([onehot, onehot], axis=1), jnp.concatenate([hi, lo], axis=0))
            acc_sc[...] += gate * back
            return carry

        lax.fori_loop(0, n_batches, scatter, 0)

    @pl.when((e == pl.num_programs(1) - 1) & (f == pl.num_programs(2) - 1))
    def _():
        y = x_ref[...] + acc_sc[...]
        o_ref[...] = _rms(y, fg_ref[...]) if final else y


def _moe_call(x, g, wr, br, w1, w3, w2, fg, *, tm, final):
    t, dm = x.shape
    ne, _, fd = w1.shape
    nl = wr.shape[1]
    ft = _ffn_tile(fd)
    cap = min(MOE_BATCH, tm)
    assert cap % BF16_ROWS == 0
    max_rows = -(-tm // cap) * cap
    xn, comb, key, key_t, counts = _route_call(x, g, wr, br, tm=tm)
    once = dict(pipeline_mode=pl.Buffered(1))
    row = lambda n: pl.BlockSpec((tm, n), lambda i, e, f, cnt: (i, 0), **once)
    grid_spec = pltpu.PrefetchScalarGridSpec(
        num_scalar_prefetch=1, grid=(t // tm, ne, fd // ft),
        in_specs=[row(dm), row(dm), row(nl), row(nl),
                  pl.BlockSpec((None, nl, tm), lambda i, e, f, cnt: (i, 0, 0), **once),
                  pl.BlockSpec((None, dm, ft), lambda i, e, f, cnt: (e, 0, f)),
                  pl.BlockSpec((None, dm, ft), lambda i, e, f, cnt: (e, 0, f)),
                  pl.BlockSpec((None, ft, dm), lambda i, e, f, cnt: (e, f, 0)),
                  pl.BlockSpec((1, dm), lambda i, e, f, cnt: (0, 0))],
        out_specs=pl.BlockSpec((tm, dm), lambda i, e, f, cnt: (i, 0)),
        scratch_shapes=[pltpu.VMEM((max_rows, dm), BF16), pltpu.VMEM((max_rows, dm), F32), pltpu.VMEM((tm, dm), F32)])
    return pl.pallas_call(
        functools.partial(_moe_body, cap=cap, final=final), name="moe",
        grid_spec=grid_spec,
        out_shape=jax.ShapeDtypeStruct((t, dm), F32),
        compiler_params=_cp("parallel", "arbitrary", "arbitrary"),
    )(counts[:, 0, :ne], x, xn, comb, key, key_t, w1, w3, w2, fg)


def kernel(x_prompt, x_sample, cache_cmp_k, cache_cmp_v, cache_slc_k, cache_slc_v, cache_win_k, cache_win_v, state_conv, state_lru, page_table, norm_mix_g, w_in, lru_conv_w, lru_conv_b, lru_wa, lru_ba, lru_wx, lru_bx, lru_lambda, nsa_cmp_wk, nsa_cmp_wv, gm_norm_g, gm_ws, gm_bs, w_branch, w_out, norm_ffn_g, ffn_w1, ffn_w3, ffn_w2, moe_wr, moe_br, moe_w1, moe_w3, moe_w2, final_norm_g):
    depth, dm, _ = w_in.shape
    nbp, seq, _ = x_prompt.shape
    nbs, ts, _ = x_sample.shape
    _, n_pool, page, ng, dh = cache_cmp_k.shape
    wd = state_lru.shape[-1]
    conv_w = lru_conv_w.shape[1]
    gmw = gm_norm_g.shape[-1]
    n_gm, chunk = gm_ws.shape[1], gm_ws.shape[2]
    kvw = ng * dh
    qw = NSA_HEADS * dh
    rep = NSA_HEADS // ng
    npg = page_table.shape[1]
    past = npg * page
    wbuf = cache_win_k.shape[2]
    tp, tsm = nbp * seq, nbs * ts
    ne = moe_wr.shape[-1]
    nblk = seq // NSA_BLOCK
    scale = dh ** -0.5
    s_scale = 1.0 if _is_power_of_two(scale) else scale

    assert seq % chunk == 0 and seq % LRU_CHUNK == 0 and seq >= NSA_WINDOW
    assert past % NSA_BLOCK == 0 and ts < NSA_BLOCK and ts < chunk and wbuf == NSA_WINDOW and nbs % 8 == 0
    assert past // NSA_BLOCK >= NSA_TOPN - 1 and page % LANES == 0

    offs = [0, wd, 2 * wd, 2 * wd + qw, 2 * wd + qw + 6 * kvw, 2 * wd + qw + 6 * kvw + 3 * NSA_HEADS]
    offs += [offs[-1] + 2 * gmw, offs[-1] + 2 * gmw + 3 * dm]
    w_in_b = w_in.astype(BF16)
    w_lru = w_in_b[:, :, 0:offs[2]]
    w_nsa = jnp.pad(w_in_b[:, :, offs[2]:offs[5]], ((0, 0), (0, 0), (0, LANES - 3 * NSA_HEADS)))
    w_uv = w_in_b[:, :, offs[5]:offs[6]]
    w_mg = w_in_b[:, :, offs[6]:offs[7]]
    wa_b, wx_b = lru_wa.astype(BF16), lru_wx.astype(BF16)
    wb_b, wo_b = w_branch.astype(BF16), w_out.astype(BF16)
    f1_b, f3_b, f2_b = ffn_w1.astype(BF16), ffn_w3.astype(BF16), ffn_w2.astype(BF16)
    m1_b, m3_b, m2_b = moe_w1.astype(BF16), moe_w3.astype(BF16), moe_w2.astype(BF16)
    wr_pad = jnp.pad(moe_wr, ((0, 0), (0, 0), (0, LANES - ne)))
    br_pad = jnp.pad(moe_br, ((0, 0), (0, LANES - ne)), constant_values=NEG)[:, None, :]
    wck = nsa_cmp_wk.reshape(depth, NSA_BLOCK, kvw)
    wcv = nsa_cmp_wv.reshape(depth, NSA_BLOCK, kvw)
    cmp_w_t = lambda w: jnp.tile(jnp.transpose(w, (0, 2, 3, 1)).reshape(depth, kvw, NSA_BLOCK), (1, 1, page // NSA_BLOCK))
    wck_t, wcv_t = cmp_w_t(nsa_cmp_wk), cmp_w_t(nsa_cmp_wv)
    gc = gmw // n_gm
    bs_full = jnp.repeat(jnp.swapaxes(gm_bs, 1, 2), gc, axis=2)
    ws_short = jnp.repeat(jnp.transpose(gm_ws[:, :, :ts, :ts], (0, 2, 3, 1)).reshape(depth, ts * ts, n_gm), gc, axis=2)
    vec = lambda a, l: a[l][None, :]
    pps = min(PAGES_PER_STEP, npg)
    feature_major = lambda a: jnp.transpose(a, (0, 1, 3, 4, 2)).reshape(a.shape[0], a.shape[1], kvw, a.shape[2])
    pool_ck, pool_cv = feature_major(cache_cmp_k), feature_major(cache_cmp_v)
    pool_sk, pool_sv = feature_major(cache_slc_k), feature_major(cache_slc_v)
    win_k, win_v = feature_major(cache_win_k), feature_major(cache_win_v)
    eye_g = jnp.eye(ng, dtype=BF16)

    tm_p = min(ROW_TILE, tp)
    xp = x_prompt.reshape(tp, dm)
    xs = x_sample.reshape(tsm, dm)
    zeros_cs = jnp.zeros((nbp, conv_w - 1, wd), F32)
    zeros_h = jnp.zeros((nbp, 1, wd), F32)
    to_tm = lambda a, n: jnp.swapaxes(a.reshape(nbs, n, -1), 0, 1).reshape(1, n * nbs, -1)
    from_tm = lambda a, n: jnp.swapaxes(a.reshape(n, nbs, -1), 0, 1)
    new_rows_t = lambda a: jnp.pad(jnp.swapaxes(a.reshape(nbs, ts, kvw), 1, 2), ((0, 0), (0, 0), (0, LANES - ts)))

    p_stacks, w_stacks = (), ()
    p_small = [[] for _ in range(2)]
    s_out = [[] for _ in range(7)]
    for l in range(depth):
        last = l == depth - 1
        gmix = vec(norm_mix_g, l)
        lru_w = (lru_conv_w[l], vec(lru_conv_b, l), wa_b[l], vec(lru_ba, l), wx_b[l], vec(lru_bx, l), vec(lru_lambda, l))

        o_a, conv_new, h_last = _lru_call(xp.reshape(nbp, seq, dm), gmix, w_lru[l], zeros_cs, zeros_h, *lru_w,
                                          nb=1, tch=min(LRU_CHUNK, seq), starts_at_zero=True)
        o_c = _gmlp_call(xp, gmix, w_uv[l], vec(gm_norm_g, l), gm_ws[l], bs_full[l], tm=tm_p)
        qt, gates_t, kcmp, vcmp, ka, vst, kwg, vwt, *p_stacks = _qkv_prompt_call(
            xp, gmix, w_nsa[l], wck[l], wcv[l], p_stacks, layer=l, depth=depth, nbatch=nbp, seq=seq, tm=tm_p,
            qw=qw, kvw=kvw, dh=dh)
        vcmp_t = jnp.swapaxes(vcmp.reshape(nbp, nblk, kvw), 1, 2)
        ocw_t, selb = _nsa_cw_call(qt, gates_t, kcmp, vcmp_t, kwg, vwt, nbatch=nbp, seq=seq, ng=ng, dh=dh,
                                   s_scale=s_scale)
        o_b = _nsa_sel_call(qt, selb, ka, vst, ocw_t, gates_t, nbatch=nbp, seq=seq, ng=ng, dh=dh, s_scale=s_scale)
        xp = _merge_call(xp, gmix, w_mg[l], o_a.reshape(tp, wd), o_b, o_c, wb_b[l], wo_b[l], tm=tm_p)
        if l % 2 == 0:
            xp = _ffn_call(xp, vec(norm_ffn_g, l), f1_b[l // 2], f3_b[l // 2], f2_b[l // 2], final_norm_g[None, :],
                           tm=tm_p, final=last)
        else:
            xp = _moe_call(xp, vec(norm_ffn_g, l), wr_pad[l // 2], br_pad[l // 2], m1_b[l // 2], m3_b[l // 2],
                           m2_b[l // 2], final_norm_g[None, :], tm=min(MOE_TILE, tp), final=last)
        p_small[0].append(conv_new)
        p_small[1].append(h_last.reshape(nbp, wd))

        xs_tm = to_tm(xs, ts)
        o_a, conv_new, h_last = _lru_call(xs_tm, gmix, w_lru[l], to_tm(state_conv[l], conv_w - 1),
                                          state_lru[l][None], *lru_w, nb=nbs, tch=ts, starts_at_zero=False)
        o_a = from_tm(o_a, ts).reshape(tsm, wd)
        o_c, gm_v = _gmlp_short_call(xs_tm[0], gmix, w_uv[l], vec(gm_norm_g, l), ws_short[l], bs_full[l][:ts],
                                     n=ts, nb=nbs)
        o_c = from_tm(o_c, ts).reshape(tsm, gmw)
        q, k_c, v_c, k_s, v_s, k_w, v_w, gates = _qkv_sample_call(xs, gmix, w_nsa[l], qw=qw, kvw=kvw, dh=dh)
        q5 = jnp.transpose(q.reshape(nbs, ts, ng, rep, dh), (0, 3, 2, 1, 4))
        q_bd = (q5[:, :, :, :, None, :] * eye_g[None, None, :, None, :, None]).reshape(nbs, rep * ng * ts, kvw)
        g5 = jnp.transpose(gates[:, :3 * NSA_HEADS].reshape(nbs, ts, ng, rep, 3), (0, 3, 2, 1, 4))
        gall = jnp.pad(g5.reshape(nbs, rep * ng * ts, 3), ((0, 0), (0, 0), (0, LANES - 3)))
        kct, vct = _cmp_pages_call(page_table, pool_ck, pool_cv, wck_t[l], wcv_t[l], layer=l, pps=pps)
        ocw, selb, *w_stacks = _samp_cw_call(
            q_bd, gall, kct, vct, win_k, win_v, new_rows_t(k_w), new_rows_t(v_w), w_stacks,
            layer=l, depth=depth, ts=ts, past=past, rep=rep, dh=dh)
        o_all = _samp_sel_call(page_table, pool_sk, pool_sv, q_bd, selb, new_rows_t(k_s), new_rows_t(v_s),
                               ocw, gall, layer=l, pps=pps, ts=ts, dh=dh)
        o6 = o_all.reshape(nbs, rep, ng, ts, ng, dh)
        o_b = jnp.stack([o6[:, :, gi, :, gi, :] for gi in range(ng)], axis=2)
        o_b = jnp.transpose(o_b, (0, 3, 2, 1, 4)).reshape(tsm, qw).astype(BF16)
        xs = _merge_call(xs, gmix, w_mg[l], o_a, o_b, o_c, wb_b[l], wo_b[l], tm=tsm)
        if l % 2 == 0:
            xs = _ffn_call(xs, vec(norm_ffn_g, l), f1_b[l // 2], f3_b[l // 2], f2_b[l // 2], final_norm_g[None, :],
                           tm=tsm, final=last)
        else:
            xs = _moe_call(xs, vec(norm_ffn_g, l), wr_pad[l // 2], br_pad[l // 2], m1_b[l // 2], m3_b[l // 2],
                           m2_b[l // 2], final_norm_g[None, :], tm=tsm, final=last)
        shs = (nbs, ts, ng, dh)
        for lst, a in zip(s_out, (k_c.reshape(shs), v_c.reshape(shs), k_s.reshape(shs), v_s.reshape(shs),
                                  from_tm(conv_new[0], conv_w - 1), h_last[0], from_tm(gm_v, ts))):
            lst.append(a)

    row_major = lambda a: jnp.transpose(a.reshape(a.shape[0], a.shape[1], ng, dh, a.shape[3]), (0, 1, 4, 2, 3))
    keep = min(NSA_WINDOW, seq)
    kc_st, vc_st, ks_st, vs_st, kw_st, vw_st = p_stacks
    s_small = [jnp.stack(a) for a in s_out]
    return (xp.reshape(nbp, seq, dm), xs.reshape(nbs, ts, dm),
            row_major(kc_st), row_major(vc_st), row_major(ks_st), row_major(vs_st),
            row_major(kw_st[..., seq - keep:]), row_major(vw_st[..., seq - keep:]),
            jnp.stack(p_small[0]), jnp.stack(p_small[1]),
            s_small[0], s_small[1], s_small[2], s_small[3],
            row_major(w_stacks[0]), row_major(w_stacks[1]),
            s_small[4], s_small[5], s_small[6])
```

```python
import functools
import math

import jax
import jax.numpy as jnp
from jax import lax
from jax.experimental import pallas as pl
from jax.experimental.pallas import tpu as pltpu

F32 = jnp.float32
BF16 = jnp.bfloat16

NSA_HEADS = 16
NSA_BLOCK = 64
NSA_TOPN = 8
NSA_WINDOW = 512
SEL_FORCE = 1.0e4
LRU_C = 8.0
TOP_K = 2
RMS_EPS = 1e-6
NEG = -1e30
M_INIT = -3.0e38

Q_BLOCK = 256
KV_TILE = 1024
LRU_CHUNK = 256
ROW_TILE = 512
MOE_TILE = 1024
MOE_BATCH = 288
PAGES_PER_STEP = 16
BF16_ROWS = 16
LANES = 128
VMEM_LIMIT = 56 * 2 ** 20


def _cp(*sem):
    return pltpu.CompilerParams(dimension_semantics=sem, vmem_limit_bytes=VMEM_LIMIT)


def _dot(a, b):
    return jnp.dot(a, b, preferred_element_type=F32)


def _dot_nt(a, b):
    return lax.dot_general(a, b, (((1,), (1,)), ((), ())), preferred_element_type=F32)


def _rms(x, g):
    return x * lax.rsqrt(jnp.mean(x * x, axis=-1, keepdims=True) + RMS_EPS) * g


def _msoftmax(s, mask, axis=-1):
    s = jnp.where(mask, s, NEG)
    e = jnp.where(mask, jnp.exp(s - jnp.max(s, axis=axis, keepdims=True)), 0.0)
    return e / jnp.maximum(jnp.sum(e, axis=axis, keepdims=True), 1e-30)


def _is_power_of_two(x):
    return math.frexp(x)[0] == 0.5


def _full(shape):
    n = len(shape)
    return pl.BlockSpec(shape, lambda *_: (0,) * n)


def _lru_body(x_ref, g_ref, w_ref, cs_ref, h0_ref, cw_ref, cb_ref, wa_ref, ba_ref, wx_ref, bx_ref, lam_ref,
              oa_ref, cn_ref, hl_ref, xbuf, hc, *, nb, tch, conv_w, starts_at_zero):
    c = pl.program_id(1)
    rows = nb * tch
    wd = hc.shape[-1]
    nst = (conv_w - 1) * nb
    cr = xbuf.shape[0] - rows
    nh = wa_ref.shape[0]
    bw = wd // nh

    @pl.when(c == 0)
    def _():
        if cr > nst:
            xbuf[0:cr - nst, :] = jnp.zeros((cr - nst, wd), F32)
        xbuf[cr - nst:cr, :] = cs_ref[...]
        hc[...] = h0_ref[...]

    xn = _rms(x_ref[...], g_ref[...]).astype(BF16)
    z = _dot(xn, w_ref[...])
    gate = z[:, wd:]
    xbuf[cr:cr + rows, :] = z[:, :wd]
    y = cb_ref[...] + xbuf[cr - nst:cr - nst + rows, :] * cw_ref[0:1, :]
    for k in range(1, conv_w):
        st = cr - nst + k * nb
        y = y + xbuf[st:st + rows, :] * cw_ref[k:k + 1, :]
    cn_ref[...] = xbuf[cr + rows - nst:cr + rows, :]
    xbuf[0:cr, :] = xbuf[rows:rows + cr, :]

    yb = y.astype(BF16)

    def block_diag(wr):
        return jnp.concatenate([_dot(yb[:, j * bw:(j + 1) * bw], wr[j]) for j in range(nh)], axis=1)

    r = jax.nn.sigmoid(block_diag(wa_ref) + ba_ref[...])
    i = jax.nn.sigmoid(block_diag(wx_ref) + bx_ref[...])
    lam = lam_ref[...]
    softplus_neg_lam = jnp.maximum(-lam, 0.0) + jnp.log1p(jnp.exp(-jnp.abs(lam)))
    a = jnp.exp((-LRU_C) * r * softplus_neg_lam)
    mult = jnp.sqrt(1.0 - a * a)
    row = lax.broadcasted_iota(jnp.int32, (rows, wd), 0)
    if starts_at_zero:
        mult = jnp.where((row < nb) & (c == 0), 1.0, mult)
    u = mult * (i * y)
    if nb == 1:
        u = u + jnp.where(row < 1, a * hc[...], 0.0)
    else:
        u = jnp.concatenate([u[:nb] + a[:nb] * hc[...], u[nb:]], axis=0)
    d = nb
    while d < rows:
        a_sh = jnp.where(row < d, 1.0, pltpu.roll(a, d, axis=0))
        u_sh = jnp.where(row < d, 0.0, pltpu.roll(u, d, axis=0))
        u = a * u_sh + u
        a = a * a_sh
        d *= 2
    hc[...] = u[rows - nb:rows, :]
    hl_ref[...] = u[rows - nb:rows, :]
    oa_ref[...] = (jax.nn.gelu(gate) * u).astype(oa_ref.dtype)


def _lru_call(x, g, w, cs, h0, cw, cb, wa, ba, wx, bx, lam, *, nb, tch, starts_at_zero):
    nbatch, rows_total, dm = x.shape
    wd = h0.shape[-1]
    conv_w = cw.shape[0]
    nst = (conv_w - 1) * nb
    rows = nb * tch
    cr = -(-nst // 8) * 8
    assert rows_total % rows == 0 and rows >= cr
    body = functools.partial(_lru_body, nb=nb, tch=tch, conv_w=conv_w, starts_at_zero=starts_at_zero)
    return pl.pallas_call(
        body, name="lru",
        grid=(nbatch, rows_total // rows),
        in_specs=[pl.BlockSpec((None, rows, dm), lambda b, c: (b, c, 0)),
                  _full(g.shape), _full(w.shape),
                  pl.BlockSpec((None, nst, wd), lambda b, c: (b, 0, 0)),
                  pl.BlockSpec((None, nb, wd), lambda b, c: (b, 0, 0)),
                  _full(cw.shape), _full(cb.shape), _full(wa.shape), _full(ba.shape), _full(wx.shape),
                  _full(bx.shape), _full(lam.shape)],
        out_specs=[pl.BlockSpec((None, rows, wd), lambda b, c: (b, c, 0)),
                   pl.BlockSpec((None, nst, wd), lambda b, c: (b, 0, 0)),
                   pl.BlockSpec((None, nb, wd), lambda b, c: (b, 0, 0))],
        out_shape=[jax.ShapeDtypeStruct((nbatch, rows_total, wd), BF16),
                   jax.ShapeDtypeStruct((nbatch, nst, wd), F32),
                   jax.ShapeDtypeStruct((nbatch, nb, wd), F32)],
        scratch_shapes=[pltpu.VMEM((cr + rows, wd), F32), pltpu.VMEM((nb, wd), F32)],
        compiler_params=_cp("arbitrary", "arbitrary"),
    )(x, g, w, cs, h0, cw, cb, wa, ba, wx, bx, lam)


def _gmlp_body(x_ref, g_ref, w_ref, gg_ref, ws_ref, bs_ref, o_ref, *, chunk):
    tm = x_ref.shape[0]
    xn = _rms(x_ref[...], g_ref[...]).astype(BF16)
    ge = jax.nn.gelu(_dot(xn, w_ref[...]))
    wd = ge.shape[1] // 2
    u = ge[:, :wd]
    vb = _rms(ge[:, wd:], gg_ref[...]).astype(BF16)
    ng = ws_ref.shape[0]
    gc = wd // ng
    tril = lax.broadcasted_iota(jnp.int32, (chunk, chunk), 0) >= lax.broadcasted_iota(jnp.int32, (chunk, chunk), 1)
    wsm = [jnp.where(tril, ws_ref[gi], 0.0).astype(BF16) for gi in range(ng)]
    for ch in range(tm // chunk):
        lo, hi = ch * chunk, (ch + 1) * chunk
        mixed = jnp.concatenate([_dot(wsm[gi], vb[lo:hi, gi * gc:(gi + 1) * gc]) for gi in range(ng)], axis=1)
        o_ref[lo:hi, :] = (u[lo:hi] * (mixed + bs_ref[...])).astype(o_ref.dtype)


def _gmlp_call(x, g, w, gg, ws, bs_full, *, tm):
    t, dm = x.shape
    wd = gg.shape[-1]
    chunk = ws.shape[-1]
    assert t % tm == 0 and tm % chunk == 0
    return pl.pallas_call(
        functools.partial(_gmlp_body, chunk=chunk), name="gmlp",
        grid=(t // tm,),
        in_specs=[pl.BlockSpec((tm, dm), lambda i: (i, 0)), _full(g.shape), _full(w.shape), _full(gg.shape),
                  _full(ws.shape), _full(bs_full.shape)],
        out_specs=pl.BlockSpec((tm, wd), lambda i: (i, 0)),
        out_shape=jax.ShapeDtypeStruct((t, wd), BF16),
        compiler_params=_cp("parallel"),
    )(x, g, w, gg, ws, bs_full)


def _gmlp_short_body(x_ref, g_ref, w_ref, gg_ref, wexp_ref, bexp_ref, o_ref, v_ref, *, n, nb):
    xn = _rms(x_ref[...], g_ref[...]).astype(BF16)
    ge = jax.nn.gelu(_dot(xn, w_ref[...]))
    wd = ge.shape[1] // 2
    u = ge[:, :wd]
    v = _rms(ge[:, wd:], gg_ref[...])
    v_ref[...] = v
    for t in range(n):
        m = bexp_ref[t:t + 1, :] + wexp_ref[t * n:t * n + 1, :] * v[0:nb]
        for s in range(1, t + 1):
            m = m + wexp_ref[t * n + s:t * n + s + 1, :] * v[s * nb:(s + 1) * nb]
        o_ref[t * nb:(t + 1) * nb, :] = (u[t * nb:(t + 1) * nb] * m).astype(o_ref.dtype)


def _gmlp_short_call(x, g, w, gg, wexp, bexp, *, n, nb):
    t, dm = x.shape
    wd = gg.shape[-1]
    return pl.pallas_call(
        functools.partial(_gmlp_short_body, n=n, nb=nb), name="gmlp_short",
        grid=(1,),
        in_specs=[_full(x.shape), _full(g.shape), _full(w.shape), _full(gg.shape), _full(wexp.shape),
                  _full(bexp.shape)],
        out_specs=[_full((t, wd)), _full((t, wd))],
        out_shape=[jax.ShapeDtypeStruct((t, wd), BF16), jax.ShapeDtypeStruct((t, wd), F32)],
        compiler_params=_cp("arbitrary"),
    )(x, g, w, gg, wexp, bexp)


def _qkv_prompt_body(x_ref, g_ref, w_ref, wck_ref, wcv_ref, *refs, qw, kvw, dh, kt, q_scale, n_alias):
    (q_ref, gt_ref, kcmp_ref, vcmp_ref, ka_ref, vst_ref, kwg_ref, vwt_ref) = refs[n_alias:n_alias + 8]
    stack_refs = refs[n_alias + 8:]
    tm = x_ref.shape[0]
    ng = kvw // dh
    xn = _rms(x_ref[...], g_ref[...]).astype(BF16)
    z = _dot(xn, w_ref[...])
    parts = [z[:, qw + j * kvw:qw + (j + 1) * kvw] for j in range(6)]
    parts_t = [p.T for p in parts]
    for ref, p in zip(stack_refs, parts_t):
        ref[...] = p
    q_ref[...] = (z[:, :qw] * q_scale).T.astype(BF16)
    gt_ref[...] = jax.nn.sigmoid(z[:, qw + 6 * kvw:]).T
    nbt = tm // NSA_BLOCK
    kcmp_ref[...] = jnp.sum(parts[0].reshape(nbt, NSA_BLOCK, kvw) * wck_ref[...][None], axis=1)
    vcmp_ref[...] = jnp.sum(parts[1].reshape(nbt, NSA_BLOCK, kvw) * wcv_ref[...][None], axis=1)
    bpt = kt // NSA_BLOCK
    pos0 = lax.rem(pl.program_id(0) * tm, kt)
    blk = (pos0 + lax.broadcasted_iota(jnp.int32, (tm, bpt), 0)) // NSA_BLOCK
    onehot = (blk == lax.broadcasted_iota(jnp.int32, (tm, bpt), 1)).astype(BF16)
    ones_row = (lax.broadcasted_iota(jnp.int32, (BF16_ROWS, tm), 0) == 0).astype(BF16)
    ksb = parts[2].astype(BF16)
    kwb = parts[4].astype(BF16)
    vst = parts_t[3].astype(BF16)
    vwt_ref[...] = parts_t[5].astype(BF16)
    for gi in range(ng):
        ka_ref[gi, :, 0:dh] = ksb[:, gi * dh:(gi + 1) * dh]
        ka_ref[gi, :, dh:dh + bpt] = onehot
        kwg_ref[gi] = kwb[:, gi * dh:(gi + 1) * dh]
        vst_ref[gi, 0:dh, :] = vst[gi * dh:(gi + 1) * dh, :]
        vst_ref[gi, dh:dh + BF16_ROWS, :] = ones_row


def _qkv_prompt_call(x, g, w, wck, wcv, stacks, *, layer, depth, nbatch, seq, tm, qw, kvw, dh):
    t, dm = x.shape
    ng = kvw // dh
    kt = min(KV_TILE, seq)
    bpt = kt // NSA_BLOCK
    nbt = tm // NSA_BLOCK
    nst = seq // tm
    assert t % tm == 0 and tm % (8 * NSA_BLOCK) == 0 and seq % tm == 0 and kt % tm == 0 and tm % LANES == 0
    row = lambda n: pl.BlockSpec((tm, n), lambda i: (i, 0))
    col = lambda n: pl.BlockSpec((n, tm), lambda i: (0, i))
    stack_spec = pl.BlockSpec((None, None, kvw, tm), lambda i: (layer, i // nst, 0, i % nst))
    stack_shape = jax.ShapeDtypeStruct((depth, nbatch, kvw, seq), F32)
    out_specs = [col(qw), col(LANES), pl.BlockSpec((nbt, kvw), lambda i: (i, 0)), pl.BlockSpec((nbt, kvw), lambda i: (i, 0)),
                 pl.BlockSpec((ng, tm, dh + bpt), lambda i: (0, i, 0)),
                 pl.BlockSpec((ng, dh + BF16_ROWS, tm), lambda i: (0, 0, i)),
                 pl.BlockSpec((ng, tm, dh), lambda i: (0, i, 0)), col(kvw)] + [stack_spec] * 6
    out_shape = [jax.ShapeDtypeStruct((qw, t), BF16), jax.ShapeDtypeStruct((LANES, t), F32),
                 jax.ShapeDtypeStruct((t // NSA_BLOCK, kvw), F32), jax.ShapeDtypeStruct((t // NSA_BLOCK, kvw), F32),
                 jax.ShapeDtypeStruct((ng, t, dh + bpt), BF16), jax.ShapeDtypeStruct((ng, dh + BF16_ROWS, t), BF16),
                 jax.ShapeDtypeStruct((ng, t, dh), BF16), jax.ShapeDtypeStruct((kvw, t), BF16)] + [stack_shape] * 6
    scale = dh ** -0.5
    q_scale = scale if _is_power_of_two(scale) else 1.0
    n_alias = len(stacks)
    body = functools.partial(_qkv_prompt_body, qw=qw, kvw=kvw, dh=dh, kt=kt, q_scale=q_scale, n_alias=n_alias)
    return pl.pallas_call(
        body, name="qkv_prompt",
        grid=(t // tm,),
        in_specs=[row(dm), _full(g.shape), _full(w.shape), _full(wck.shape), _full(wcv.shape)]
        + [pl.BlockSpec(memory_space=pl.ANY)] * n_alias,
        out_specs=out_specs, out_shape=out_shape,
        input_output_aliases={5 + k: 8 + k for k in range(n_alias)},
        compiler_params=_cp("parallel"),
    )(x, g, w, wck, wcv, *stacks)


def _qkv_sample_body(x_ref, g_ref, w_ref, q_ref, kc_ref, vc_ref, ks_ref, vs_ref, kw_ref, vw_ref, gt_ref,
                     *, qw, kvw, dh):
    xn = _rms(x_ref[...], g_ref[...]).astype(BF16)
    z = _dot(xn, w_ref[...])
    for j, ref in enumerate((kc_ref, vc_ref, ks_ref, vs_ref, kw_ref, vw_ref)):
        for gi in range(kvw // dh):
            ref[:, gi, :] = z[:, qw + j * kvw + gi * dh:qw + j * kvw + (gi + 1) * dh]
    q_ref[...] = z[:, :qw].astype(BF16)
    gt_ref[...] = jax.nn.sigmoid(z[:, qw + 6 * kvw:])


def _qkv_sample_call(x, g, w, *, qw, kvw, dh):
    t, dm = x.shape
    ng = kvw // dh
    return pl.pallas_call(
        functools.partial(_qkv_sample_body, qw=qw, kvw=kvw, dh=dh), name="qkv_sample",
        grid=(1,),
        in_specs=[_full(x.shape), _full(g.shape), _full(w.shape)],
        out_specs=[_full((t, qw))] + [_full((t, ng, dh))] * 6 + [_full((t, LANES))],
        out_shape=[jax.ShapeDtypeStruct((t, qw), BF16)] + [jax.ShapeDtypeStruct((t, ng, dh), F32)] * 6
        + [jax.ShapeDtypeStruct((t, LANES), F32)],
        compiler_params=_cp("arbitrary"),
    )(x, g, w)


def _select_blocks(score, blk, nblk, count, axis):
    sel = jnp.zeros(score.shape, jnp.bool_)
    for _ in range(count):
        m = jnp.max(score, axis=axis, keepdims=True)
        idx = jnp.min(jnp.where(score == m, blk, nblk), axis=axis, keepdims=True)
        hit = blk == idx
        sel = sel | hit
        score = jnp.where(hit, M_INIT, score)
    return sel


def _nsa_cw_body(qt_ref, gt_ref, kc_ref, vct_ref, kw_ref, vwt_ref, ocw_ref, selb_ref,
                 *, seq, ng, rep, dh, s_scale):
    c = pl.program_id(1)
    qb = qt_ref.shape[1]
    nblk = kc_ref.shape[0]
    band = min(NSA_WINDOW + qb, seq)
    st = pl.multiple_of(jnp.maximum(c * qb - NSA_WINDOW, 0), qb)
    cols = rep * qb
    t_q = c * qb + lax.broadcasted_iota(jnp.int32, (1, qb), 1)
    t_c = jnp.concatenate([t_q] * rep, axis=1)
    ok_c = (lax.broadcasted_iota(jnp.int32, (nblk, cols), 0) + 1) * NSA_BLOCK - 1 <= t_c
    kp = st + lax.broadcasted_iota(jnp.int32, (band, cols), 0)
    ok_w = (kp <= t_c) & (kp > t_c - NSA_WINDOW)
    blk = lax.broadcasted_iota(jnp.int32, (nblk, qb), 0)
    cur = t_q // NSA_BLOCK
    for gi in range(ng):
        heads = [gi * rep + r for r in range(rep)]
        lanes = slice(gi * dh, (gi + 1) * dh)
        qg = jnp.concatenate([qt_ref[h * dh:(h + 1) * dh, :] for h in heads], axis=1)
        s_c = _dot(kc_ref[:, lanes].astype(BF16), qg)
        if s_scale != 1.0:
            s_c = s_c * s_scale
        p_c = _msoftmax(s_c, ok_c, axis=0)
        o_c = _dot(vct_ref[lanes, :].astype(BF16), p_c.astype(BF16))
        imp = p_c[:, 0:qb]
        for r in range(1, rep):
            imp = imp + p_c[:, r * qb:(r + 1) * qb]
        score = jnp.where(blk == cur, SEL_FORCE, jnp.where(blk < cur, imp, -1.0))
        sel = _select_blocks(score, blk, nblk, min(NSA_TOPN, nblk), axis=0)
        selb_ref[gi] = jnp.where(sel, 0.0, NEG).astype(BF16)
        s_w = _dot(kw_ref[gi, pl.ds(st, band), :], qg)
        if s_scale != 1.0:
            s_w = s_w * s_scale
        s_w = jnp.where(ok_w, s_w, NEG)
        e_w = jnp.exp(s_w - jnp.max(s_w, axis=0, keepdims=True))
        o_w = _dot(vwt_ref[lanes, pl.ds(st, band)], e_w.astype(BF16)) / jnp.sum(e_w, axis=0, keepdims=True)
        g_c = jnp.concatenate([gt_ref[3 * h:3 * h + 1, :] for h in heads], axis=1)
        g_w = jnp.concatenate([gt_ref[3 * h + 2:3 * h + 3, :] for h in heads], axis=1)
        ocw = g_c * o_c + g_w * o_w
        for r, h in enumerate(heads):
            ocw_ref[h * dh:(h + 1) * dh, :] = ocw[:, r * qb:(r + 1) * qb]


def _nsa_cw_call(qt, gates_t, kcmp, vcmp_t, kwg, vwt, *, nbatch, seq, ng, dh, s_scale):
    qw, t = qt.shape
    kvw = ng * dh
    rep = qw // kvw
    qb = min(Q_BLOCK, seq)
    nq = seq // qb
    nblk = seq // NSA_BLOCK
    body = functools.partial(_nsa_cw_body, seq=seq, ng=ng, rep=rep, dh=dh, s_scale=s_scale)
    qcol = lambda n: pl.BlockSpec((n, qb), lambda b, c: (0, b * nq + c))
    return pl.pallas_call(
        body, name="nsa_cmp_win",
        grid=(nbatch, nq),
        in_specs=[qcol(qw), qcol(LANES),
                  pl.BlockSpec((nblk, kvw), lambda b, c: (b, 0)), pl.BlockSpec((None, kvw, nblk), lambda b, c: (b, 0, 0)),
                  pl.BlockSpec((ng, seq, dh), lambda b, c: (0, b, 0)), pl.BlockSpec((kvw, seq), lambda b, c: (0, b))],
        out_specs=[qcol(qw), pl.BlockSpec((ng, None, nblk, qb), lambda b, c: (0, b * nq + c, 0, 0))],
        out_shape=[jax.ShapeDtypeStruct((qw, t), F32), jax.ShapeDtypeStruct((ng, nbatch * nq, nblk, qb), BF16)],
        compiler_params=_cp("parallel", "arbitrary"),
    )(qt, gates_t, kcmp, vcmp_t, kwg, vwt)


def _nsa_sel_body(qt_ref, selb_ref, ka_ref, vst_ref, ocw_ref, gt_ref, o_ref, m_sc, acc_sc, sa_sc, sb_sc, ma_sc, mb_sc,
                  *, rep, dh, kt, s_scale):
    g = pl.program_id(1)
    c = pl.program_id(2)
    qb = qt_ref.shape[1]
    cols = rep * qb
    bpt = kt // NSA_BLOCK
    qg = jnp.concatenate([qt_ref[r * dh:(r + 1) * dh, :] for r in range(rep)], axis=1)
    m_sc[...] = jnp.full(m_sc.shape, M_INIT, F32)
    acc_sc[...] = jnp.zeros(acc_sc.shape, F32)

    def scores(j, s_ref, mt_ref):
        st = pl.multiple_of(j * kt, kt)
        sb = selb_ref[pl.ds(pl.multiple_of(j * bpt, bpt), bpt), :]
        qa = jnp.concatenate([qg, jnp.concatenate([sb] * rep, axis=1)], axis=0)
        s = _dot(ka_ref[pl.ds(st, kt), :], qa)
        if s_scale != 1.0:
            s = s * s_scale
        s_ref[...] = s
        mt_ref[...] = jnp.max(s, axis=0, keepdims=True)

    def absorb(j, s, m_tile):
        st = pl.multiple_of(j * kt, kt)
        m_old = m_sc[...]
        m_new = jnp.maximum(m_old, m_tile)
        p = jnp.exp(s - m_new)
        acc_sc[...] = jnp.exp(m_old - m_new) * acc_sc[...] + _dot(vst_ref[:, pl.ds(st, kt)], p.astype(BF16))
        m_sc[...] = m_new

    def finish(s_ref):
        st = pl.multiple_of(n_full * kt, kt)
        kp = st + lax.broadcasted_iota(jnp.int32, (kt, cols), 0)
        t_c = c * qb + (lax.broadcasted_iota(jnp.int32, (kt, cols), 1) & (qb - 1))
        s = jnp.where(kp <= t_c, s_ref[...], NEG)
        absorb(n_full, s, jnp.max(s, axis=0, keepdims=True))
        acc = acc_sc[...]
        o_s = acc[0:dh] / acc[dh:dh + 1]
        out = []
        for r in range(rep):
            g_s = gt_ref[pl.ds(3 * (g * rep + r) + 1, 1), :]
            out.append(ocw_ref[r * dh:(r + 1) * dh, :] + g_s * o_s[:, r * qb:(r + 1) * qb])
        o_ref[...] = jnp.concatenate(out, axis=0).T.astype(o_ref.dtype)

    n_full = (c * qb) // kt
    scores(0, sa_sc, ma_sc)

    def pair(k, carry):
        scores(2 * k + 1, sb_sc, mb_sc)
        absorb(2 * k, sa_sc[...], ma_sc[...])
        scores(2 * k + 2, sa_sc, ma_sc)
        absorb(2 * k + 1, sb_sc[...], mb_sc[...])
        return carry

    lax.fori_loop(0, n_full // 2, pair, 0)

    @pl.when(n_full % 2 == 1)
    def _():
        scores(n_full, sb_sc, mb_sc)
        absorb(n_full - 1, sa_sc[...], ma_sc[...])
        finish(sb_sc)

    @pl.when(n_full % 2 == 0)
    def _():
        finish(sa_sc)


def _nsa_sel_call(qt, selb, ka, vst, ocw_t, gates_t, *, nbatch, seq, ng, dh, s_scale):
    qw, t = qt.shape
    rep = qw // (ng * dh)
    qb = min(Q_BLOCK, seq)
    assert qb & (qb - 1) == 0
    nq = seq // qb
    nblk = selb.shape[2]
    kt = min(KV_TILE, seq)
    bpt = kt // NSA_BLOCK
    assert seq % kt == 0 and kt % qb == 0 and bpt % BF16_ROWS == 0 and ka.shape[-1] == dh + bpt
    body = functools.partial(_nsa_sel_body, rep=rep, dh=dh, kt=kt, s_scale=s_scale)
    qcol = pl.BlockSpec((rep * dh, qb), lambda b, g, c: (g, b * nq + c))
    return pl.pallas_call(
        body, name="nsa_selected",
        grid=(nbatch, ng, nq),
        in_specs=[qcol,
                  pl.BlockSpec((None, None, nblk, qb), lambda b, g, c: (g, b * nq + c, 0, 0)),
                  pl.BlockSpec((None, seq, dh + bpt), lambda b, g, c: (g, b, 0)),
                  pl.BlockSpec((None, dh + BF16_ROWS, seq), lambda b, g, c: (g, 0, b)),
                  qcol,
                  pl.BlockSpec((LANES, qb), lambda b, g, c: (0, b * nq + c))],
        out_specs=pl.BlockSpec((qb, rep * dh), lambda b, g, c: (b * nq + c, g)),
        out_shape=jax.ShapeDtypeStruct((t, qw), BF16),
        scratch_shapes=[pltpu.VMEM((1, rep * qb), F32), pltpu.VMEM((dh + BF16_ROWS, rep * qb), F32),
                        pltpu.VMEM((kt, rep * qb), F32), pltpu.VMEM((kt, rep * qb), F32),
                        pltpu.VMEM((1, rep * qb), F32), pltpu.VMEM((1, rep * qb), F32)],
        compiler_params=_cp("parallel", "parallel", "arbitrary"),
    )(qt, selb, ka, vst, ocw_t, gates_t)


def _page_specs(pps, kvw, page, layer):
    return [pl.BlockSpec((None, None, kvw, page), lambda b, j, pt, i=i: (layer, pt[b, j * pps + i], 0, 0))
            for i in range(pps)]


def _cmp_pages_body(pt_ref, *refs, pps):
    del pt_ref
    k_refs, v_refs = refs[:pps], refs[pps:2 * pps]
    wk_ref, wv_ref, kc_ref, vc_ref = refs[2 * pps:]
    j = pl.program_id(1)
    kvw, page = k_refs[0].shape
    nblk = kc_ref.shape[1]
    bpp = page // NSA_BLOCK

    @pl.when(j == 0)
    def _():
        kc_ref[...] = jnp.zeros(kc_ref.shape, F32)
        vc_ref[...] = jnp.zeros(vc_ref.shape, F32)

    row_blk = lax.broadcasted_iota(jnp.int32, (page, nblk), 0) // NSA_BLOCK
    col = lax.broadcasted_iota(jnp.int32, (page, nblk), 1)
    for page_refs, w_ref, o_ref in ((k_refs, wk_ref, kc_ref), (v_refs, wv_ref, vc_ref)):
        acc = jnp.zeros((kvw, nblk), F32)
        for i, r in enumerate(page_refs):
            y = r[...] * w_ref[...]
            hi = y.astype(BF16)
            lo = (y - hi.astype(F32)).astype(BF16)
            onehot = (col == (j * pps + i) * bpp + row_blk).astype(BF16)
            acc = acc + _dot(jnp.concatenate([hi, lo], axis=1), jnp.concatenate([onehot, onehot], axis=0))
        o_ref[...] += acc


def _cmp_pages_call(page_table, pool_k, pool_v, wk_t, wv_t, *, layer, pps):
    nbs, npg = page_table.shape
    _, _, kvw, page = pool_k.shape
    nblk = npg * (page // NSA_BLOCK)
    assert npg % pps == 0
    wspec = pl.BlockSpec((kvw, page), lambda b, j, pt: (0, 0))
    grid_spec = pltpu.PrefetchScalarGridSpec(
        num_scalar_prefetch=1, grid=(nbs, npg // pps),
        in_specs=_page_specs(pps, kvw, page, layer) * 2 + [wspec] * 2,
        out_specs=[pl.BlockSpec((None, kvw, nblk), lambda b, j, pt: (b, 0, 0))] * 2)
    return pl.pallas_call(
        functools.partial(_cmp_pages_body, pps=pps), name="sample_cmp_pages",
        grid_spec=grid_spec,
        out_shape=[jax.ShapeDtypeStruct((nbs, kvw, nblk), F32)] * 2,
        compiler_params=_cp("parallel", "arbitrary"),
    )(page_table, *([pool_k] * pps), *([pool_v] * pps), wk_t, wv_t)


def _samp_cw_body(q_ref, g_ref, kct_ref, vct_ref, wk_ref, wv_ref, kn_ref, vn_ref, *refs,
                  ts, past, rep, scale, n_alias):
    ocw_ref, selb_ref, wko_ref, wvo_ref = refs[n_alias:]
    rows = q_ref.shape[0]
    kvw, nblk = kct_ref.shape
    wb = wk_ref.shape[1]
    npad = kn_ref.shape[1]
    gt = rows // rep
    q = q_ref[...]
    ti = lax.rem(lax.broadcasted_iota(jnp.int32, (rows, 1), 0), ts)
    t = past + ti
    blk_r = lax.broadcasted_iota(jnp.int32, (rows, nblk), 1)
    p_c = _msoftmax(_dot(q, kct_ref[...].astype(BF16)) * scale, (blk_r + 1) * NSA_BLOCK - 1 <= t)
    o_c = _dot_nt(p_c.astype(BF16), vct_ref[...].astype(BF16))
    imp = p_c[0:gt]
    for r in range(1, rep):
        imp = imp + p_c[r * gt:(r + 1) * gt]
    blk = lax.broadcasted_iota(jnp.int32, (gt, nblk), 1)
    sel = _select_blocks(imp, blk, nblk, min(NSA_TOPN - 1, nblk), axis=1)
    selb_ref[...] = jnp.concatenate([jnp.where(sel, 0.0, NEG)] * rep, axis=0)
    wk, wv, kn, vn = wk_ref[...], wv_ref[...], kn_ref[...], vn_ref[...]
    kp1 = past - wb + lax.broadcasted_iota(jnp.int32, (rows, wb), 1)
    j2 = lax.broadcasted_iota(jnp.int32, (rows, npad), 1)
    ok1 = (kp1 <= t) & (kp1 > t - NSA_WINDOW) & (kp1 >= 0)
    ok2 = (j2 <= ti) & (j2 < ts)
    s1 = jnp.where(ok1, _dot(q, wk.astype(BF16)) * scale, NEG)
    s2 = jnp.where(ok2, _dot(q, kn.astype(BF16)) * scale, NEG)
    m = jnp.maximum(jnp.max(s1, axis=1, keepdims=True), jnp.max(s2, axis=1, keepdims=True))
    e1 = jnp.where(ok1, jnp.exp(s1 - m), 0.0)
    e2 = jnp.where(ok2, jnp.exp(s2 - m), 0.0)
    den = jnp.maximum(jnp.sum(e1, axis=1, keepdims=True) + jnp.sum(e2, axis=1, keepdims=True), 1e-30)
    o_w = _dot_nt((e1 / den).astype(BF16), wv.astype(BF16)) + _dot_nt((e2 / den).astype(BF16), vn.astype(BF16))
    g = g_ref[...]
    ocw_ref[...] = g[:, 0:1] * o_c + g[:, 2:3] * o_w
    lane = lax.broadcasted_iota(jnp.int32, (kvw, wb), 1)

    def advance(old, new):
        tail = jnp.concatenate([jnp.zeros((kvw, wb - npad), F32), pltpu.roll(new, npad - ts, axis=1)], axis=1)
        return jnp.where(lane >= wb - ts, tail, pltpu.roll(old, wb - ts, axis=1))

    wko_ref[...] = advance(wk, kn)
    wvo_ref[...] = advance(wv, vn)


def _samp_cw_call(q_bd, gall, kct, vct, win_k, win_v, kn_t, vn_t, stacks, *, layer, depth, ts, past, rep, dh):
    nbs, rows, kvw = q_bd.shape
    nblk = kct.shape[2]
    wb = win_k.shape[3]
    npad = kn_t.shape[2]
    assert wb >= npad
    per_b = lambda *s: pl.BlockSpec((None,) + s, lambda b: (b,) + (0,) * len(s))
    win = pl.BlockSpec((None, None, kvw, wb), lambda b: (layer, b, 0, 0))
    n_alias = len(stacks)
    body = functools.partial(_samp_cw_body, ts=ts, past=past, rep=rep, scale=dh ** -0.5, n_alias=n_alias)
    return pl.pallas_call(
        body, name="sample_cmp_win",
        grid=(nbs,),
        in_specs=[per_b(rows, kvw), per_b(rows, LANES), per_b(kvw, nblk), per_b(kvw, nblk), win, win,
                  per_b(kvw, npad), per_b(kvw, npad)] + [pl.BlockSpec(memory_space=pl.ANY)] * n_alias,
        out_specs=[per_b(rows, kvw), per_b(rows, nblk), win, win],
        out_shape=[jax.ShapeDtypeStruct((nbs, rows, kvw), F32), jax.ShapeDtypeStruct((nbs, rows, nblk), F32),
                   jax.ShapeDtypeStruct((depth, nbs, kvw, wb), F32), jax.ShapeDtypeStruct((depth, nbs, kvw, wb), F32)],
        input_output_aliases={8 + k: 2 + k for k in range(n_alias)},
        compiler_params=_cp("parallel"),
    )(q_bd, gall, kct, vct, win_k, win_v, kn_t, vn_t, *stacks)


def _samp_sel_body(pt_ref, *refs, pps, ts, scale):
    del pt_ref
    k_refs, v_refs = refs[:pps], refs[pps:2 * pps]
    q_ref, selb_ref, kn_ref, vn_ref, ocw_ref, g_ref, o_ref, m_sc, l_sc, acc_sc = refs[2 * pps:]
    j = pl.program_id(1)
    rows = q_ref.shape[0]
    nblk = selb_ref.shape[1]
    page = k_refs[0].shape[1]
    kt = pps * page
    q = q_ref[...]

    @pl.when(j == 0)
    def _():
        m_sc[...] = jnp.full(m_sc.shape, M_INIT, F32)
        l_sc[...] = jnp.zeros(l_sc.shape, F32)
        acc_sc[...] = jnp.zeros(acc_sc.shape, F32)

    def update(s, v_t):
        m_old = m_sc[...]
        m_new = jnp.maximum(m_old, jnp.max(s, axis=1, keepdims=True))
        alpha = jnp.exp(m_old - m_new)
        p = jnp.exp(s - m_new)
        l_sc[...] = alpha * l_sc[...] + jnp.sum(p, axis=1, keepdims=True)
        acc_sc[...] = alpha * acc_sc[...] + _dot_nt(p.astype(BF16), v_t)
        m_sc[...] = m_new

    k_t = jnp.concatenate([r[...] for r in k_refs], axis=1).astype(BF16)
    v_t = jnp.concatenate([r[...] for r in v_refs], axis=1).astype(BF16)
    key_blk = j * (kt // NSA_BLOCK) + lax.broadcasted_iota(jnp.int32, (nblk, kt), 1) // NSA_BLOCK
    onehot = (key_blk == lax.broadcasted_iota(jnp.int32, (nblk, kt), 0)).astype(BF16)
    update(_dot(q, k_t) * scale + _dot(selb_ref[...].astype(BF16), onehot), v_t)

    @pl.when(j == pl.num_programs(1) - 1)
    def _():
        npad = kn_ref.shape[1]
        ti = lax.rem(lax.broadcasted_iota(jnp.int32, (rows, npad), 0), ts)
        j2 = lax.broadcasted_iota(jnp.int32, (rows, npad), 1)
        s = jnp.where((j2 <= ti) & (j2 < ts), _dot(q, kn_ref[...].astype(BF16)) * scale, NEG)
        update(s, vn_ref[...].astype(BF16))
        o_ref[...] = ocw_ref[...] + g_ref[:, 1:2] * (acc_sc[...] / l_sc[...])


def _samp_sel_call(page_table, pool_k, pool_v, q_bd, selb, kn_t, vn_t, ocw, gall, *, layer, pps, ts, dh):
    nbs, npg = page_table.shape
    _, _, kvw, page = pool_k.shape
    rows = q_bd.shape[1]
    per_b = lambda *s: pl.BlockSpec((None,) + s, lambda b, j, pt: (b,) + (0,) * len(s))
    npad = kn_t.shape[2]
    grid_spec = pltpu.PrefetchScalarGridSpec(
        num_scalar_prefetch=1, grid=(nbs, npg // pps),
        in_specs=_page_specs(pps, kvw, page, layer) * 2 + [
            per_b(rows, kvw), per_b(rows, selb.shape[2]), per_b(kvw, npad), per_b(kvw, npad),
            per_b(rows, kvw), per_b(rows, LANES)],
        out_specs=per_b(rows, kvw),
        scratch_shapes=[pltpu.VMEM((rows, 1), F32), pltpu.VMEM((rows, 1), F32), pltpu.VMEM((rows, kvw), F32)])
    body = functools.partial(_samp_sel_body, pps=pps, ts=ts, scale=dh ** -0.5)
    return pl.pallas_call(
        body, name="sample_selected",
        grid_spec=grid_spec,
        out_shape=jax.ShapeDtypeStruct((nbs, rows, kvw), F32),
        compiler_params=_cp("parallel", "arbitrary"),
    )(page_table, *([pool_k] * pps), *([pool_v] * pps), q_bd, selb, kn_t, vn_t, ocw, gall)


def _merge_body(x_ref, g_ref, wmg_ref, oa_ref, ob_ref, oc_ref, wb_ref, wo_ref, o_ref):
    x = x_ref[...]
    dm = x.shape[1]
    xn = _rms(x, g_ref[...]).astype(BF16)
    mg = jax.nn.sigmoid(_dot(xn, wmg_ref[...]))
    y = mg[:, 0:dm] * _dot(oa_ref[...], wb_ref[0])
    y = y + mg[:, dm:2 * dm] * _dot(ob_ref[...], wb_ref[1])
    y = y + mg[:, 2 * dm:3 * dm] * _dot(oc_ref[...], wb_ref[2])
    o_ref[...] = x + _dot(y.astype(BF16), wo_ref[...])


def _merge_call(x, g, wmg, oa, ob, oc, wb, wo, *, tm):
    t, dm = x.shape
    row = pl.BlockSpec((tm, dm), lambda i: (i, 0))
    return pl.pallas_call(
        _merge_body, name="merge",
        grid=(t // tm,),
        in_specs=[row, _full(g.shape), _full(wmg.shape), row, row, row, _full(wb.shape), _full(wo.shape)],
        out_specs=row,
        out_shape=jax.ShapeDtypeStruct((t, dm), F32),
        compiler_params=_cp("parallel"),
    )(x, g, wmg, oa, ob, oc, wb, wo)


def _ffn_body(x_ref, g_ref, w1_ref, w3_ref, w2_ref, fg_ref, o_ref, xn_sc, acc_sc, *, final):
    f = pl.program_id(1)

    @pl.when(f == 0)
    def _():
        xn_sc[...] = _rms(x_ref[...], g_ref[...]).astype(BF16)
        acc_sc[...] = jnp.zeros(acc_sc.shape, F32)

    xn = xn_sc[...]
    h = jax.nn.silu(_dot(xn, w1_ref[...])) * _dot(xn, w3_ref[...])
    acc_sc[...] += _dot(h.astype(BF16), w2_ref[...])

    @pl.when(f == pl.num_programs(1) - 1)
    def _():
        y = x_ref[...] + acc_sc[...]
        o_ref[...] = _rms(y, fg_ref[...]) if final else y


def _ffn_tile(f_dim):
    half = f_dim // 2
    return half if f_dim % 2 == 0 and half % LANES == 0 else f_dim


def _ffn_call(x, g, w1, w3, w2, fg, *, tm, final):
    t, dm = x.shape
    fd = w1.shape[1]
    ft = _ffn_tile(fd)
    row = pl.BlockSpec((tm, dm), lambda i, f: (i, 0))
    vec = pl.BlockSpec((1, dm), lambda i, f: (0, 0))
    return pl.pallas_call(
        functools.partial(_ffn_body, final=final), name="ffn",
        grid=(t // tm, fd // ft),
        in_specs=[row, vec, pl.BlockSpec((dm, ft), lambda i, f: (0, f)), pl.BlockSpec((dm, ft), lambda i, f: (0, f)),
                  pl.BlockSpec((ft, dm), lambda i, f: (f, 0)), vec],
        out_specs=row,
        out_shape=jax.ShapeDtypeStruct((t, dm), F32),
        scratch_shapes=[pltpu.VMEM((tm, dm), BF16), pltpu.VMEM((tm, dm), F32)],
        compiler_params=_cp("parallel", "arbitrary"),
    )(x, g, w1, w3, w2, fg)


def _route_body(x_ref, g_ref, wr_ref, br_ref, xn_ref, comb_ref, key_ref, keyt_ref, cnt_ref):
    tm = x_ref.shape[0]
    xn = _rms(x_ref[...], g_ref[...])
    xn_ref[...] = xn.astype(BF16)
    logits = jnp.dot(xn, wr_ref[...], preferred_element_type=F32, precision=lax.Precision.HIGHEST) + br_ref[...]
    lane = lax.broadcasted_iota(jnp.int32, logits.shape, 1)
    nl = logits.shape[1]
    m1 = jnp.max(logits, axis=1, keepdims=True)
    i1 = jnp.min(jnp.where(logits == m1, lane, nl), axis=1, keepdims=True)
    rest = jnp.where(lane == i1, M_INIT, logits)
    m2 = jnp.max(rest, axis=1, keepdims=True)
    i2 = jnp.min(jnp.where(rest == m2, lane, nl), axis=1, keepdims=True)
    e2 = jnp.exp(m2 - m1)
    den = 1.0 + e2
    comb_ref[...] = jnp.where(lane == i1, 1.0 / den, 0.0) + jnp.where(lane == i2, e2 / den, 0.0)
    member = (lane == i1) | (lane == i2)
    earlier = lax.broadcasted_iota(jnp.int32, (tm, tm), 1) < lax.broadcasted_iota(jnp.int32, (tm, tm), 0)
    rank = _dot(earlier.astype(BF16), member.astype(BF16))
    key = jnp.where(member, rank, -1.0)
    key_ref[...] = key
    keyt_ref[...] = key.T
    count = jnp.sum(member.astype(F32), axis=0, keepdims=True).astype(jnp.int32)
    cnt_ref[...] = jnp.broadcast_to(count, cnt_ref.shape)


def _route_call(x, g, wr, br, *, tm):
    t, dm = x.shape
    nl = wr.shape[1]
    nt = t // tm
    row = lambda n: pl.BlockSpec((tm, n), lambda i: (i, 0))
    return pl.pallas_call(
        _route_body, name="moe_route",
        grid=(nt,),
        in_specs=[row(dm), _full(g.shape), _full(wr.shape), _full(br.shape)],
        out_specs=[row(dm), row(nl), row(nl), pl.BlockSpec((None, nl, tm), lambda i: (i, 0, 0)),
                   pl.BlockSpec((None, 8, nl), lambda i: (i, 0, 0))],
        out_shape=[jax.ShapeDtypeStruct((t, dm), BF16), jax.ShapeDtypeStruct((t, nl), F32),
                   jax.ShapeDtypeStruct((t, nl), F32), jax.ShapeDtypeStruct((nt, nl, tm), F32),
                   jax.ShapeDtypeStruct((nt, 8, nl), jnp.int32)],
        compiler_params=_cp("parallel"),
    )(x, g, wr, br)


def _moe_body(cnt_ref, x_ref, xn_ref, comb_ref, key_ref, keyt_ref, w1_ref, w3_ref, w2_ref, fg_ref, o_ref,
              xe_sc, ye_sc, acc_sc, *, cap, final):
    i = pl.program_id(0)
    e = pl.program_id(1)
    f = pl.program_id(2)
    tm = x_ref.shape[0]
    n_batches = (cnt_ref[i, e] + cap - 1) // cap
    rows_of = lambda b: pl.ds(pl.multiple_of(b * cap, cap), cap)

    @pl.when((e == 0) & (f == 0))
    def _():
        acc_sc[...] = jnp.zeros(acc_sc.shape, F32)

    @pl.when(f == 0)
    def _():
        key_row = keyt_ref[pl.ds(e, 1), :]

        def gather(b, carry):
            slot = (b * cap + lax.broadcasted_iota(jnp.int32, (cap, tm), 0)).astype(F32)
            xe_sc[rows_of(b), :] = _dot((key_row == slot).astype(BF16), xn_ref[...]).astype(BF16)
            return carry

        lax.fori_loop(0, n_batches, gather, 0)

    def expert(b, carry):
        xe = xe_sc[rows_of(b), :]
        h = jax.nn.silu(_dot(xe, w1_ref[...])) * _dot(xe, w3_ref[...])
        part = _dot(h.astype(BF16), w2_ref[...])

        @pl.when(f == 0)
        def _():
            ye_sc[rows_of(b), :] = part

        @pl.when(f != 0)
        def _():
            ye_sc[rows_of(b), :] += part

        return carry

    lax.fori_loop(0, n_batches, expert, 0)

    @pl.when(f == pl.num_programs(2) - 1)
    def _():
        lane = lax.broadcasted_iota(jnp.int32, key_ref.shape, 1)
        key_col = jnp.sum(jnp.where(lane == e, key_ref[...], 0.0), axis=1, keepdims=True)
        gate = jnp.sum(jnp.where(lane == e, comb_ref[...], 0.0), axis=1, keepdims=True)

        def scatter(b, carry):
            slot = (b * cap + lax.broadcasted_iota(jnp.int32, (tm, cap), 1)).astype(F32)
            onehot = (key_col == slot).astype(BF16)
            ye = ye_sc[rows_of(b), :]
            hi = ye.astype(BF16)
            lo = (ye - hi.astype(F32)).astype(BF16)
            back = _dot(jnp.concatenate([onehot, onehot], axis=1), jnp.concatenate([hi, lo], axis=0))
            acc_sc[...] += gate * back
            return carry

        lax.fori_loop(0, n_batches, scatter, 0)

    @pl.when((e == pl.num_programs(1) - 1) & (f == pl.num_programs(2) - 1))
    def _():
        y = x_ref[...] + acc_sc[...]
        o_ref[...] = _rms(y, fg_ref[...]) if final else y


def _moe_call(x, g, wr, br, w1, w3, w2, fg, *, tm, final):
    t, dm = x.shape
    ne, _, fd = w1.shape
    nl = wr.shape[1]
    ft = _ffn_tile(fd)
    cap = min(MOE_BATCH, tm)
    assert cap % BF16_ROWS == 0
    max_rows = -(-tm // cap) * cap
    xn, comb, key, key_t, counts = _route_call(x, g, wr, br, tm=tm)
    once = dict(pipeline_mode=pl.Buffered(1))
    row = lambda n: pl.BlockSpec((tm, n), lambda i, e, f, cnt: (i, 0), **once)
    grid_spec = pltpu.PrefetchScalarGridSpec(
        num_scalar_prefetch=1, grid=(t // tm, ne, fd // ft),
        in_specs=[row(dm), row(dm), row(nl), row(nl),
                  pl.BlockSpec((None, nl, tm), lambda i, e, f, cnt: (i, 0, 0), **once),
                  pl.BlockSpec((None, dm, ft), lambda i, e, f, cnt: (e, 0, f)),
                  pl.BlockSpec((None, dm, ft), lambda i, e, f, cnt: (e, 0, f)),
                  pl.BlockSpec((None, ft, dm), lambda i, e, f, cnt: (e, f, 0)),
                  pl.BlockSpec((1, dm), lambda i, e, f, cnt: (0, 0))],
        out_specs=pl.BlockSpec((tm, dm), lambda i, e, f, cnt: (i, 0)),
        scratch_shapes=[pltpu.VMEM((max_rows, dm), BF16), pltpu.VMEM((max_rows, dm), F32), pltpu.VMEM((tm, dm), F32)])
    return pl.pallas_call(
        functools.partial(_moe_body, cap=cap, final=final), name="moe",
        grid_spec=grid_spec,
        out_shape=jax.ShapeDtypeStruct((t, dm), F32),
        compiler_params=_cp("parallel", "arbitrary", "arbitrary"),
    )(counts[:, 0, :ne], x, xn, comb, key, key_t, w1, w3, w2, fg)


def kernel(x_prompt, x_sample, cache_cmp_k, cache_cmp_v, cache_slc_k, cache_slc_v, cache_win_k, cache_win_v, state_conv, state_lru, page_table, norm_mix_g, w_in, lru_conv_w, lru_conv_b, lru_wa, lru_ba, lru_wx, lru_bx, lru_lambda, nsa_cmp_wk, nsa_cmp_wv, gm_norm_g, gm_ws, gm_bs, w_branch, w_out, norm_ffn_g, ffn_w1, ffn_w3, ffn_w2, moe_wr, moe_br, moe_w1, moe_w3, moe_w2, final_norm_g):
    depth, dm, _ = w_in.shape
    nbp, seq, _ = x_prompt.shape
    nbs, ts, _ = x_sample.shape
    _, n_pool, page, ng, dh = cache_cmp_k.shape
    wd = state_lru.shape[-1]
    conv_w = lru_conv_w.shape[1]
    gmw = gm_norm_g.shape[-1]
    n_gm, chunk = gm_ws.shape[1], gm_ws.shape[2]
    kvw = ng * dh
    qw = NSA_HEADS * dh
    rep = NSA_HEADS // ng
    npg = page_table.shape[1]
    past = npg * page
    wbuf = cache_win_k.shape[2]
    tp, tsm = nbp * seq, nbs * ts
    ne = moe_wr.shape[-1]
    nblk = seq // NSA_BLOCK
    scale = dh ** -0.5
    s_scale = 1.0 if _is_power_of_two(scale) else scale

    assert seq % chunk == 0 and seq % LRU_CHUNK == 0 and seq >= NSA_WINDOW
    assert past % NSA_BLOCK == 0 and ts < NSA_BLOCK and ts < chunk and wbuf == NSA_WINDOW and nbs % 8 == 0
    assert past // NSA_BLOCK >= NSA_TOPN - 1 and page % LANES == 0

    offs = [0, wd, 2 * wd, 2 * wd + qw, 2 * wd + qw + 6 * kvw, 2 * wd + qw + 6 * kvw + 3 * NSA_HEADS]
    offs += [offs[-1] + 2 * gmw, offs[-1] + 2 * gmw + 3 * dm]
    w_in_b = w_in.astype(BF16)
    w_lru = w_in_b[:, :, 0:offs[2]]
    w_nsa = jnp.pad(w_in_b[:, :, offs[2]:offs[5]], ((0, 0), (0, 0), (0, LANES - 3 * NSA_HEADS)))
    w_uv = w_in_b[:, :, offs[5]:offs[6]]
    w_mg = w_in_b[:, :, offs[6]:offs[7]]
    wa_b, wx_b = lru_wa.astype(BF16), lru_wx.astype(BF16)
    wb_b, wo_b = w_branch.astype(BF16), w_out.astype(BF16)
    f1_b, f3_b, f2_b = ffn_w1.astype(BF16), ffn_w3.astype(BF16), ffn_w2.astype(BF16)
    m1_b, m3_b, m2_b = moe_w1.astype(BF16), moe_w3.astype(BF16), moe_w2.astype(BF16)
    wr_pad = jnp.pad(moe_wr, ((0, 0), (0, 0), (0, LANES - ne)))
    br_pad = jnp.pad(moe_br, ((0, 0), (0, LANES - ne)), constant_values=NEG)[:, None, :]
    wck = nsa_cmp_wk.reshape(depth, NSA_BLOCK, kvw)
    wcv = nsa_cmp_wv.reshape(depth, NSA_BLOCK, kvw)
    cmp_w_t = lambda w: jnp.tile(jnp.transpose(w, (0, 2, 3, 1)).reshape(depth, kvw, NSA_BLOCK), (1, 1, page // NSA_BLOCK))
    wck_t, wcv_t = cmp_w_t(nsa_cmp_wk), cmp_w_t(nsa_cmp_wv)
    gc = gmw // n_gm
    bs_full = jnp.repeat(jnp.swapaxes(gm_bs, 1, 2), gc, axis=2)
    ws_short = jnp.repeat(jnp.transpose(gm_ws[:, :, :ts, :ts], (0, 2, 3, 1)).reshape(depth, ts * ts, n_gm), gc, axis=2)
    vec = lambda a, l: a[l][None, :]
    pps = min(PAGES_PER_STEP, npg)
    feature_major = lambda a: jnp.transpose(a, (0, 1, 3, 4, 2)).reshape(a.shape[0], a.shape[1], kvw, a.shape[2])
    pool_ck, pool_cv = feature_major(cache_cmp_k), feature_major(cache_cmp_v)
    pool_sk, pool_sv = feature_major(cache_slc_k), feature_major(cache_slc_v)
    win_k, win_v = feature_major(cache_win_k), feature_major(cache_win_v)
    eye_g = jnp.eye(ng, dtype=BF16)

    tm_p = min(ROW_TILE, tp)
    xp = x_prompt.reshape(tp, dm)
    xs = x_sample.reshape(tsm, dm)
    zeros_cs = jnp.zeros((nbp, conv_w - 1, wd), F32)
    zeros_h = jnp.zeros((nbp, 1, wd), F32)
    to_tm = lambda a, n: jnp.swapaxes(a.reshape(nbs, n, -1), 0, 1).reshape(1, n * nbs, -1)
    from_tm = lambda a, n: jnp.swapaxes(a.reshape(n, nbs, -1), 0, 1)
    new_rows_t = lambda a: jnp.pad(jnp.swapaxes(a.reshape(nbs, ts, kvw), 1, 2), ((0, 0), (0, 0), (0, LANES - ts)))

    p_stacks, w_stacks = (), ()
    p_small = [[] for _ in range(2)]
    s_out = [[] for _ in range(7)]
    for l in range(depth):
        last = l == depth - 1
        gmix = vec(norm_mix_g, l)
        lru_w = (lru_conv_w[l], vec(lru_conv_b, l), wa_b[l], vec(lru_ba, l), wx_b[l], vec(lru_bx, l), vec(lru_lambda, l))

        o_a, conv_new, h_last = _lru_call(xp.reshape(nbp, seq, dm), gmix, w_lru[l], zeros_cs, zeros_h, *lru_w,
                                          nb=1, tch=min(LRU_CHUNK, seq), starts_at_zero=True)
        o_c = _gmlp_call(xp, gmix, w_uv[l], vec(gm_norm_g, l), gm_ws[l], bs_full[l], tm=tm_p)
        qt, gates_t, kcmp, vcmp, ka, vst, kwg, vwt, *p_stacks = _qkv_prompt_call(
            xp, gmix, w_nsa[l], wck[l], wcv[l], p_stacks, layer=l, depth=depth, nbatch=nbp, seq=seq, tm=tm_p,
            qw=qw, kvw=kvw, dh=dh)
        vcmp_t = jnp.swapaxes(vcmp.reshape(nbp, nblk, kvw), 1, 2)
        ocw_t, selb = _nsa_cw_call(qt, gates_t, kcmp, vcmp_t, kwg, vwt, nbatch=nbp, seq=seq, ng=ng, dh=dh,
                                   s_scale=s_scale)
        o_b = _nsa_sel_call(qt, selb, ka, vst, ocw_t, gates_t, nbatch=nbp, seq=seq, ng=ng, dh=dh, s_scale=s_scale)
        xp = _merge_call(xp, gmix, w_mg[l], o_a.reshape(tp, wd), o_b, o_c, wb_b[l], wo_b[l], tm=tm_p)
        if l % 2 == 0:
            xp = _ffn_call(xp, vec(norm_ffn_g, l), f1_b[l // 2], f3_b[l // 2], f2_b[l // 2], final_norm_g[None, :],
                           tm=tm_p, final=last)
        else:
            xp = _moe_call(xp, vec(norm_ffn_g, l), wr_pad[l // 2], br_pad[l // 2], m1_b[l // 2], m3_b[l // 2],
                           m2_b[l // 2], final_norm_g[None, :], tm=min(MOE_TILE, tp), final=last)
        p_small[0].append(conv_new)
        p_small[1].append(h_last.reshape(nbp, wd))

        xs_tm = to_tm(xs, ts)
        o_a, conv_new, h_last = _lru_call(xs_tm, gmix, w_lru[l], to_tm(state_conv[l], conv_w - 1),
                                          state_lru[l][None], *lru_w, nb=nbs, tch=ts, starts_at_zero=False)
        o_a = from_tm(o_a, ts).reshape(tsm, wd)
        o_c, gm_v = _gmlp_short_call(xs_tm[0], gmix, w_uv[l], vec(gm_norm_g, l), ws_short[l], bs_full[l][:ts],
                                     n=ts, nb=nbs)
        o_c = from_tm(o_c, ts).reshape(tsm, gmw)
        q, k_c, v_c, k_s, v_s, k_w, v_w, gates = _qkv_sample_call(xs, gmix, w_nsa[l], qw=qw, kvw=kvw, dh=dh)
        q5 = jnp.transpose(q.reshape(nbs, ts, ng, rep, dh), (0, 3, 2, 1, 4))
        q_bd = (q5[:, :, :, :, None, :] * eye_g[None, None, :, None, :, None]).reshape(nbs, rep * ng * ts, kvw)
        g5 = jnp.transpose(gates[:, :3 * NSA_HEADS].reshape(nbs, ts, ng, rep, 3), (0, 3, 2, 1, 4))
        gall = jnp.pad(g5.reshape(nbs, rep * ng * ts, 3), ((0, 0), (0, 0), (0, LANES - 3)))
        kct, vct = _cmp_pages_call(page_table, pool_ck, pool_cv, wck_t[l], wcv_t[l], layer=l, pps=pps)
        ocw, selb, *w_stacks = _samp_cw_call(
            q_bd, gall, kct, vct, win_k, win_v, new_rows_t(k_w), new_rows_t(v_w), w_stacks,
            layer=l, depth=depth, ts=ts, past=past, rep=rep, dh=dh)
        o_all = _samp_sel_call(page_table, pool_sk, pool_sv, q_bd, selb, new_rows_t(k_s), new_rows_t(v_s),
                               ocw, gall, layer=l, pps=pps, ts=ts, dh=dh)
        o6 = o_all.reshape(nbs, rep, ng, ts, ng, dh)
        o_b = jnp.stack([o6[:, :, gi, :, gi, :] for gi in range(ng)], axis=2)
        o_b = jnp.transpose(o_b, (0, 3, 2, 1, 4)).reshape(tsm, qw).astype(BF16)
        xs = _merge_call(xs, gmix, w_mg[l], o_a, o_b, o_c, wb_b[l], wo_b[l], tm=tsm)
        if l % 2 == 0:
            xs = _ffn_call(xs, vec(norm_ffn_g, l), f1_b[l // 2], f3_b[l // 2], f2_b[l // 2], final_norm_g[None, :],
                           tm=tsm, final=last)
        else:
            xs = _moe_call(xs, vec(norm_ffn_g, l), wr_pad[l // 2], br_pad[l // 2], m1_b[l // 2], m3_b[l // 2],
                           m2_b[l // 2], final_norm_g[None, :], tm=tsm, final=last)
        shs = (nbs, ts, ng, dh)
        for lst, a in zip(s_out, (k_c.reshape(shs), v_c.reshape(shs), k_s.reshape(shs), v_s.reshape(shs),
                                  from_tm(conv_new[0], conv_w - 1), h_last[0], from_tm(gm_v, ts))):
            lst.append(a)

    row_major = lambda a: jnp.transpose(a.reshape(a.shape[0], a.shape[1], ng, dh, a.shape[3]), (0, 1, 4, 2, 3))
    keep = min(NSA_WINDOW, seq)
    kc_st, vc_st, ks_st, vs_st, kw_st, vw_st = p_stacks
    s_small = [jnp.stack(a) for a in s_out]
    return (xp.reshape(nbp, seq, dm), xs.reshape(nbs, ts, dm),
            row_major(kc_st), row_major(vc_st), row_major(ks_st), row_major(vs_st),
            row_major(kw_st[..., seq - keep:]), row_major(vw_st[..., seq - keep:]),
            jnp.stack(p_small[0]), jnp.stack(p_small[1]),
            s_small[0], s_small[1], s_small[2], s_small[3],
            row_major(w_stacks[0]), row_major(w_stacks[1]),
            s_small[4], s_small[5], s_small[6])
```

```python
import functools
import math

import jax
import jax.numpy as jnp
from jax import lax
from jax.experimental import pallas as pl
from jax.experimental.pallas import tpu as pltpu

F32 = jnp.float32
BF16 = jnp.bfloat16

NSA_HEADS = 16
NSA_BLOCK = 64
NSA_TOPN = 8
NSA_WINDOW = 512
SEL_FORCE = 1.0e4
LRU_C = 8.0
TOP_K = 2
RMS_EPS = 1e-6
NEG = -1e30
M_INIT = -3.0e38

Q_BLOCK = 256
KV_TILE = 1024
LRU_CHUNK = 256
ROW_TILE = 512
MOE_TILE = 1024
MOE_BATCH = 288
PAGES_PER_STEP = 32
BF16_ROWS = 16
LANES = 128
VMEM_LIMIT = 56 * 2 ** 20


def _cp(*sem):
    return pltpu.CompilerParams(dimension_semantics=sem, vmem_limit_bytes=VMEM_LIMIT)


def _dot(a, b):
    return jnp.dot(a, b, preferred_element_type=F32)


def _dot_nt(a, b):
    return lax.dot_general(a, b, (((1,), (1,)), ((), ())), preferred_element_type=F32)


def _rms(x, g):
    return x * lax.rsqrt(jnp.mean(x * x, axis=-1, keepdims=True) + RMS_EPS) * g


def _msoftmax(s, mask, axis=-1):
    s = jnp.where(mask, s, NEG)
    e = jnp.where(mask, jnp.exp(s - jnp.max(s, axis=axis, keepdims=True)), 0.0)
    return e / jnp.maximum(jnp.sum(e, axis=axis, keepdims=True), 1e-30)


def _is_power_of_two(x):
    return math.frexp(x)[0] == 0.5


def _full(shape):
    n = len(shape)
    return pl.BlockSpec(shape, lambda *_: (0,) * n)


def _lru_body(x_ref, g_ref, w_ref, cs_ref, h0_ref, cw_ref, cb_ref, wa_ref, ba_ref, wx_ref, bx_ref, lam_ref,
              oa_ref, cn_ref, hl_ref, xbuf, hc, *, nb, tch, conv_w, starts_at_zero):
    c = pl.program_id(1)
    rows = nb * tch
    wd = hc.shape[-1]
    nst = (conv_w - 1) * nb
    cr = xbuf.shape[0] - rows
    nh = wa_ref.shape[0]
    bw = wd // nh

    @pl.when(c == 0)
    def _():
        if cr > nst:
            xbuf[0:cr - nst, :] = jnp.zeros((cr - nst, wd), F32)
        xbuf[cr - nst:cr, :] = cs_ref[...]
        hc[...] = h0_ref[...]

    xn = _rms(x_ref[...], g_ref[...]).astype(BF16)
    z = _dot(xn, w_ref[...])
    gate = z[:, wd:]
    xbuf[cr:cr + rows, :] = z[:, :wd]
    y = cb_ref[...] + xbuf[cr - nst:cr - nst + rows, :] * cw_ref[0:1, :]
    for k in range(1, conv_w):
        st = cr - nst + k * nb
        y = y + xbuf[st:st + rows, :] * cw_ref[k:k + 1, :]
    cn_ref[...] = xbuf[cr + rows - nst:cr + rows, :]
    xbuf[0:cr, :] = xbuf[rows:rows + cr, :]

    yb = y.astype(BF16)

    def block_diag(wr):
        return jnp.concatenate([_dot(yb[:, j * bw:(j + 1) * bw], wr[j]) for j in range(nh)], axis=1)

    r = jax.nn.sigmoid(block_diag(wa_ref) + ba_ref[...])
    i = jax.nn.sigmoid(block_diag(wx_ref) + bx_ref[...])
    lam = lam_ref[...]
    softplus_neg_lam = jnp.maximum(-lam, 0.0) + jnp.log1p(jnp.exp(-jnp.abs(lam)))
    a = jnp.exp((-LRU_C) * r * softplus_neg_lam)
    mult = jnp.sqrt(1.0 - a * a)
    row = lax.broadcasted_iota(jnp.int32, (rows, wd), 0)
    if starts_at_zero:
        mult = jnp.where((row < nb) & (c == 0), 1.0, mult)
    u = mult * (i * y)
    if nb == 1:
        u = u + jnp.where(row < 1, a * hc[...], 0.0)
    else:
        u = jnp.concatenate([u[:nb] + a[:nb] * hc[...], u[nb:]], axis=0)
    d = nb
    while d < rows:
        a_sh = jnp.where(row < d, 1.0, pltpu.roll(a, d, axis=0))
        u_sh = jnp.where(row < d, 0.0, pltpu.roll(u, d, axis=0))
        u = a * u_sh + u
        a = a * a_sh
        d *= 2
    hc[...] = u[rows - nb:rows, :]
    hl_ref[...] = u[rows - nb:rows, :]
    oa_ref[...] = (jax.nn.gelu(gate) * u).astype(oa_ref.dtype)


def _lru_call(x, g, w, cs, h0, cw, cb, wa, ba, wx, bx, lam, *, nb, tch, starts_at_zero):
    nbatch, rows_total, dm = x.shape
    wd = h0.shape[-1]
    conv_w = cw.shape[0]
    nst = (conv_w - 1) * nb
    rows = nb * tch
    cr = -(-nst // 8) * 8
    assert rows_total % rows == 0 and rows >= cr
    body = functools.partial(_lru_body, nb=nb, tch=tch, conv_w=conv_w, starts_at_zero=starts_at_zero)
    return pl.pallas_call(
        body, name="lru",
        grid=(nbatch, rows_total // rows),
        in_specs=[pl.BlockSpec((None, rows, dm), lambda b, c: (b, c, 0)),
                  _full(g.shape), _full(w.shape),
                  pl.BlockSpec((None, nst, wd), lambda b, c: (b, 0, 0)),
                  pl.BlockSpec((None, nb, wd), lambda b, c: (b, 0, 0)),
                  _full(cw.shape), _full(cb.shape), _full(wa.shape), _full(ba.shape), _full(wx.shape),
                  _full(bx.shape), _full(lam.shape)],
        out_specs=[pl.BlockSpec((None, rows, wd), lambda b, c: (b, c, 0)),
                   pl.BlockSpec((None, nst, wd), lambda b, c: (b, 0, 0)),
                   pl.BlockSpec((None, nb, wd), lambda b, c: (b, 0, 0))],
        out_shape=[jax.ShapeDtypeStruct((nbatch, rows_total, wd), BF16),
                   jax.ShapeDtypeStruct((nbatch, nst, wd), F32),
                   jax.ShapeDtypeStruct((nbatch, nb, wd), F32)],
        scratch_shapes=[pltpu.VMEM((cr + rows, wd), F32), pltpu.VMEM((nb, wd), F32)],
        compiler_params=_cp("arbitrary", "arbitrary"),
    )(x, g, w, cs, h0, cw, cb, wa, ba, wx, bx, lam)


def _gmlp_body(x_ref, g_ref, w_ref, gg_ref, ws_ref, bs_ref, o_ref, *, chunk):
    tm = x_ref.shape[0]
    xn = _rms(x_ref[...], g_ref[...]).astype(BF16)
    ge = jax.nn.gelu(_dot(xn, w_ref[...]))
    wd = ge.shape[1] // 2
    u = ge[:, :wd]
    vb = _rms(ge[:, wd:], gg_ref[...]).astype(BF16)
    ng = ws_ref.shape[0]
    gc = wd // ng
    tril = lax.broadcasted_iota(jnp.int32, (chunk, chunk), 0) >= lax.broadcasted_iota(jnp.int32, (chunk, chunk), 1)
    wsm = [jnp.where(tril, ws_ref[gi], 0.0).astype(BF16) for gi in range(ng)]
    for ch in range(tm // chunk):
        lo, hi = ch * chunk, (ch + 1) * chunk
        mixed = jnp.concatenate([_dot(wsm[gi], vb[lo:hi, gi * gc:(gi + 1) * gc]) for gi in range(ng)], axis=1)
        o_ref[lo:hi, :] = (u[lo:hi] * (mixed + bs_ref[...])).astype(o_ref.dtype)


def _gmlp_call(x, g, w, gg, ws, bs_full, *, tm):
    t, dm = x.shape
    wd = gg.shape[-1]
    chunk = ws.shape[-1]
    assert t % tm == 0 and tm % chunk == 0
    return pl.pallas_call(
        functools.partial(_gmlp_body, chunk=chunk), name="gmlp",
        grid=(t // tm,),
        in_specs=[pl.BlockSpec((tm, dm), lambda i: (i, 0)), _full(g.shape), _full(w.shape), _full(gg.shape),
                  _full(ws.shape), _full(bs_full.shape)],
        out_specs=pl.BlockSpec((tm, wd), lambda i: (i, 0)),
        out_shape=jax.ShapeDtypeStruct((t, wd), BF16),
        compiler_params=_cp("parallel"),
    )(x, g, w, gg, ws, bs_full)


def _gmlp_short_body(x_ref, g_ref, w_ref, gg_ref, wexp_ref, bexp_ref, o_ref, v_ref, *, n, nb):
    xn = _rms(x_ref[...], g_ref[...]).astype(BF16)
    ge = jax.nn.gelu(_dot(xn, w_ref[...]))
    wd = ge.shape[1] // 2
    u = ge[:, :wd]
    v = _rms(ge[:, wd:], gg_ref[...])
    v_ref[...] = v
    for t in range(n):
        m = bexp_ref[t:t + 1, :] + wexp_ref[t * n:t * n + 1, :] * v[0:nb]
        for s in range(1, t + 1):
            m = m + wexp_ref[t * n + s:t * n + s + 1, :] * v[s * nb:(s + 1) * nb]
        o_ref[t * nb:(t + 1) * nb, :] = (u[t * nb:(t + 1) * nb] * m).astype(o_ref.dtype)


def _gmlp_short_call(x, g, w, gg, wexp, bexp, *, n, nb):
    t, dm = x.shape
    wd = gg.shape[-1]
    return pl.pallas_call(
        functools.partial(_gmlp_short_body, n=n, nb=nb), name="gmlp_short",
        grid=(1,),
        in_specs=[_full(x.shape), _full(g.shape), _full(w.shape), _full(gg.shape), _full(wexp.shape),
                  _full(bexp.shape)],
        out_specs=[_full((t, wd)), _full((t, wd))],
        out_shape=[jax.ShapeDtypeStruct((t, wd), BF16), jax.ShapeDtypeStruct((t, wd), F32)],
        compiler_params=_cp("arbitrary"),
    )(x, g, w, gg, wexp, bexp)


def _qkv_prompt_body(x_ref, g_ref, w_ref, wck_ref, wcv_ref, *refs, qw, kvw, dh, kt, q_scale, n_alias):
    (q_ref, gt_ref, kcmp_ref, vcmp_ref, ka_ref, vst_ref, kwg_ref, vwt_ref) = refs[n_alias:n_alias + 8]
    stack_refs = refs[n_alias + 8:]
    tm = x_ref.shape[0]
    ng = kvw // dh
    xn = _rms(x_ref[...], g_ref[...]).astype(BF16)
    z = _dot(xn, w_ref[...])
    parts = [z[:, qw + j * kvw:qw + (j + 1) * kvw] for j in range(6)]
    parts_t = [p.T for p in parts]
    for ref, p in zip(stack_refs, parts_t):
        ref[...] = p
    q_ref[...] = (z[:, :qw] * q_scale).T.astype(BF16)
    gt_ref[...] = jax.nn.sigmoid(z[:, qw + 6 * kvw:]).T
    nbt = tm // NSA_BLOCK
    kcmp_ref[...] = jnp.sum(parts[0].reshape(nbt, NSA_BLOCK, kvw) * wck_ref[...][None], axis=1)
    vcmp_ref[...] = jnp.sum(parts[1].reshape(nbt, NSA_BLOCK, kvw) * wcv_ref[...][None], axis=1)
    bpt = kt // NSA_BLOCK
    pos0 = lax.rem(pl.program_id(0) * tm, kt)
    blk = (pos0 + lax.broadcasted_iota(jnp.int32, (tm, bpt), 0)) // NSA_BLOCK
    onehot = (blk == lax.broadcasted_iota(jnp.int32, (tm, bpt), 1)).astype(BF16)
    ones_row = (lax.broadcasted_iota(jnp.int32, (BF16_ROWS, tm), 0) == 0).astype(BF16)
    ksb = parts[2].astype(BF16)
    kwb = parts[4].astype(BF16)
    vst = parts_t[3].astype(BF16)
    vwt = parts_t[5].astype(BF16)
    for gi in range(ng):
        ka_ref[gi, :, 0:dh] = ksb[:, gi * dh:(gi + 1) * dh]
        ka_ref[gi, :, dh:dh + bpt] = onehot
        kwg_ref[gi] = kwb[:, gi * dh:(gi + 1) * dh]
        for ref, v in ((vst_ref, vst), (vwt_ref, vwt)):
            ref[gi, 0:dh, :] = v[gi * dh:(gi + 1) * dh, :]
            ref[gi, dh:dh + BF16_ROWS, :] = ones_row


def _qkv_prompt_call(x, g, w, wck, wcv, stacks, *, layer, depth, nbatch, seq, tm, qw, kvw, dh):
    t, dm = x.shape
    ng = kvw // dh
    kt = min(KV_TILE, seq)
    bpt = kt // NSA_BLOCK
    nbt = tm // NSA_BLOCK
    nst = seq // tm
    assert t % tm == 0 and tm % (8 * NSA_BLOCK) == 0 and seq % tm == 0 and kt % tm == 0 and tm % LANES == 0
    row = lambda n: pl.BlockSpec((tm, n), lambda i: (i, 0))
    col = lambda n: pl.BlockSpec((n, tm), lambda i: (0, i))
    stack_spec = pl.BlockSpec((None, None, kvw, tm), lambda i: (layer, i // nst, 0, i % nst))
    stack_shape = jax.ShapeDtypeStruct((depth, nbatch, kvw, seq), F32)
    out_specs = [col(qw), col(LANES), pl.BlockSpec((nbt, kvw), lambda i: (i, 0)), pl.BlockSpec((nbt, kvw), lambda i: (i, 0)),
                 pl.BlockSpec((ng, tm, dh + bpt), lambda i: (0, i, 0)),
                 pl.BlockSpec((ng, dh + BF16_ROWS, tm), lambda i: (0, 0, i)),
                 pl.BlockSpec((ng, tm, dh), lambda i: (0, i, 0)),
                 pl.BlockSpec((ng, dh + BF16_ROWS, tm), lambda i: (0, 0, i))] + [stack_spec] * 6
    out_shape = [jax.ShapeDtypeStruct((qw, t), BF16), jax.ShapeDtypeStruct((LANES, t), F32),
                 jax.ShapeDtypeStruct((t // NSA_BLOCK, kvw), F32), jax.ShapeDtypeStruct((t // NSA_BLOCK, kvw), F32),
                 jax.ShapeDtypeStruct((ng, t, dh + bpt), BF16), jax.ShapeDtypeStruct((ng, dh + BF16_ROWS, t), BF16),
                 jax.ShapeDtypeStruct((ng, t, dh), BF16),
                 jax.ShapeDtypeStruct((ng, dh + BF16_ROWS, t), BF16)] + [stack_shape] * 6
    scale = dh ** -0.5
    q_scale = scale if _is_power_of_two(scale) else 1.0
    n_alias = len(stacks)
    body = functools.partial(_qkv_prompt_body, qw=qw, kvw=kvw, dh=dh, kt=kt, q_scale=q_scale, n_alias=n_alias)
    return pl.pallas_call(
        body, name="qkv_prompt",
        grid=(t // tm,),
        in_specs=[row(dm), _full(g.shape), _full(w.shape), _full(wck.shape), _full(wcv.shape)]
        + [pl.BlockSpec(memory_space=pl.ANY)] * n_alias,
        out_specs=out_specs, out_shape=out_shape,
        input_output_aliases={5 + k: 8 + k for k in range(n_alias)},
        compiler_params=_cp("parallel"),
    )(x, g, w, wck, wcv, *stacks)


def _qkv_sample_body(x_ref, g_ref, w_ref, q_ref, kc_ref, vc_ref, ks_ref, vs_ref, kw_ref, vw_ref, gt_ref,
                     *, qw, kvw, dh):
    xn = _rms(x_ref[...], g_ref[...]).astype(BF16)
    z = _dot(xn, w_ref[...])
    for j, ref in enumerate((kc_ref, vc_ref, ks_ref, vs_ref, kw_ref, vw_ref)):
        for gi in range(kvw // dh):
            ref[:, gi, :] = z[:, qw + j * kvw + gi * dh:qw + j * kvw + (gi + 1) * dh]
    q_ref[...] = z[:, :qw].astype(BF16)
    gt_ref[...] = jax.nn.sigmoid(z[:, qw + 6 * kvw:])


def _qkv_sample_call(x, g, w, *, qw, kvw, dh):
    t, dm = x.shape
    ng = kvw // dh
    return pl.pallas_call(
        functools.partial(_qkv_sample_body, qw=qw, kvw=kvw, dh=dh), name="qkv_sample",
        grid=(1,),
        in_specs=[_full(x.shape), _full(g.shape), _full(w.shape)],
        out_specs=[_full((t, qw))] + [_full((t, ng, dh))] * 6 + [_full((t, LANES))],
        out_shape=[jax.ShapeDtypeStruct((t, qw), BF16)] + [jax.ShapeDtypeStruct((t, ng, dh), F32)] * 6
        + [jax.ShapeDtypeStruct((t, LANES), F32)],
        compiler_params=_cp("arbitrary"),
    )(x, g, w)


def _select_blocks(score, blk, nblk, count, axis):
    for _ in range(count):
        m = jnp.max(score, axis=axis, keepdims=True)
        idx = jnp.min(jnp.where(score == m, blk, nblk), axis=axis, keepdims=True)
        score = jnp.where(blk == idx, M_INIT, score)
    return score == M_INIT


def _nsa_cw_body(qt_ref, gt_ref, kc_ref, vct_ref, kw_ref, vwt_ref, ocw_ref, selb_ref,
                 *, seq, ng, rep, dh, s_scale):
    c = pl.program_id(1)
    qb = qt_ref.shape[1]
    nblk = kc_ref.shape[0]
    band = min(NSA_WINDOW + qb, seq)
    st = pl.multiple_of(jnp.maximum(c * qb - NSA_WINDOW, 0), qb)
    cols = rep * qb
    t_q = c * qb + lax.broadcasted_iota(jnp.int32, (1, qb), 1)
    t_c = jnp.concatenate([t_q] * rep, axis=1)
    ok_c = (lax.broadcasted_iota(jnp.int32, (nblk, cols), 0) + 1) * NSA_BLOCK - 1 <= t_c
    kp = st + lax.broadcasted_iota(jnp.int32, (band, cols), 0)
    ok_w = (kp <= t_c) & (kp > t_c - NSA_WINDOW)
    blk = lax.broadcasted_iota(jnp.int32, (nblk, qb), 0)
    cur = t_q // NSA_BLOCK
    for gi in range(ng):
        heads = [gi * rep + r for r in range(rep)]
        lanes = slice(gi * dh, (gi + 1) * dh)
        qg = jnp.concatenate([qt_ref[h * dh:(h + 1) * dh, :] for h in heads], axis=1)
        s_c = _dot(kc_ref[:, lanes].astype(BF16), qg)
        if s_scale != 1.0:
            s_c = s_c * s_scale
        p_c = _msoftmax(s_c, ok_c, axis=0)
        o_c = _dot(vct_ref[lanes, :].astype(BF16), p_c.astype(BF16))
        imp = p_c[:, 0:qb]
        for r in range(1, rep):
            imp = imp + p_c[:, r * qb:(r + 1) * qb]
        score = jnp.where(blk == cur, SEL_FORCE, jnp.where(blk < cur, imp, -1.0))
        sel = _select_blocks(score, blk, nblk, min(NSA_TOPN, nblk), axis=0)
        selb_ref[gi] = jnp.where(sel, 0.0, NEG).astype(BF16)
        s_w = _dot(kw_ref[gi, pl.ds(st, band), :], qg)
        if s_scale != 1.0:
            s_w = s_w * s_scale
        s_w = jnp.where(ok_w, s_w, NEG)
        e_w = jnp.exp(s_w - jnp.max(s_w, axis=0, keepdims=True))
        ov = _dot(vwt_ref[gi, :, pl.ds(st, band)], e_w.astype(BF16))
        o_w = ov[0:dh] / ov[dh:dh + 1]
        g_c = jnp.concatenate([gt_ref[3 * h:3 * h + 1, :] for h in heads], axis=1)
        g_w = jnp.concatenate([gt_ref[3 * h + 2:3 * h + 3, :] for h in heads], axis=1)
        ocw = g_c * o_c + g_w * o_w
        for r, h in enumerate(heads):
            ocw_ref[h * dh:(h + 1) * dh, :] = ocw[:, r * qb:(r + 1) * qb]


def _nsa_cw_call(qt, gates_t, kcmp, vcmp_t, kwg, vwt, *, nbatch, seq, ng, dh, s_scale):
    qw, t = qt.shape
    kvw = ng * dh
    rep = qw // kvw
    qb = min(Q_BLOCK, seq)
    nq = seq // qb
    nblk = seq // NSA_BLOCK
    body = functools.partial(_nsa_cw_body, seq=seq, ng=ng, rep=rep, dh=dh, s_scale=s_scale)
    qcol = lambda n: pl.BlockSpec((n, qb), lambda b, c: (0, b * nq + c))
    return pl.pallas_call(
        body, name="nsa_cmp_win",
        grid=(nbatch, nq),
        in_specs=[qcol(qw), qcol(LANES),
                  pl.BlockSpec((nblk, kvw), lambda b, c: (b, 0)), pl.BlockSpec((None, kvw, nblk), lambda b, c: (b, 0, 0)),
                  pl.BlockSpec((ng, seq, dh), lambda b, c: (0, b, 0)),
                  pl.BlockSpec((ng, dh + BF16_ROWS, seq), lambda b, c: (0, 0, b))],
        out_specs=[qcol(qw), pl.BlockSpec((ng, None, nblk, qb), lambda b, c: (0, b * nq + c, 0, 0))],
        out_shape=[jax.ShapeDtypeStruct((qw, t), F32), jax.ShapeDtypeStruct((ng, nbatch * nq, nblk, qb), BF16)],
        compiler_params=_cp("parallel", "arbitrary"),
    )(qt, gates_t, kcmp, vcmp_t, kwg, vwt)


def _nsa_sel_body(qt_ref, selb_ref, ka_ref, vst_ref, ocw_ref, gt_ref, o_ref, m_sc, acc_sc, sa_sc, sb_sc, ma_sc, mb_sc,
                  *, rep, dh, kt, s_scale):
    g = pl.program_id(1)
    c = pl.program_id(2)
    qb = qt_ref.shape[1]
    cols = rep * qb
    bpt = kt // NSA_BLOCK
    qg = jnp.concatenate([qt_ref[r * dh:(r + 1) * dh, :] for r in range(rep)], axis=1)
    m_sc[...] = jnp.full(m_sc.shape, M_INIT, F32)
    acc_sc[...] = jnp.zeros(acc_sc.shape, F32)

    def scores(j, s_ref, mt_ref):
        st = pl.multiple_of(j * kt, kt)
        sb = selb_ref[pl.ds(pl.multiple_of(j * bpt, bpt), bpt), :]
        qa = jnp.concatenate([qg, jnp.concatenate([sb] * rep, axis=1)], axis=0)
        s = _dot(ka_ref[pl.ds(st, kt), :], qa)
        if s_scale != 1.0:
            s = s * s_scale
        s_ref[...] = s
        mt_ref[...] = jnp.max(s, axis=0, keepdims=True)

    def absorb(j, s, m_tile):
        st = pl.multiple_of(j * kt, kt)
        m_old = m_sc[...]
        m_new = jnp.maximum(m_old, m_tile)
        p = jnp.exp(s - m_new)
        acc_sc[...] = jnp.exp(m_old - m_new) * acc_sc[...] + _dot(vst_ref[:, pl.ds(st, kt)], p.astype(BF16))
        m_sc[...] = m_new

    def finish(s_ref):
        st = pl.multiple_of(n_full * kt, kt)
        kp = st + lax.broadcasted_iota(jnp.int32, (kt, cols), 0)
        t_c = c * qb + (lax.broadcasted_iota(jnp.int32, (kt, cols), 1) & (qb - 1))
        s = jnp.where(kp <= t_c, s_ref[...], NEG)
        absorb(n_full, s, jnp.max(s, axis=0, keepdims=True))
        acc = acc_sc[...]
        o_s = acc[0:dh] / acc[dh:dh + 1]
        out = []
        for r in range(rep):
            g_s = gt_ref[pl.ds(3 * (g * rep + r) + 1, 1), :]
            out.append(ocw_ref[r * dh:(r + 1) * dh, :] + g_s * o_s[:, r * qb:(r + 1) * qb])
        o_ref[...] = jnp.concatenate(out, axis=0).T.astype(o_ref.dtype)

    n_full = (c * qb) // kt
    scores(0, sa_sc, ma_sc)

    def pair(k, carry):
        scores(2 * k + 1, sb_sc, mb_sc)
        absorb(2 * k, sa_sc[...], ma_sc[...])
        scores(2 * k + 2, sa_sc, ma_sc)
        absorb(2 * k + 1, sb_sc[...], mb_sc[...])
        return carry

    lax.fori_loop(0, n_full // 2, pair, 0)

    @pl.when(n_full % 2 == 1)
    def _():
        scores(n_full, sb_sc, mb_sc)
        absorb(n_full - 1, sa_sc[...], ma_sc[...])
        finish(sb_sc)

    @pl.when(n_full % 2 == 0)
    def _():
        finish(sa_sc)


def _nsa_sel_call(qt, selb, ka, vst, ocw_t, gates_t, *, nbatch, seq, ng, dh, s_scale):
    qw, t = qt.shape
    rep = qw // (ng * dh)
    qb = min(Q_BLOCK, seq)
    assert qb & (qb - 1) == 0
    nq = seq // qb
    nblk = selb.shape[2]
    kt = min(KV_TILE, seq)
    bpt = kt // NSA_BLOCK
    assert seq % kt == 0 and kt % qb == 0 and bpt % BF16_ROWS == 0 and ka.shape[-1] == dh + bpt
    body = functools.partial(_nsa_sel_body, rep=rep, dh=dh, kt=kt, s_scale=s_scale)
    qcol = pl.BlockSpec((rep * dh, qb), lambda b, g, c: (g, b * nq + c))
    return pl.pallas_call(
        body, name="nsa_selected",
        grid=(nbatch, ng, nq),
        in_specs=[qcol,
                  pl.BlockSpec((None, None, nblk, qb), lambda b, g, c: (g, b * nq + c, 0, 0)),
                  pl.BlockSpec((None, seq, dh + bpt), lambda b, g, c: (g, b, 0)),
                  pl.BlockSpec((None, dh + BF16_ROWS, seq), lambda b, g, c: (g, 0, b)),
                  qcol,
                  pl.BlockSpec((LANES, qb), lambda b, g, c: (0, b * nq + c))],
        out_specs=pl.BlockSpec((qb, rep * dh), lambda b, g, c: (b * nq + c, g)),
        out_shape=jax.ShapeDtypeStruct((t, qw), BF16),
        scratch_shapes=[pltpu.VMEM((1, rep * qb), F32), pltpu.VMEM((dh + BF16_ROWS, rep * qb), F32),
                        pltpu.VMEM((kt, rep * qb), F32), pltpu.VMEM((kt, rep * qb), F32),
                        pltpu.VMEM((1, rep * qb), F32), pltpu.VMEM((1, rep * qb), F32)],
        compiler_params=_cp("parallel", "parallel", "arbitrary"),
    )(qt, selb, ka, vst, ocw_t, gates_t)


def _page_specs(pps, kvw, page, layer):
    return [pl.BlockSpec((None, None, kvw, page), lambda b, j, pt, i=i: (layer, pt[b, j * pps + i], 0, 0))
            for i in range(pps)]


def _cmp_pages_body(pt_ref, *refs, pps):
    del pt_ref
    k_refs, v_refs = refs[:pps], refs[pps:2 * pps]
    wk_ref, wv_ref, kc_ref, vc_ref = refs[2 * pps:]
    j = pl.program_id(1)
    kvw, page = k_refs[0].shape
    nblk = kc_ref.shape[1]
    bpp = page // NSA_BLOCK

    @pl.when(j == 0)
    def _():
        kc_ref[...] = jnp.zeros(kc_ref.shape, F32)
        vc_ref[...] = jnp.zeros(vc_ref.shape, F32)

    row_blk = lax.broadcasted_iota(jnp.int32, (page, nblk), 0) // NSA_BLOCK
    col = lax.broadcasted_iota(jnp.int32, (page, nblk), 1)
    for page_refs, w_ref, o_ref in ((k_refs, wk_ref, kc_ref), (v_refs, wv_ref, vc_ref)):
        acc = jnp.zeros((kvw, nblk), F32)
        for i, r in enumerate(page_refs):
            y = r[...] * w_ref[...]
            hi = y.astype(BF16)
            lo = (y - hi.astype(F32)).astype(BF16)
            onehot = (col == (j * pps + i) * bpp + row_blk).astype(BF16)
            acc = acc + _dot(jnp.concatenate([hi, lo], axis=1), jnp.concatenate([onehot, onehot], axis=0))
        o_ref[...] += acc


def _cmp_pages_call(page_table, pool_k, pool_v, wk_t, wv_t, *, layer, pps):
    nbs, npg = page_table.shape
    _, _, kvw, page = pool_k.shape
    nblk = npg * (page // NSA_BLOCK)
    assert npg % pps == 0
    wspec = pl.BlockSpec((kvw, page), lambda b, j, pt: (0, 0))
    grid_spec = pltpu.PrefetchScalarGridSpec(
        num_scalar_prefetch=1, grid=(nbs, npg // pps),
        in_specs=_page_specs(pps, kvw, page, layer) * 2 + [wspec] * 2,
        out_specs=[pl.BlockSpec((None, kvw, nblk), lambda b, j, pt: (b, 0, 0))] * 2)
    return pl.pallas_call(
        functools.partial(_cmp_pages_body, pps=pps), name="sample_cmp_pages",
        grid_spec=grid_spec,
        out_shape=[jax.ShapeDtypeStruct((nbs, kvw, nblk), F32)] * 2,
        compiler_params=_cp("parallel", "arbitrary"),
    )(page_table, *([pool_k] * pps), *([pool_v] * pps), wk_t, wv_t)


def _samp_cw_body(q_ref, g_ref, kct_ref, vct_ref, wk_ref, wv_ref, kn_ref, vn_ref, *refs,
                  ts, past, rep, scale, n_alias):
    ocw_ref, selb_ref, wko_ref, wvo_ref = refs[n_alias:]
    rows = q_ref.shape[0]
    kvw, nblk = kct_ref.shape
    wb = wk_ref.shape[1]
    npad = kn_ref.shape[1]
    gt = rows // rep
    q = q_ref[...]
    ti = lax.rem(lax.broadcasted_iota(jnp.int32, (rows, 1), 0), ts)
    t = past + ti
    blk_r = lax.broadcasted_iota(jnp.int32, (rows, nblk), 1)
    p_c = _msoftmax(_dot(q, kct_ref[...].astype(BF16)) * scale, (blk_r + 1) * NSA_BLOCK - 1 <= t)
    o_c = _dot_nt(p_c.astype(BF16), vct_ref[...].astype(BF16))
    imp = p_c[0:gt]
    for r in range(1, rep):
        imp = imp + p_c[r * gt:(r + 1) * gt]
    blk = lax.broadcasted_iota(jnp.int32, (gt, nblk), 1)
    sel = _select_blocks(imp, blk, nblk, min(NSA_TOPN - 1, nblk), axis=1)
    selb_ref[...] = jnp.concatenate([jnp.where(sel, 0.0, NEG)] * rep, axis=0)
    wk, wv, kn, vn = wk_ref[...], wv_ref[...], kn_ref[...], vn_ref[...]
    kp1 = past - wb + lax.broadcasted_iota(jnp.int32, (rows, wb), 1)
    j2 = lax.broadcasted_iota(jnp.int32, (rows, npad), 1)
    ok1 = (kp1 <= t) & (kp1 > t - NSA_WINDOW) & (kp1 >= 0)
    ok2 = (j2 <= ti) & (j2 < ts)
    s1 = jnp.where(ok1, _dot(q, wk.astype(BF16)) * scale, NEG)
    s2 = jnp.where(ok2, _dot(q, kn.astype(BF16)) * scale, NEG)
    m = jnp.maximum(jnp.max(s1, axis=1, keepdims=True), jnp.max(s2, axis=1, keepdims=True))
    e1 = jnp.where(ok1, jnp.exp(s1 - m), 0.0)
    e2 = jnp.where(ok2, jnp.exp(s2 - m), 0.0)
    den = jnp.maximum(jnp.sum(e1, axis=1, keepdims=True) + jnp.sum(e2, axis=1, keepdims=True), 1e-30)
    o_w = _dot_nt((e1 / den).astype(BF16), wv.astype(BF16)) + _dot_nt((e2 / den).astype(BF16), vn.astype(BF16))
    g = g_ref[...]
    ocw_ref[...] = g[:, 0:1] * o_c + g[:, 2:3] * o_w
    lane = lax.broadcasted_iota(jnp.int32, (kvw, wb), 1)

    def advance(old, new):
        tail = jnp.concatenate([jnp.zeros((kvw, wb - npad), F32), pltpu.roll(new, npad - ts, axis=1)], axis=1)
        return jnp.where(lane >= wb - ts, tail, pltpu.roll(old, wb - ts, axis=1))

    wko_ref[...] = advance(wk, kn)
    wvo_ref[...] = advance(wv, vn)


def _samp_cw_call(q_bd, gall, kct, vct, win_k, win_v, kn_t, vn_t, stacks, *, layer, depth, ts, past, rep, dh):
    nbs, rows, kvw = q_bd.shape
    nblk = kct.shape[2]
    wb = win_k.shape[3]
    npad = kn_t.shape[2]
    assert wb >= npad
    per_b = lambda *s: pl.BlockSpec((None,) + s, lambda b: (b,) + (0,) * len(s))
    win = pl.BlockSpec((None, None, kvw, wb), lambda b: (layer, b, 0, 0))
    n_alias = len(stacks)
    body = functools.partial(_samp_cw_body, ts=ts, past=past, rep=rep, scale=dh ** -0.5, n_alias=n_alias)
    return pl.pallas_call(
        body, name="sample_cmp_win",
        grid=(nbs,),
        in_specs=[per_b(rows, kvw), per_b(rows, LANES), per_b(kvw, nblk), per_b(kvw, nblk), win, win,
                  per_b(kvw, npad), per_b(kvw, npad)] + [pl.BlockSpec(memory_space=pl.ANY)] * n_alias,
        out_specs=[per_b(rows, kvw), per_b(rows, nblk), win, win],
        out_shape=[jax.ShapeDtypeStruct((nbs, rows, kvw), F32), jax.ShapeDtypeStruct((nbs, rows, nblk), F32),
                   jax.ShapeDtypeStruct((depth, nbs, kvw, wb), F32), jax.ShapeDtypeStruct((depth, nbs, kvw, wb), F32)],
        input_output_aliases={8 + k: 2 + k for k in range(n_alias)},
        compiler_params=_cp("parallel"),
    )(q_bd, gall, kct, vct, win_k, win_v, kn_t, vn_t, *stacks)


def _samp_sel_body(pt_ref, *refs, pps, ts, scale):
    del pt_ref
    k_refs, v_refs = refs[:pps], refs[pps:2 * pps]
    q_ref, selb_ref, kn_ref, vn_ref, ocw_ref, g_ref, o_ref, m_sc, l_sc, acc_sc = refs[2 * pps:]
    j = pl.program_id(1)
    rows = q_ref.shape[0]
    nblk = selb_ref.shape[1]
    page = k_refs[0].shape[1]
    kt = pps * page
    q = q_ref[...]

    @pl.when(j == 0)
    def _():
        m_sc[...] = jnp.full(m_sc.shape, M_INIT, F32)
        l_sc[...] = jnp.zeros(l_sc.shape, F32)
        acc_sc[...] = jnp.zeros(acc_sc.shape, F32)

    def update(s, v_t):
        m_old = m_sc[...]
        m_new = jnp.maximum(m_old, jnp.max(s, axis=1, keepdims=True))
        alpha = jnp.exp(m_old - m_new)
        p = jnp.exp(s - m_new)
        l_sc[...] = alpha * l_sc[...] + jnp.sum(p, axis=1, keepdims=True)
        acc_sc[...] = alpha * acc_sc[...] + _dot_nt(p.astype(BF16), v_t)
        m_sc[...] = m_new

    k_t = jnp.concatenate([r[...] for r in k_refs], axis=1).astype(BF16)
    v_t = jnp.concatenate([r[...] for r in v_refs], axis=1).astype(BF16)
    key_blk = j * (kt // NSA_BLOCK) + lax.broadcasted_iota(jnp.int32, (nblk, kt), 1) // NSA_BLOCK
    onehot = (key_blk == lax.broadcasted_iota(jnp.int32, (nblk, kt), 0)).astype(BF16)
    update(_dot(q, k_t) * scale + _dot(selb_ref[...].astype(BF16), onehot), v_t)

    @pl.when(j == pl.num_programs(1) - 1)
    def _():
        npad = kn_ref.shape[1]
        ti = lax.rem(lax.broadcasted_iota(jnp.int32, (rows, npad), 0), ts)
        j2 = lax.broadcasted_iota(jnp.int32, (rows, npad), 1)
        s = jnp.where((j2 <= ti) & (j2 < ts), _dot(q, kn_ref[...].astype(BF16)) * scale, NEG)
        update(s, vn_ref[...].astype(BF16))
        o_ref[...] = ocw_ref[...] + g_ref[:, 1:2] * (acc_sc[...] / l_sc[...])


def _samp_sel_call(page_table, pool_k, pool_v, q_bd, selb, kn_t, vn_t, ocw, gall, *, layer, pps, ts, dh):
    nbs, npg = page_table.shape
    _, _, kvw, page = pool_k.shape
    rows = q_bd.shape[1]
    per_b = lambda *s: pl.BlockSpec((None,) + s, lambda b, j, pt: (b,) + (0,) * len(s))
    npad = kn_t.shape[2]
    grid_spec = pltpu.PrefetchScalarGridSpec(
        num_scalar_prefetch=1, grid=(nbs, npg // pps),
        in_specs=_page_specs(pps, kvw, page, layer) * 2 + [
            per_b(rows, kvw), per_b(rows, selb.shape[2]), per_b(kvw, npad), per_b(kvw, npad),
            per_b(rows, kvw), per_b(rows, LANES)],
        out_specs=per_b(rows, kvw),
        scratch_shapes=[pltpu.VMEM((rows, 1), F32), pltpu.VMEM((rows, 1), F32), pltpu.VMEM((rows, kvw), F32)])
    body = functools.partial(_samp_sel_body, pps=pps, ts=ts, scale=dh ** -0.5)
    return pl.pallas_call(
        body, name="sample_selected",
        grid_spec=grid_spec,
        out_shape=jax.ShapeDtypeStruct((nbs, rows, kvw), F32),
        compiler_params=_cp("parallel", "arbitrary"),
    )(page_table, *([pool_k] * pps), *([pool_v] * pps), q_bd, selb, kn_t, vn_t, ocw, gall)


def _merge_body(x_ref, g_ref, wmg_ref, oa_ref, ob_ref, oc_ref, wb_ref, wo_ref, o_ref):
    x = x_ref[...]
    dm = x.shape[1]
    xn = _rms(x, g_ref[...]).astype(BF16)
    mg = jax.nn.sigmoid(_dot(xn, wmg_ref[...]))
    y = mg[:, 0:dm] * _dot(oa_ref[...], wb_ref[0])
    y = y + mg[:, dm:2 * dm] * _dot(ob_ref[...], wb_ref[1])
    y = y + mg[:, 2 * dm:3 * dm] * _dot(oc_ref[...], wb_ref[2])
    o_ref[...] = x + _dot(y.astype(BF16), wo_ref[...])


def _merge_call(x, g, wmg, oa, ob, oc, wb, wo, *, tm):
    t, dm = x.shape
    row = pl.BlockSpec((tm, dm), lambda i: (i, 0))
    return pl.pallas_call(
        _merge_body, name="merge",
        grid=(t // tm,),
        in_specs=[row, _full(g.shape), _full(wmg.shape), row, row, row, _full(wb.shape), _full(wo.shape)],
        out_specs=row,
        out_shape=jax.ShapeDtypeStruct((t, dm), F32),
        compiler_params=_cp("parallel"),
    )(x, g, wmg, oa, ob, oc, wb, wo)


def _ffn_body(x_ref, g_ref, w1_ref, w3_ref, w2_ref, fg_ref, o_ref, xn_sc, acc_sc, *, final):
    f = pl.program_id(1)

    @pl.when(f == 0)
    def _():
        xn_sc[...] = _rms(x_ref[...], g_ref[...]).astype(BF16)
        acc_sc[...] = jnp.zeros(acc_sc.shape, F32)

    xn = xn_sc[...]
    h = jax.nn.silu(_dot(xn, w1_ref[...])) * _dot(xn, w3_ref[...])
    acc_sc[...] += _dot(h.astype(BF16), w2_ref[...])

    @pl.when(f == pl.num_programs(1) - 1)
    def _():
        y = x_ref[...] + acc_sc[...]
        o_ref[...] = _rms(y, fg_ref[...]) if final else y


def _ffn_tile(f_dim):
    half = f_dim // 2
    return half if f_dim % 2 == 0 and half % LANES == 0 else f_dim


def _ffn_call(x, g, w1, w3, w2, fg, *, tm, final):
    t, dm = x.shape
    fd = w1.shape[1]
    ft = _ffn_tile(fd)
    row = pl.BlockSpec((tm, dm), lambda i, f: (i, 0))
    vec = pl.BlockSpec((1, dm), lambda i, f: (0, 0))
    return pl.pallas_call(
        functools.partial(_ffn_body, final=final), name="ffn",
        grid=(t // tm, fd // ft),
        in_specs=[row, vec, pl.BlockSpec((dm, ft), lambda i, f: (0, f)), pl.BlockSpec((dm, ft), lambda i, f: (0, f)),
                  pl.BlockSpec((ft, dm), lambda i, f: (f, 0)), vec],
        out_specs=row,
        out_shape=jax.ShapeDtypeStruct((t, dm), F32),
        scratch_shapes=[pltpu.VMEM((tm, dm), BF16), pltpu.VMEM((tm, dm), F32)],
        compiler_params=_cp("parallel", "arbitrary"),
    )(x, g, w1, w3, w2, fg)


def _route_body(x_ref, g_ref, wr_ref, br_ref, xn_ref, comb_ref, key_ref, keyt_ref, cnt_ref):
    tm = x_ref.shape[0]
    xn = _rms(x_ref[...], g_ref[...])
    xn_ref[...] = xn.astype(BF16)
    logits = jnp.dot(xn, wr_ref[...], preferred_element_type=F32, precision=lax.Precision.HIGHEST) + br_ref[...]
    lane = lax.broadcasted_iota(jnp.int32, logits.shape, 1)
    nl = logits.shape[1]
    m1 = jnp.max(logits, axis=1, keepdims=True)
    i1 = jnp.min(jnp.where(logits == m1, lane, nl), axis=1, keepdims=True)
    rest = jnp.where(lane == i1, M_INIT, logits)
    m2 = jnp.max(rest, axis=1, keepdims=True)
    i2 = jnp.min(jnp.where(rest == m2, lane, nl), axis=1, keepdims=True)
    e2 = jnp.exp(m2 - m1)
    den = 1.0 + e2
    comb_ref[...] = jnp.where(lane == i1, 1.0 / den, 0.0) + jnp.where(lane == i2, e2 / den, 0.0)
    member = (lane == i1) | (lane == i2)
    earlier = lax.broadcasted_iota(jnp.int32, (tm, tm), 1) < lax.broadcasted_iota(jnp.int32, (tm, tm), 0)
    rank = _dot(earlier.astype(BF16), member.astype(BF16))
    key = jnp.where(member, rank, -1.0)
    key_ref[...] = key
    keyt_ref[...] = key.T
    count = jnp.sum(member.astype(F32), axis=0, keepdims=True).astype(jnp.int32)
    cnt_ref[...] = jnp.broadcast_to(count, cnt_ref.shape)


def _route_call(x, g, wr, br, *, tm):
    t, dm = x.shape
    nl = wr.shape[1]
    nt = t // tm
    row = lambda n: pl.BlockSpec((tm, n), lambda i: (i, 0))
    return pl.pallas_call(
        _route_body, name="moe_route",
        grid=(nt,),
        in_specs=[row(dm), _full(g.shape), _full(wr.shape), _full(br.shape)],
        out_specs=[row(dm), row(nl), row(nl), pl.BlockSpec((None, nl, tm), lambda i: (i, 0, 0)),
                   pl.BlockSpec((None, 8, nl), lambda i: (i, 0, 0))],
        out_shape=[jax.ShapeDtypeStruct((t, dm), BF16), jax.ShapeDtypeStruct((t, nl), F32),
                   jax.ShapeDtypeStruct((t, nl), F32), jax.ShapeDtypeStruct((nt, nl, tm), F32),
                   jax.ShapeDtypeStruct((nt, 8, nl), jnp.int32)],
        compiler_params=_cp("parallel"),
    )(x, g, wr, br)


def _moe_body(cnt_ref, x_ref, xn_ref, comb_ref, key_ref, keyt_ref, w1_ref, w3_ref, w2_ref, fg_ref, o_ref,
              xe_sc, ye_sc, acc_sc, *, cap, final):
    i = pl.program_id(0)
    e = pl.program_id(1)
    f = pl.program_id(2)
    tm = x_ref.shape[0]
    n_batches = (cnt_ref[i, e] + cap - 1) // cap
    rows_of = lambda b: pl.ds(pl.multiple_of(b * cap, cap), cap)

    @pl.when((e == 0) & (f == 0))
    def _():
        acc_sc[...] = jnp.zeros(acc_sc.shape, F32)

    @pl.when(f == 0)
    def _():
        key_row = keyt_ref[pl.ds(e, 1), :]

        def gather(b, carry):
            slot = (b * cap + lax.broadcasted_iota(jnp.int32, (cap, tm), 0)).astype(F32)
            xe_sc[rows_of(b), :] = _dot((key_row == slot).astype(BF16), xn_ref[...]).astype(BF16)
            return carry

        lax.fori_loop(0, n_batches, gather, 0)

    def expert(b, carry):
        xe = xe_sc[rows_of(b), :]
        h = jax.nn.silu(_dot(xe, w1_ref[...])) * _dot(xe, w3_ref[...])
        part = _dot(h.astype(BF16), w2_ref[...])

        @pl.when(f == 0)
        def _():
            ye_sc[rows_of(b), :] = part

        @pl.when(f != 0)
        def _():
            ye_sc[rows_of(b), :] += part

        return carry

    lax.fori_loop(0, n_batches, expert, 0)

    @pl.when(f == pl.num_programs(2) - 1)
    def _():
        lane = lax.broadcasted_iota(jnp.int32, key_ref.shape, 1)
        key_col = jnp.sum(jnp.where(lane == e, key_ref[...], 0.0), axis=1, keepdims=True)
        gate = jnp.sum(jnp.where(lane == e, comb_ref[...], 0.0), axis=1, keepdims=True)

        def scatter(b, carry):
            slot = (b * cap + lax.broadcasted_iota(jnp.int32, (tm, cap), 1)).astype(F32)
            onehot = (key_col == slot).astype(BF16)
            ye = ye_sc[rows_of(b), :]
            hi = ye.astype(BF16)
            lo = (ye - hi.astype(F32)).astype(BF16)
            back = _dot(jnp.concatenate([onehot, onehot], axis=1), jnp.concatenate([hi, lo], axis=0))
            acc_sc[...] += gate * back
            return carry

        lax.fori_loop(0, n_batches, scatter, 0)

    @pl.when((e == pl.num_programs(1) - 1) & (f == pl.num_programs(2) - 1))
    def _():
        y = x_ref[...] + acc_sc[...]
        o_ref[...] = _rms(y, fg_ref[...]) if final else y


def _moe_call(x, g, wr, br, w1, w3, w2, fg, *, tm, final):
    t, dm = x.shape
    ne, _, fd = w1.shape
    nl = wr.shape[1]
    ft = _ffn_tile(fd)
    cap = min(MOE_BATCH, tm)
    assert cap % BF16_ROWS == 0
    max_rows = -(-tm // cap) * cap
    xn, comb, key, key_t, counts = _route_call(x, g, wr, br, tm=tm)
    once = dict(pipeline_mode=pl.Buffered(1))
    row = lambda n: pl.BlockSpec((tm, n), lambda i, e, f, cnt: (i, 0), **once)
    grid_spec = pltpu.PrefetchScalarGridSpec(
        num_scalar_prefetch=1, grid=(t // tm, ne, fd // ft),
        in_specs=[row(dm), row(dm), row(nl), row(nl),
                  pl.BlockSpec((None, nl, tm), lambda i, e, f, cnt: (i, 0, 0), **once),
                  pl.BlockSpec((None, dm, ft), lambda i, e, f, cnt: (e, 0, f)),
                  pl.BlockSpec((None, dm, ft), lambda i, e, f, cnt: (e, 0, f)),
                  pl.BlockSpec((None, ft, dm), lambda i, e, f, cnt: (e, f, 0)),
                  pl.BlockSpec((1, dm), lambda i, e, f, cnt: (0, 0))],
        out_specs=pl.BlockSpec((tm, dm), lambda i, e, f, cnt: (i, 0)),
        scratch_shapes=[pltpu.VMEM((max_rows, dm), BF16), pltpu.VMEM((max_rows, dm), F32), pltpu.VMEM((tm, dm), F32)])
    return pl.pallas_call(
        functools.partial(_moe_body, cap=cap, final=final), name="moe",
        grid_spec=grid_spec,
        out_shape=jax.ShapeDtypeStruct((t, dm), F32),
        compiler_params=_cp("parallel", "arbitrary", "arbitrary"),
    )(counts[:, 0, :ne], x, xn, comb, key, key_t, w1, w3, w2, fg)


def kernel(x_prompt, x_sample, cache_cmp_k, cache_cmp_v, cache_slc_k, cache_slc_v, cache_win_k, cache_win_v, state_conv, state_lru, page_table, norm_mix_g, w_in, lru_conv_w, lru_conv_b, lru_wa, lru_ba, lru_wx, lru_bx, lru_lambda, nsa_cmp_wk, nsa_cmp_wv, gm_norm_g, gm_ws, gm_bs, w_branch, w_out, norm_ffn_g, ffn_w1, ffn_w3, ffn_w2, moe_wr, moe_br, moe_w1, moe_w3, moe_w2, final_norm_g):
    depth, dm, _ = w_in.shape
    nbp, seq, _ = x_prompt.shape
    nbs, ts, _ = x_sample.shape
    _, n_pool, page, ng, dh = cache_cmp_k.shape
    wd = state_lru.shape[-1]
    conv_w = lru_conv_w.shape[1]
    gmw = gm_norm_g.shape[-1]
    n_gm, chunk = gm_ws.shape[1], gm_ws.shape[2]
    kvw = ng * dh
    qw = NSA_HEADS * dh
    rep = NSA_HEADS // ng
    npg = page_table.shape[1]
    past = npg * page
    wbuf = cache_win_k.shape[2]
    tp, tsm = nbp * seq, nbs * ts
    ne = moe_wr.shape[-1]
    nblk = seq // NSA_BLOCK
    scale = dh ** -0.5
    s_scale = 1.0 if _is_power_of_two(scale) else scale

    assert seq % chunk == 0 and seq % LRU_CHUNK == 0 and seq >= NSA_WINDOW
    assert past % NSA_BLOCK == 0 and ts < NSA_BLOCK and ts < chunk and wbuf == NSA_WINDOW and nbs % 8 == 0
    assert past // NSA_BLOCK >= NSA_TOPN - 1 and page % LANES == 0

    offs = [0, wd, 2 * wd, 2 * wd + qw, 2 * wd + qw + 6 * kvw, 2 * wd + qw + 6 * kvw + 3 * NSA_HEADS]
    offs += [offs[-1] + 2 * gmw, offs[-1] + 2 * gmw + 3 * dm]
    w_part = lambda lo, hi: w_in[:, :, lo:hi].astype(BF16)
    w_lru = w_part(0, offs[2])
    w_nsa = jnp.pad(w_part(offs[2], offs[5]), ((0, 0), (0, 0), (0, LANES - 3 * NSA_HEADS)))
    w_uv = w_part(offs[5], offs[6])
    w_mg = w_part(offs[6], offs[7])
    wa_b, wx_b = lru_wa.astype(BF16), lru_wx.astype(BF16)
    wb_b, wo_b = w_branch.astype(BF16), w_out.astype(BF16)
    f1_b, f3_b, f2_b = ffn_w1.astype(BF16), ffn_w3.astype(BF16), ffn_w2.astype(BF16)
    m1_b, m3_b, m2_b = moe_w1.astype(BF16), moe_w3.astype(BF16), moe_w2.astype(BF16)
    wr_pad = jnp.pad(moe_wr, ((0, 0), (0, 0), (0, LANES - ne)))
    br_pad = jnp.pad(moe_br, ((0, 0), (0, LANES - ne)), constant_values=NEG)[:, None, :]
    wck = nsa_cmp_wk.reshape(depth, NSA_BLOCK, kvw)
    wcv = nsa_cmp_wv.reshape(depth, NSA_BLOCK, kvw)
    cmp_w_t = lambda w: jnp.tile(jnp.transpose(w, (0, 2, 3, 1)).reshape(depth, kvw, NSA_BLOCK), (1, 1, page // NSA_BLOCK))
    wck_t, wcv_t = cmp_w_t(nsa_cmp_wk), cmp_w_t(nsa_cmp_wv)
    gc = gmw // n_gm
    bs_full = jnp.repeat(jnp.swapaxes(gm_bs, 1, 2), gc, axis=2)
    ws_short = jnp.repeat(jnp.transpose(gm_ws[:, :, :ts, :ts], (0, 2, 3, 1)).reshape(depth, ts * ts, n_gm), gc, axis=2)
    vec = lambda a, l: a[l][None, :]
    pps = min(PAGES_PER_STEP, npg)
    feature_major = lambda a: jnp.transpose(a, (0, 1, 3, 4, 2)).reshape(a.shape[0], a.shape[1], kvw, a.shape[2])
    pool_ck, pool_cv = feature_major(cache_cmp_k), feature_major(cache_cmp_v)
    pool_sk, pool_sv = feature_major(cache_slc_k), feature_major(cache_slc_v)
    win_k, win_v = feature_major(cache_win_k), feature_major(cache_win_v)
    eye_g = jnp.eye(ng, dtype=BF16)

    tm_p = min(ROW_TILE, tp)
    xp = x_prompt.reshape(tp, dm)
    xs = x_sample.reshape(tsm, dm)
    zeros_cs = jnp.zeros((nbp, conv_w - 1, wd), F32)
    zeros_h = jnp.zeros((nbp, 1, wd), F32)
    to_tm = lambda a, n: jnp.swapaxes(a.reshape(nbs, n, -1), 0, 1).reshape(1, n * nbs, -1)
    from_tm = lambda a, n: jnp.swapaxes(a.reshape(n, nbs, -1), 0, 1)
    new_rows_t = lambda a: jnp.pad(jnp.swapaxes(a.reshape(nbs, ts, kvw), 1, 2), ((0, 0), (0, 0), (0, LANES - ts)))

    p_stacks, w_stacks = (), ()
    p_small = [[] for _ in range(2)]
    s_out = [[] for _ in range(7)]
    for l in range(depth):
        last = l == depth - 1
        gmix = vec(norm_mix_g, l)
        lru_w = (lru_conv_w[l], vec(lru_conv_b, l), wa_b[l], vec(lru_ba, l), wx_b[l], vec(lru_bx, l), vec(lru_lambda, l))

        o_a, conv_new, h_last = _lru_call(xp.reshape(nbp, seq, dm), gmix, w_lru[l], zeros_cs, zeros_h, *lru_w,
                                          nb=1, tch=min(LRU_CHUNK, seq), starts_at_zero=True)
        o_c = _gmlp_call(xp, gmix, w_uv[l], vec(gm_norm_g, l), gm_ws[l], bs_full[l], tm=tm_p)
        qt, gates_t, kcmp, vcmp, ka, vst, kwg, vwt, *p_stacks = _qkv_prompt_call(
            xp, gmix, w_nsa[l], wck[l], wcv[l], p_stacks, layer=l, depth=depth, nbatch=nbp, seq=seq, tm=tm_p,
            qw=qw, kvw=kvw, dh=dh)
        vcmp_t = jnp.swapaxes(vcmp.reshape(nbp, nblk, kvw), 1, 2)
        ocw_t, selb = _nsa_cw_call(qt, gates_t, kcmp, vcmp_t, kwg, vwt, nbatch=nbp, seq=seq, ng=ng, dh=dh,
                                   s_scale=s_scale)
        o_b = _nsa_sel_call(qt, selb, ka, vst, ocw_t, gates_t, nbatch=nbp, seq=seq, ng=ng, dh=dh, s_scale=s_scale)
        xp = _merge_call(xp, gmix, w_mg[l], o_a.reshape(tp, wd), o_b, o_c, wb_b[l], wo_b[l], tm=tm_p)
        if l % 2 == 0:
            xp = _ffn_call(xp, vec(norm_ffn_g, l), f1_b[l // 2], f3_b[l // 2], f2_b[l // 2], final_norm_g[None, :],
                           tm=tm_p, final=last)
        else:
            xp = _moe_call(xp, vec(norm_ffn_g, l), wr_pad[l // 2], br_pad[l // 2], m1_b[l // 2], m3_b[l // 2],
                           m2_b[l // 2], final_norm_g[None, :], tm=min(MOE_TILE, tp), final=last)
        p_small[0].append(conv_new)
        p_small[1].append(h_last.reshape(nbp, wd))

        xs_tm = to_tm(xs, ts)
        o_a, conv_new, h_last = _lru_call(xs_tm, gmix, w_lru[l], to_tm(state_conv[l], conv_w - 1),
                                          state_lru[l][None], *lru_w, nb=nbs, tch=ts, starts_at_zero=False)
        o_a = from_tm(o_a, ts).reshape(tsm, wd)
        o_c, gm_v = _gmlp_short_call(xs_tm[0], gmix, w_uv[l], vec(gm_norm_g, l), ws_short[l], bs_full[l][:ts],
                                     n=ts, nb=nbs)
        o_c = from_tm(o_c, ts).reshape(tsm, gmw)
        q, k_c, v_c, k_s, v_s, k_w, v_w, gates = _qkv_sample_call(xs, gmix, w_nsa[l], qw=qw, kvw=kvw, dh=dh)
        q5 = jnp.transpose(q.reshape(nbs, ts, ng, rep, dh), (0, 3, 2, 1, 4))
        q_bd = (q5[:, :, :, :, None, :] * eye_g[None, None, :, None, :, None]).reshape(nbs, rep * ng * ts, kvw)
        g5 = jnp.transpose(gates[:, :3 * NSA_HEADS].reshape(nbs, ts, ng, rep, 3), (0, 3, 2, 1, 4))
        gall = jnp.pad(g5.reshape(nbs, rep * ng * ts, 3), ((0, 0), (0, 0), (0, LANES - 3)))
        kct, vct = _cmp_pages_call(page_table, pool_ck, pool_cv, wck_t[l], wcv_t[l], layer=l, pps=pps)
        ocw, selb, *w_stacks = _samp_cw_call(
            q_bd, gall, kct, vct, win_k, win_v, new_rows_t(k_w), new_rows_t(v_w), w_stacks,
            layer=l, depth=depth, ts=ts, past=past, rep=rep, dh=dh)
        o_all = _samp_sel_call(page_table, pool_sk, pool_sv, q_bd, selb, new_rows_t(k_s), new_rows_t(v_s),
                               ocw, gall, layer=l, pps=pps, ts=ts, dh=dh)
        o6 = o_all.reshape(nbs, rep, ng, ts, ng, dh)
        o_b = jnp.stack([o6[:, :, gi, :, gi, :] for gi in range(ng)], axis=2)
        o_b = jnp.transpose(o_b, (0, 3, 2, 1, 4)).reshape(tsm, qw).astype(BF16)
        xs = _merge_call(xs, gmix, w_mg[l], o_a, o_b, o_c, wb_b[l], wo_b[l], tm=tsm)
        if l % 2 == 0:
            xs = _ffn_call(xs, vec(norm_ffn_g, l), f1_b[l // 2], f3_b[l // 2], f2_b[l // 2], final_norm_g[None, :],
                           tm=tsm, final=last)
        else:
            xs = _moe_call(xs, vec(norm_ffn_g, l), wr_pad[l // 2], br_pad[l // 2], m1_b[l // 2], m3_b[l // 2],
                           m2_b[l // 2], final_norm_g[None, :], tm=tsm, final=last)
        shs = (nbs, ts, ng, dh)
        for lst, a in zip(s_out, (k_c.reshape(shs), v_c.reshape(shs), k_s.reshape(shs), v_s.reshape(shs),
                                  from_tm(conv_new[0], conv_w - 1), h_last[0], from_tm(gm_v, ts))):
            lst.append(a)

    row_major = lambda a: jnp.transpose(a.reshape(a.shape[0], a.shape[1], ng, dh, a.shape[3]), (0, 1, 4, 2, 3))
    keep = min(NSA_WINDOW, seq)
    kc_st, vc_st, ks_st, vs_st, kw_st, vw_st = p_stacks
    s_small = [jnp.stack(a) for a in s_out]
    return (xp.reshape(nbp, seq, dm), xs.reshape(nbs, ts, dm),
            row_major(kc_st), row_major(vc_st), row_major(ks_st), row_major(vs_st),
            row_major(kw_st[..., seq - keep:]), row_major(vw_st[..., seq - keep:]),
            jnp.stack(p_small[0]), jnp.stack(p_small[1]),
            s_small[0], s_small[1], s_small[2], s_small[3],
            row_major(w_stacks[0]), row_major(w_stacks[1]),
            s_small[4], s_small[5], s_small[6])
```

```python
import functools
import math

import jax
import jax.numpy as jnp
from jax import lax
from jax.experimental import pallas as pl
from jax.experimental.pallas import tpu as pltpu

F32 = jnp.float32
BF16 = jnp.bfloat16

NSA_HEADS = 16
NSA_BLOCK = 64
NSA_TOPN = 8
NSA_WINDOW = 512
SEL_FORCE = 1.0e4
LRU_C = 8.0
TOP_K = 2
RMS_EPS = 1e-6
NEG = -1e30
M_INIT = -3.0e38

Q_BLOCK = 256
KV_TILE = 1024
LRU_CHUNK = 256
ROW_TILE = 512
MOE_TILE = 1024
MOE_BATCH = 288
PAGES_PER_STEP = 32
BF16_ROWS = 16
LANES = 128
VMEM_LIMIT = 56 * 2 ** 20


def _cp(*sem):
    return pltpu.CompilerParams(dimension_semantics=sem, vmem_limit_bytes=VMEM_LIMIT)


def _dot(a, b):
    return jnp.dot(a, b, preferred_element_type=F32)


def _dot_nt(a, b):
    return lax.dot_general(a, b, (((1,), (1,)), ((), ())), preferred_element_type=F32)


def _rms(x, g):
    return x * lax.rsqrt(jnp.mean(x * x, axis=-1, keepdims=True) + RMS_EPS) * g


def _msoftmax(s, mask, axis=-1):
    s = jnp.where(mask, s, NEG)
    e = jnp.where(mask, jnp.exp(s - jnp.max(s, axis=axis, keepdims=True)), 0.0)
    return e / jnp.maximum(jnp.sum(e, axis=axis, keepdims=True), 1e-30)


def _is_power_of_two(x):
    return math.frexp(x)[0] == 0.5


def _full(shape):
    n = len(shape)
    return pl.BlockSpec(shape, lambda *_: (0,) * n)


def _at_layer(stacked, layer):
    n = stacked.ndim - 1
    return pl.BlockSpec((None,) + stacked.shape[1:], lambda *_: (layer,) + (0,) * n)


def _lru_body(x_ref, g_ref, w_ref, cs_ref, h0_ref, cw_ref, cb_ref, wa_ref, ba_ref, wx_ref, bx_ref, lam_ref,
              oa_ref, cn_ref, hl_ref, xbuf, hc, *, nb, tch, conv_w, starts_at_zero):
    c = pl.program_id(1)
    rows = nb * tch
    wd = hc.shape[-1]
    nst = (conv_w - 1) * nb
    cr = xbuf.shape[0] - rows
    nh = wa_ref.shape[0]
    bw = wd // nh

    @pl.when(c == 0)
    def _():
        if cr > nst:
            xbuf[0:cr - nst, :] = jnp.zeros((cr - nst, wd), F32)
        xbuf[cr - nst:cr, :] = cs_ref[...]
        hc[...] = h0_ref[...]

    xn = _rms(x_ref[...], g_ref[...]).astype(BF16)
    z = _dot(xn, w_ref[...])
    gate = z[:, wd:]
    xbuf[cr:cr + rows, :] = z[:, :wd]
    y = cb_ref[...] + xbuf[cr - nst:cr - nst + rows, :] * cw_ref[0:1, :]
    for k in range(1, conv_w):
        st = cr - nst + k * nb
        y = y + xbuf[st:st + rows, :] * cw_ref[k:k + 1, :]
    cn_ref[...] = xbuf[cr + rows - nst:cr + rows, :]
    xbuf[0:cr, :] = xbuf[rows:rows + cr, :]

    yb = y.astype(BF16)

    def block_diag(wr):
        return jnp.concatenate([_dot(yb[:, j * bw:(j + 1) * bw], wr[j]) for j in range(nh)], axis=1)

    r = jax.nn.sigmoid(block_diag(wa_ref) + ba_ref[...])
    i = jax.nn.sigmoid(block_diag(wx_ref) + bx_ref[...])
    lam = lam_ref[...]
    softplus_neg_lam = jnp.maximum(-lam, 0.0) + jnp.log1p(jnp.exp(-jnp.abs(lam)))
    a = jnp.exp((-LRU_C) * r * softplus_neg_lam)
    mult = jnp.sqrt(1.0 - a * a)
    row = lax.broadcasted_iota(jnp.int32, (rows, wd), 0)
    if starts_at_zero:
        mult = jnp.where((row < nb) & (c == 0), 1.0, mult)
    u = mult * (i * y)
    if nb == 1:
        u = u + jnp.where(row < 1, a * hc[...], 0.0)
    else:
        u = jnp.concatenate([u[:nb] + a[:nb] * hc[...], u[nb:]], axis=0)
    d = nb
    while d < rows:
        a_sh = jnp.where(row < d, 1.0, pltpu.roll(a, d, axis=0))
        u_sh = jnp.where(row < d, 0.0, pltpu.roll(u, d, axis=0))
        u = a * u_sh + u
        a = a * a_sh
        d *= 2
    hc[...] = u[rows - nb:rows, :]
    hl_ref[...] = u[rows - nb:rows, :]
    oa_ref[...] = (jax.nn.gelu(gate) * u).astype(oa_ref.dtype)


def _lru_call(x, g, w, cs, h0, cw, cb, wa, ba, wx, bx, lam, *, layer, nb, tch, starts_at_zero):
    nbatch, rows_total, dm = x.shape
    wd = h0.shape[-1]
    conv_w = cw.shape[0]
    nst = (conv_w - 1) * nb
    rows = nb * tch
    cr = -(-nst // 8) * 8
    assert rows_total % rows == 0 and rows >= cr
    body = functools.partial(_lru_body, nb=nb, tch=tch, conv_w=conv_w, starts_at_zero=starts_at_zero)
    return pl.pallas_call(
        body, name="lru",
        grid=(nbatch, rows_total // rows),
        in_specs=[pl.BlockSpec((None, rows, dm), lambda b, c: (b, c, 0)),
                  _full(g.shape), _at_layer(w, layer),
                  pl.BlockSpec((None, nst, wd), lambda b, c: (b, 0, 0)),
                  pl.BlockSpec((None, nb, wd), lambda b, c: (b, 0, 0)),
                  _full(cw.shape), _full(cb.shape), _at_layer(wa, layer), _full(ba.shape), _at_layer(wx, layer),
                  _full(bx.shape), _full(lam.shape)],
        out_specs=[pl.BlockSpec((None, rows, wd), lambda b, c: (b, c, 0)),
                   pl.BlockSpec((None, nst, wd), lambda b, c: (b, 0, 0)),
                   pl.BlockSpec((None, nb, wd), lambda b, c: (b, 0, 0))],
        out_shape=[jax.ShapeDtypeStruct((nbatch, rows_total, wd), BF16),
                   jax.ShapeDtypeStruct((nbatch, nst, wd), F32),
                   jax.ShapeDtypeStruct((nbatch, nb, wd), F32)],
        scratch_shapes=[pltpu.VMEM((cr + rows, wd), F32), pltpu.VMEM((nb, wd), F32)],
        compiler_params=_cp("arbitrary", "arbitrary"),
    )(x, g, w, cs, h0, cw, cb, wa, ba, wx, bx, lam)


def _gmlp_body(x_ref, g_ref, w_ref, gg_ref, ws_ref, bs_ref, o_ref, *, chunk):
    tm = x_ref.shape[0]
    xn = _rms(x_ref[...], g_ref[...]).astype(BF16)
    ge = jax.nn.gelu(_dot(xn, w_ref[...]))
    wd = ge.shape[1] // 2
    u = ge[:, :wd]
    vb = _rms(ge[:, wd:], gg_ref[...]).astype(BF16)
    ng = ws_ref.shape[0]
    gc = wd // ng
    tril = lax.broadcasted_iota(jnp.int32, (chunk, chunk), 0) >= lax.broadcasted_iota(jnp.int32, (chunk, chunk), 1)
    wsm = [jnp.where(tril, ws_ref[gi], 0.0).astype(BF16) for gi in range(ng)]
    for ch in range(tm // chunk):
        lo, hi = ch * chunk, (ch + 1) * chunk
        mixed = jnp.concatenate([_dot(wsm[gi], vb[lo:hi, gi * gc:(gi + 1) * gc]) for gi in range(ng)], axis=1)
        o_ref[lo:hi, :] = (u[lo:hi] * (mixed + bs_ref[...])).astype(o_ref.dtype)


def _gmlp_call(x, g, w, gg, ws, bs_full, *, layer, tm):
    t, dm = x.shape
    wd = gg.shape[-1]
    chunk = ws.shape[-1]
    assert t % tm == 0 and tm % chunk == 0
    return pl.pallas_call(
        functools.partial(_gmlp_body, chunk=chunk), name="gmlp",
        grid=(t // tm,),
        in_specs=[pl.BlockSpec((tm, dm), lambda i: (i, 0)), _full(g.shape), _at_layer(w, layer), _full(gg.shape),
                  _at_layer(ws, layer), _full(bs_full.shape)],
        out_specs=pl.BlockSpec((tm, wd), lambda i: (i, 0)),
        out_shape=jax.ShapeDtypeStruct((t, wd), BF16),
        compiler_params=_cp("parallel"),
    )(x, g, w, gg, ws, bs_full)


def _gmlp_short_body(x_ref, g_ref, w_ref, gg_ref, wexp_ref, bexp_ref, o_ref, v_ref, *, n, nb):
    xn = _rms(x_ref[...], g_ref[...]).astype(BF16)
    ge = jax.nn.gelu(_dot(xn, w_ref[...]))
    wd = ge.shape[1] // 2
    u = ge[:, :wd]
    v = _rms(ge[:, wd:], gg_ref[...])
    v_ref[...] = v
    for t in range(n):
        m = bexp_ref[t:t + 1, :] + wexp_ref[t * n:t * n + 1, :] * v[0:nb]
        for s in range(1, t + 1):
            m = m + wexp_ref[t * n + s:t * n + s + 1, :] * v[s * nb:(s + 1) * nb]
        o_ref[t * nb:(t + 1) * nb, :] = (u[t * nb:(t + 1) * nb] * m).astype(o_ref.dtype)


def _gmlp_short_call(x, g, w, gg, wexp, bexp, *, layer, n, nb):
    t, dm = x.shape
    wd = gg.shape[-1]
    return pl.pallas_call(
        functools.partial(_gmlp_short_body, n=n, nb=nb), name="gmlp_short",
        grid=(1,),
        in_specs=[_full(x.shape), _full(g.shape), _at_layer(w, layer), _full(gg.shape), _full(wexp.shape),
                  _full(bexp.shape)],
        out_specs=[_full((t, wd)), _full((t, wd))],
        out_shape=[jax.ShapeDtypeStruct((t, wd), BF16), jax.ShapeDtypeStruct((t, wd), F32)],
        compiler_params=_cp("arbitrary"),
    )(x, g, w, gg, wexp, bexp)


def _qkv_prompt_body(x_ref, g_ref, w_ref, wck_ref, wcv_ref, *refs, qw, kvw, dh, kt, q_scale, n_alias):
    (q_ref, gt_ref, kcmp_ref, vcmp_ref, ka_ref, vst_ref, kwg_ref, vwt_ref) = refs[n_alias:n_alias + 8]
    stack_refs = refs[n_alias + 8:]
    tm = x_ref.shape[0]
    ng = kvw // dh
    xn = _rms(x_ref[...], g_ref[...]).astype(BF16)
    z = _dot(xn, w_ref[...])
    parts = [z[:, qw + j * kvw:qw + (j + 1) * kvw] for j in range(6)]
    parts_t = [p.T for p in parts]
    for ref, p in zip(stack_refs, parts_t):
        ref[...] = p
    q_ref[...] = (z[:, :qw] * q_scale).T.astype(BF16)
    gt_ref[...] = jax.nn.sigmoid(z[:, qw + 6 * kvw:]).T
    nbt = tm // NSA_BLOCK
    kcmp_ref[...] = jnp.sum(parts[0].reshape(nbt, NSA_BLOCK, kvw) * wck_ref[...][None], axis=1)
    vcmp_ref[...] = jnp.sum(parts[1].reshape(nbt, NSA_BLOCK, kvw) * wcv_ref[...][None], axis=1)
    bpt = kt // NSA_BLOCK
    pos0 = lax.rem(pl.program_id(0) * tm, kt)
    blk = (pos0 + lax.broadcasted_iota(jnp.int32, (tm, bpt), 0)) // NSA_BLOCK
    onehot = (blk == lax.broadcasted_iota(jnp.int32, (tm, bpt), 1)).astype(BF16)
    ones_row = (lax.broadcasted_iota(jnp.int32, (BF16_ROWS, tm), 0) == 0).astype(BF16)
    ksb = parts[2].astype(BF16)
    kwb = parts[4].astype(BF16)
    vst = parts_t[3].astype(BF16)
    vwt = parts_t[5].astype(BF16)
    for gi in range(ng):
        ka_ref[gi, :, 0:dh] = ksb[:, gi * dh:(gi + 1) * dh]
        ka_ref[gi, :, dh:dh + bpt] = onehot
        kwg_ref[gi] = kwb[:, gi * dh:(gi + 1) * dh]
        for ref, v in ((vst_ref, vst), (vwt_ref, vwt)):
            ref[gi, 0:dh, :] = v[gi * dh:(gi + 1) * dh, :]
            ref[gi, dh:dh + BF16_ROWS, :] = ones_row


def _qkv_prompt_call(x, g, w, wck, wcv, stacks, *, layer, depth, nbatch, seq, tm, qw, kvw, dh):
    t, dm = x.shape
    ng = kvw // dh
    kt = min(KV_TILE, seq)
    bpt = kt // NSA_BLOCK
    nbt = tm // NSA_BLOCK
    nst = seq // tm
    assert t % tm == 0 and tm % (8 * NSA_BLOCK) == 0 and seq % tm == 0 and kt % tm == 0 and tm % LANES == 0
    row = lambda n: pl.BlockSpec((tm, n), lambda i: (i, 0))
    col = lambda n: pl.BlockSpec((n, tm), lambda i: (0, i))
    stack_spec = pl.BlockSpec((None, None, kvw, tm), lambda i: (layer, i // nst, 0, i % nst))
    stack_shape = jax.ShapeDtypeStruct((depth, nbatch, kvw, seq), F32)
    out_specs = [col(qw), col(LANES), pl.BlockSpec((nbt, kvw), lambda i: (i, 0)), pl.BlockSpec((nbt, kvw), lambda i: (i, 0)),
                 pl.BlockSpec((ng, tm, dh + bpt), lambda i: (0, i, 0)),
                 pl.BlockSpec((ng, dh + BF16_ROWS, tm), lambda i: (0, 0, i)),
                 pl.BlockSpec((ng, tm, dh), lambda i: (0, i, 0)),
                 pl.BlockSpec((ng, dh + BF16_ROWS, tm), lambda i: (0, 0, i))] + [stack_spec] * 6
    out_shape = [jax.ShapeDtypeStruct((qw, t), BF16), jax.ShapeDtypeStruct((LANES, t), F32),
                 jax.ShapeDtypeStruct((t // NSA_BLOCK, kvw), F32), jax.ShapeDtypeStruct((t // NSA_BLOCK, kvw), F32),
                 jax.ShapeDtypeStruct((ng, t, dh + bpt), BF16), jax.ShapeDtypeStruct((ng, dh + BF16_ROWS, t), BF16),
                 jax.ShapeDtypeStruct((ng, t, dh), BF16),
                 jax.ShapeDtypeStruct((ng, dh + BF16_ROWS, t), BF16)] + [stack_shape] * 6
    scale = dh ** -0.5
    q_scale = scale if _is_power_of_two(scale) else 1.0
    n_alias = len(stacks)
    body = functools.partial(_qkv_prompt_body, qw=qw, kvw=kvw, dh=dh, kt=kt, q_scale=q_scale, n_alias=n_alias)
    return pl.pallas_call(
        body, name="qkv_prompt",
        grid=(t // tm,),
        in_specs=[row(dm), _full(g.shape), _at_layer(w, layer), _full(wck.shape), _full(wcv.shape)]
        + [pl.BlockSpec(memory_space=pl.ANY)] * n_alias,
        out_specs=out_specs, out_shape=out_shape,
        input_output_aliases={5 + k: 8 + k for k in range(n_alias)},
        compiler_params=_cp("parallel"),
    )(x, g, w, wck, wcv, *stacks)


def _qkv_sample_body(x_ref, g_ref, w_ref, q_ref, kc_ref, vc_ref, ks_ref, vs_ref, kw_ref, vw_ref, gt_ref,
                     *, qw, kvw, dh):
    xn = _rms(x_ref[...], g_ref[...]).astype(BF16)
    z = _dot(xn, w_ref[...])
    for j, ref in enumerate((kc_ref, vc_ref, ks_ref, vs_ref, kw_ref, vw_ref)):
        for gi in range(kvw // dh):
            ref[:, gi, :] = z[:, qw + j * kvw + gi * dh:qw + j * kvw + (gi + 1) * dh]
    q_ref[...] = z[:, :qw].astype(BF16)
    gt_ref[...] = jax.nn.sigmoid(z[:, qw + 6 * kvw:])


def _qkv_sample_call(x, g, w, *, layer, qw, kvw, dh):
    t, dm = x.shape
    ng = kvw // dh
    return pl.pallas_call(
        functools.partial(_qkv_sample_body, qw=qw, kvw=kvw, dh=dh), name="qkv_sample",
        grid=(1,),
        in_specs=[_full(x.shape), _full(g.shape), _at_layer(w, layer)],
        out_specs=[_full((t, qw))] + [_full((t, ng, dh))] * 6 + [_full((t, LANES))],
        out_shape=[jax.ShapeDtypeStruct((t, qw), BF16)] + [jax.ShapeDtypeStruct((t, ng, dh), F32)] * 6
        + [jax.ShapeDtypeStruct((t, LANES), F32)],
        compiler_params=_cp("arbitrary"),
    )(x, g, w)


def _select_blocks(score, blk, nblk, count, axis):
    for _ in range(count):
        m = jnp.max(score, axis=axis, keepdims=True)
        idx = jnp.min(jnp.where(score == m, blk, nblk), axis=axis, keepdims=True)
        score = jnp.where(blk == idx, M_INIT, score)
    return score == M_INIT


def _nsa_cw_body(qt_ref, gt_ref, kc_ref, vct_ref, kw_ref, vwt_ref, ocw_ref, selb_ref,
                 *, seq, ng, rep, dh, s_scale):
    c = pl.program_id(1)
    qb = qt_ref.shape[1]
    nblk = kc_ref.shape[0]
    band = min(NSA_WINDOW + qb, seq)
    st = pl.multiple_of(jnp.maximum(c * qb - NSA_WINDOW, 0), qb)
    cols = rep * qb
    t_q = c * qb + lax.broadcasted_iota(jnp.int32, (1, qb), 1)
    t_c = jnp.concatenate([t_q] * rep, axis=1)
    ok_c = (lax.broadcasted_iota(jnp.int32, (nblk, cols), 0) + 1) * NSA_BLOCK - 1 <= t_c
    kp = st + lax.broadcasted_iota(jnp.int32, (band, cols), 0)
    ok_w = (kp <= t_c) & (kp > t_c - NSA_WINDOW)
    blk = lax.broadcasted_iota(jnp.int32, (nblk, qb), 0)
    cur = t_q // NSA_BLOCK
    for gi in range(ng):
        heads = [gi * rep + r for r in range(rep)]
        lanes = slice(gi * dh, (gi + 1) * dh)
        qg = jnp.concatenate([qt_ref[h * dh:(h + 1) * dh, :] for h in heads], axis=1)
        s_c = _dot(kc_ref[:, lanes].astype(BF16), qg)
        if s_scale != 1.0:
            s_c = s_c * s_scale
        p_c = _msoftmax(s_c, ok_c, axis=0)
        o_c = _dot(vct_ref[lanes, :].astype(BF16), p_c.astype(BF16))
        imp = p_c[:, 0:qb]
        for r in range(1, rep):
            imp = imp + p_c[:, r * qb:(r + 1) * qb]
        score = jnp.where(blk == cur, SEL_FORCE, jnp.where(blk < cur, imp, -1.0))
        sel = _select_blocks(score, blk, nblk, min(NSA_TOPN, nblk), axis=0)
        selb_ref[gi] = jnp.where(sel, 0.0, NEG).astype(BF16)
        s_w = _dot(kw_ref[gi, pl.ds(st, band), :], qg)
        if s_scale != 1.0:
            s_w = s_w * s_scale
        s_w = jnp.where(ok_w, s_w, NEG)
        e_w = jnp.exp(s_w - jnp.max(s_w, axis=0, keepdims=True))
        ov = _dot(vwt_ref[gi, :, pl.ds(st, band)], e_w.astype(BF16))
        o_w = ov[0:dh] / ov[dh:dh + 1]
        g_c = jnp.concatenate([gt_ref[3 * h:3 * h + 1, :] for h in heads], axis=1)
        g_w = jnp.concatenate([gt_ref[3 * h + 2:3 * h + 3, :] for h in heads], axis=1)
        ocw = g_c * o_c + g_w * o_w
        for r, h in enumerate(heads):
            ocw_ref[h * dh:(h + 1) * dh, :] = ocw[:, r * qb:(r + 1) * qb]


def _nsa_cw_call(qt, gates_t, kcmp, vcmp_t, kwg, vwt, *, nbatch, seq, ng, dh, s_scale):
    qw, t = qt.shape
    kvw = ng * dh
    rep = qw // kvw
    qb = min(Q_BLOCK, seq)
    nq = seq // qb
    nblk = seq // NSA_BLOCK
    body = functools.partial(_nsa_cw_body, seq=seq, ng=ng, rep=rep, dh=dh, s_scale=s_scale)
    qcol = lambda n: pl.BlockSpec((n, qb), lambda b, c: (0, b * nq + c))
    return pl.pallas_call(
        body, name="nsa_cmp_win",
        grid=(nbatch, nq),
        in_specs=[qcol(qw), qcol(LANES),
                  pl.BlockSpec((nblk, kvw), lambda b, c: (b, 0)), pl.BlockSpec((None, kvw, nblk), lambda b, c: (b, 0, 0)),
                  pl.BlockSpec((ng, seq, dh), lambda b, c: (0, b, 0)),
                  pl.BlockSpec((ng, dh + BF16_ROWS, seq), lambda b, c: (0, 0, b))],
        out_specs=[qcol(qw), pl.BlockSpec((ng, None, nblk, qb), lambda b, c: (0, b * nq + c, 0, 0))],
        out_shape=[jax.ShapeDtypeStruct((qw, t), F32), jax.ShapeDtypeStruct((ng, nbatch * nq, nblk, qb), BF16)],
        compiler_params=_cp("parallel", "arbitrary"),
    )(qt, gates_t, kcmp, vcmp_t, kwg, vwt)


def _nsa_sel_body(qt_ref, selb_ref, ka_ref, vst_ref, ocw_ref, gt_ref, o_ref, m_sc, acc_sc, sa_sc, sb_sc, ma_sc, mb_sc,
                  *, rep, dh, kt, s_scale):
    g = pl.program_id(1)
    c = pl.program_id(2)
    qb = qt_ref.shape[1]
    cols = rep * qb
    bpt = kt // NSA_BLOCK
    qg = jnp.concatenate([qt_ref[r * dh:(r + 1) * dh, :] for r in range(rep)], axis=1)
    m_sc[...] = jnp.full(m_sc.shape, M_INIT, F32)
    acc_sc[...] = jnp.zeros(acc_sc.shape, F32)

    def scores(j, s_ref, mt_ref):
        st = pl.multiple_of(j * kt, kt)
        sb = selb_ref[pl.ds(pl.multiple_of(j * bpt, bpt), bpt), :]
        qa = jnp.concatenate([qg, jnp.concatenate([sb] * rep, axis=1)], axis=0)
        s = _dot(ka_ref[pl.ds(st, kt), :], qa)
        if s_scale != 1.0:
            s = s * s_scale
        s_ref[...] = s
        mt_ref[...] = jnp.max(s, axis=0, keepdims=True)

    def absorb(j, s, m_tile):
        st = pl.multiple_of(j * kt, kt)
        m_old = m_sc[...]
        m_new = jnp.maximum(m_old, m_tile)
        p = jnp.exp(s - m_new)
        acc_sc[...] = jnp.exp(m_old - m_new) * acc_sc[...] + _dot(vst_ref[:, pl.ds(st, kt)], p.astype(BF16))
        m_sc[...] = m_new

    def finish(s_ref):
        off = pl.multiple_of(c * qb - n_full * kt, qb)
        own = pl.ds(off, qb)
        key_i = lax.broadcasted_iota(jnp.int32, (qb, cols), 0)
        qry_i = lax.broadcasted_iota(jnp.int32, (qb, cols), 1) & (qb - 1)
        s_ref[own, :] = jnp.where(key_i <= qry_i, s_ref[own, :], NEG)
        s = jnp.where(lax.broadcasted_iota(jnp.int32, (kt, cols), 0) < off + qb, s_ref[...], NEG)
        absorb(n_full, s, jnp.max(s, axis=0, keepdims=True))
        acc = acc_sc[...]
        o_s = acc[0:dh] / acc[dh:dh + 1]
        out = []
        for r in range(rep):
            g_s = gt_ref[pl.ds(3 * (g * rep + r) + 1, 1), :]
            out.append(ocw_ref[r * dh:(r + 1) * dh, :] + g_s * o_s[:, r * qb:(r + 1) * qb])
        o_ref[...] = jnp.concatenate(out, axis=0).T.astype(o_ref.dtype)

    n_full = (c * qb) // kt
    scores(0, sa_sc, ma_sc)

    def pair(k, carry):
        scores(2 * k + 1, sb_sc, mb_sc)
        absorb(2 * k, sa_sc[...], ma_sc[...])
        scores(2 * k + 2, sa_sc, ma_sc)
        absorb(2 * k + 1, sb_sc[...], mb_sc[...])
        return carry

    lax.fori_loop(0, n_full // 2, pair, 0)

    @pl.when(n_full % 2 == 1)
    def _():
        scores(n_full, sb_sc, mb_sc)
        absorb(n_full - 1, sa_sc[...], ma_sc[...])
        finish(sb_sc)

    @pl.when(n_full % 2 == 0)
    def _():
        finish(sa_sc)


def _nsa_sel_call(qt, selb, ka, vst, ocw_t, gates_t, *, nbatch, seq, ng, dh, s_scale):
    qw, t = qt.shape
    rep = qw // (ng * dh)
    qb = min(Q_BLOCK, seq)
    assert qb & (qb - 1) == 0
    nq = seq // qb
    nblk = selb.shape[2]
    kt = min(KV_TILE, seq)
    bpt = kt // NSA_BLOCK
    assert seq % kt == 0 and kt % qb == 0 and bpt % BF16_ROWS == 0 and ka.shape[-1] == dh + bpt
    body = functools.partial(_nsa_sel_body, rep=rep, dh=dh, kt=kt, s_scale=s_scale)
    qcol = pl.BlockSpec((rep * dh, qb), lambda b, g, c: (g, b * nq + c))
    return pl.pallas_call(
        body, name="nsa_selected",
        grid=(nbatch, ng, nq),
        in_specs=[qcol,
                  pl.BlockSpec((None, None, nblk, qb), lambda b, g, c: (g, b * nq + c, 0, 0)),
                  pl.BlockSpec((None, seq, dh + bpt), lambda b, g, c: (g, b, 0)),
                  pl.BlockSpec((None, dh + BF16_ROWS, seq), lambda b, g, c: (g, 0, b)),
                  qcol,
                  pl.BlockSpec((LANES, qb), lambda b, g, c: (0, b * nq + c))],
        out_specs=pl.BlockSpec((qb, rep * dh), lambda b, g, c: (b * nq + c, g)),
        out_shape=jax.ShapeDtypeStruct((t, qw), BF16),
        scratch_shapes=[pltpu.VMEM((1, rep * qb), F32), pltpu.VMEM((dh + BF16_ROWS, rep * qb), F32),
                        pltpu.VMEM((kt, rep * qb), F32), pltpu.VMEM((kt, rep * qb), F32),
                        pltpu.VMEM((1, rep * qb), F32), pltpu.VMEM((1, rep * qb), F32)],
        compiler_params=_cp("parallel", "parallel", "arbitrary"),
    )(qt, selb, ka, vst, ocw_t, gates_t)


def _page_specs(pps, kvw, page, layer):
    return [pl.BlockSpec((None, None, kvw, page), lambda b, j, pt, i=i: (layer, pt[b, j * pps + i], 0, 0))
            for i in range(pps)]


def _cmp_pages_body(pt_ref, *refs, pps):
    del pt_ref
    k_refs, v_refs = refs[:pps], refs[pps:2 * pps]
    wk_ref, wv_ref, kc_ref, vc_ref = refs[2 * pps:]
    j = pl.program_id(1)
    kvw, page = k_refs[0].shape
    nblk = kc_ref.shape[1]
    bpp = page // NSA_BLOCK

    @pl.when(j == 0)
    def _():
        kc_ref[...] = jnp.zeros(kc_ref.shape, F32)
        vc_ref[...] = jnp.zeros(vc_ref.shape, F32)

    row_blk = lax.broadcasted_iota(jnp.int32, (page, nblk), 0) // NSA_BLOCK
    col = lax.broadcasted_iota(jnp.int32, (page, nblk), 1)
    for page_refs, w_ref, o_ref in ((k_refs, wk_ref, kc_ref), (v_refs, wv_ref, vc_ref)):
        acc = jnp.zeros((kvw, nblk), F32)
        for i, r in enumerate(page_refs):
            y = r[...] * w_ref[...]
            hi = y.astype(BF16)
            lo = (y - hi.astype(F32)).astype(BF16)
            onehot = (col == (j * pps + i) * bpp + row_blk).astype(BF16)
            acc = acc + _dot(jnp.concatenate([hi, lo], axis=1), jnp.concatenate([onehot, onehot], axis=0))
        o_ref[...] += acc


def _cmp_pages_call(page_table, pool_k, pool_v, wk_t, wv_t, *, layer, pps):
    nbs, npg = page_table.shape
    _, _, kvw, page = pool_k.shape
    nblk = npg * (page // NSA_BLOCK)
    assert npg % pps == 0
    wspec = pl.BlockSpec((kvw, page), lambda b, j, pt: (0, 0))
    grid_spec = pltpu.PrefetchScalarGridSpec(
        num_scalar_prefetch=1, grid=(nbs, npg // pps),
        in_specs=_page_specs(pps, kvw, page, layer) * 2 + [wspec] * 2,
        out_specs=[pl.BlockSpec((None, kvw, nblk), lambda b, j, pt: (b, 0, 0))] * 2)
    return pl.pallas_call(
        functools.partial(_cmp_pages_body, pps=pps), name="sample_cmp_pages",
        grid_spec=grid_spec,
        out_shape=[jax.ShapeDtypeStruct((nbs, kvw, nblk), F32)] * 2,
        compiler_params=_cp("parallel", "arbitrary"),
    )(page_table, *([pool_k] * pps), *([pool_v] * pps), wk_t, wv_t)


def _samp_cw_body(q_ref, g_ref, kct_ref, vct_ref, wk_ref, wv_ref, kn_ref, vn_ref, *refs,
                  ts, past, rep, scale, n_alias):
    ocw_ref, selb_ref, wko_ref, wvo_ref = refs[n_alias:]
    rows = q_ref.shape[0]
    kvw, nblk = kct_ref.shape
    wb = wk_ref.shape[1]
    npad = kn_ref.shape[1]
    gt = rows // rep
    q = q_ref[...]
    ti = lax.rem(lax.broadcasted_iota(jnp.int32, (rows, 1), 0), ts)
    t = past + ti
    blk_r = lax.broadcasted_iota(jnp.int32, (rows, nblk), 1)
    p_c = _msoftmax(_dot(q, kct_ref[...].astype(BF16)) * scale, (blk_r + 1) * NSA_BLOCK - 1 <= t)
    o_c = _dot_nt(p_c.astype(BF16), vct_ref[...].astype(BF16))
    imp = p_c[0:gt]
    for r in range(1, rep):
        imp = imp + p_c[r * gt:(r + 1) * gt]
    blk = lax.broadcasted_iota(jnp.int32, (gt, nblk), 1)
    sel = _select_blocks(imp, blk, nblk, min(NSA_TOPN - 1, nblk), axis=1)
    selb_ref[...] = jnp.concatenate([jnp.where(sel, 0.0, NEG)] * rep, axis=0)
    wk, wv, kn, vn = wk_ref[...], wv_ref[...], kn_ref[...], vn_ref[...]
    kp1 = past - wb + lax.broadcasted_iota(jnp.int32, (rows, wb), 1)
    j2 = lax.broadcasted_iota(jnp.int32, (rows, npad), 1)
    ok1 = (kp1 <= t) & (kp1 > t - NSA_WINDOW) & (kp1 >= 0)
    ok2 = (j2 <= ti) & (j2 < ts)
    s1 = jnp.where(ok1, _dot(q, wk.astype(BF16)) * scale, NEG)
    s2 = jnp.where(ok2, _dot(q, kn.astype(BF16)) * scale, NEG)
    m = jnp.maximum(jnp.max(s1, axis=1, keepdims=True), jnp.max(s2, axis=1, keepdims=True))
    e1 = jnp.where(ok1, jnp.exp(s1 - m), 0.0)
    e2 = jnp.where(ok2, jnp.exp(s2 - m), 0.0)
    den = jnp.maximum(jnp.sum(e1, axis=1, keepdims=True) + jnp.sum(e2, axis=1, keepdims=True), 1e-30)
    o_w = _dot_nt((e1 / den).astype(BF16), wv.astype(BF16)) + _dot_nt((e2 / den).astype(BF16), vn.astype(BF16))
    g = g_ref[...]
    ocw_ref[...] = g[:, 0:1] * o_c + g[:, 2:3] * o_w
    lane = lax.broadcasted_iota(jnp.int32, (kvw, wb), 1)

    def advance(old, new):
        tail = jnp.concatenate([jnp.zeros((kvw, wb - npad), F32), pltpu.roll(new, npad - ts, axis=1)], axis=1)
        return jnp.where(lane >= wb - ts, tail, pltpu.roll(old, wb - ts, axis=1))

    wko_ref[...] = advance(wk, kn)
    wvo_ref[...] = advance(wv, vn)


def _samp_cw_call(q_bd, gall, kct, vct, win_k, win_v, kn_t, vn_t, stacks, *, layer, depth, ts, past, rep, dh):
    nbs, rows, kvw = q_bd.shape
    nblk = kct.shape[2]
    wb = win_k.shape[3]
    npad = kn_t.shape[2]
    assert wb >= npad
    per_b = lambda *s: pl.BlockSpec((None,) + s, lambda b: (b,) + (0,) * len(s))
    win = pl.BlockSpec((None, None, kvw, wb), lambda b: (layer, b, 0, 0))
    n_alias = len(stacks)
    body = functools.partial(_samp_cw_body, ts=ts, past=past, rep=rep, scale=dh ** -0.5, n_alias=n_alias)
    return pl.pallas_call(
        body, name="sample_cmp_win",
        grid=(nbs,),
        in_specs=[per_b(rows, kvw), per_b(rows, LANES), per_b(kvw, nblk), per_b(kvw, nblk), win, win,
                  per_b(kvw, npad), per_b(kvw, npad)] + [pl.BlockSpec(memory_space=pl.ANY)] * n_alias,
        out_specs=[per_b(rows, kvw), per_b(rows, nblk), win, win],
        out_shape=[jax.ShapeDtypeStruct((nbs, rows, kvw), F32), jax.ShapeDtypeStruct((nbs, rows, nblk), F32),
                   jax.ShapeDtypeStruct((depth, nbs, kvw, wb), F32), jax.ShapeDtypeStruct((depth, nbs, kvw, wb), F32)],
        input_output_aliases={8 + k: 2 + k for k in range(n_alias)},
        compiler_params=_cp("parallel"),
    )(q_bd, gall, kct, vct, win_k, win_v, kn_t, vn_t, *stacks)


def _samp_sel_body(pt_ref, *refs, pps, ts, scale):
    del pt_ref
    k_refs, v_refs = refs[:pps], refs[pps:2 * pps]
    q_ref, selb_ref, kn_ref, vn_ref, ocw_ref, g_ref, o_ref, m_sc, l_sc, acc_sc = refs[2 * pps:]
    j = pl.program_id(1)
    rows = q_ref.shape[0]
    nblk = selb_ref.shape[1]
    page = k_refs[0].shape[1]
    kt = pps * page
    q = q_ref[...]

    @pl.when(j == 0)
    def _():
        m_sc[...] = jnp.full(m_sc.shape, M_INIT, F32)
        l_sc[...] = jnp.zeros(l_sc.shape, F32)
        acc_sc[...] = jnp.zeros(acc_sc.shape, F32)

    def update(s, v_t):
        m_old = m_sc[...]
        m_new = jnp.maximum(m_old, jnp.max(s, axis=1, keepdims=True))
        alpha = jnp.exp(m_old - m_new)
        p = jnp.exp(s - m_new)
        l_sc[...] = alpha * l_sc[...] + jnp.sum(p, axis=1, keepdims=True)
        acc_sc[...] = alpha * acc_sc[...] + _dot_nt(p.astype(BF16), v_t)
        m_sc[...] = m_new

    k_t = jnp.concatenate([r[...] for r in k_refs], axis=1).astype(BF16)
    v_t = jnp.concatenate([r[...] for r in v_refs], axis=1).astype(BF16)
    key_blk = j * (kt // NSA_BLOCK) + lax.broadcasted_iota(jnp.int32, (nblk, kt), 1) // NSA_BLOCK
    onehot = (key_blk == lax.broadcasted_iota(jnp.int32, (nblk, kt), 0)).astype(BF16)
    update(_dot(q, k_t) * scale + _dot(selb_ref[...].astype(BF16), onehot), v_t)

    @pl.when(j == pl.num_programs(1) - 1)
    def _():
        npad = kn_ref.shape[1]
        ti = lax.rem(lax.broadcasted_iota(jnp.int32, (rows, npad), 0), ts)
        j2 = lax.broadcasted_iota(jnp.int32, (rows, npad), 1)
        s = jnp.where((j2 <= ti) & (j2 < ts), _dot(q, kn_ref[...].astype(BF16)) * scale, NEG)
        update(s, vn_ref[...].astype(BF16))
        o_ref[...] = ocw_ref[...] + g_ref[:, 1:2] * (acc_sc[...] / l_sc[...])


def _samp_sel_call(page_table, pool_k, pool_v, q_bd, selb, kn_t, vn_t, ocw, gall, *, layer, pps, ts, dh):
    nbs, npg = page_table.shape
    _, _, kvw, page = pool_k.shape
    rows = q_bd.shape[1]
    per_b = lambda *s: pl.BlockSpec((None,) + s, lambda b, j, pt: (b,) + (0,) * len(s))
    npad = kn_t.shape[2]
    grid_spec = pltpu.PrefetchScalarGridSpec(
        num_scalar_prefetch=1, grid=(nbs, npg // pps),
        in_specs=_page_specs(pps, kvw, page, layer) * 2 + [
            per_b(rows, kvw), per_b(rows, selb.shape[2]), per_b(kvw, npad), per_b(kvw, npad),
            per_b(rows, kvw), per_b(rows, LANES)],
        out_specs=per_b(rows, kvw),
        scratch_shapes=[pltpu.VMEM((rows, 1), F32), pltpu.VMEM((rows, 1), F32), pltpu.VMEM((rows, kvw), F32)])
    body = functools.partial(_samp_sel_body, pps=pps, ts=ts, scale=dh ** -0.5)
    return pl.pallas_call(
        body, name="sample_selected",
        grid_spec=grid_spec,
        out_shape=jax.ShapeDtypeStruct((nbs, rows, kvw), F32),
        compiler_params=_cp("parallel", "arbitrary"),
    )(page_table, *([pool_k] * pps), *([pool_v] * pps), q_bd, selb, kn_t, vn_t, ocw, gall)


def _merge_body(x_ref, g_ref, wmg_ref, oa_ref, ob_ref, oc_ref, wb_ref, wo_ref, o_ref):
    x = x_ref[...]
    dm = x.shape[1]
    xn = _rms(x, g_ref[...]).astype(BF16)
    mg = jax.nn.sigmoid(_dot(xn, wmg_ref[...]))
    y = mg[:, 0:dm] * _dot(oa_ref[...], wb_ref[0])
    y = y + mg[:, dm:2 * dm] * _dot(ob_ref[...], wb_ref[1])
    y = y + mg[:, 2 * dm:3 * dm] * _dot(oc_ref[...], wb_ref[2])
    o_ref[...] = x + _dot(y.astype(BF16), wo_ref[...])


def _merge_call(x, g, wmg, oa, ob, oc, wb, wo, *, layer, tm):
    t, dm = x.shape
    row = pl.BlockSpec((tm, dm), lambda i: (i, 0))
    return pl.pallas_call(
        _merge_body, name="merge",
        grid=(t // tm,),
        in_specs=[row, _full(g.shape), _at_layer(wmg, layer), row, row, row, _at_layer(wb, layer),
                  _at_layer(wo, layer)],
        out_specs=row,
        out_shape=jax.ShapeDtypeStruct((t, dm), F32),
        compiler_params=_cp("parallel"),
    )(x, g, wmg, oa, ob, oc, wb, wo)


def _ffn_body(x_ref, g_ref, w1_ref, w3_ref, w2_ref, fg_ref, o_ref, xn_sc, acc_sc, *, final):
    f = pl.program_id(1)

    @pl.when(f == 0)
    def _():
        xn_sc[...] = _rms(x_ref[...], g_ref[...]).astype(BF16)
        acc_sc[...] = jnp.zeros(acc_sc.shape, F32)

    xn = xn_sc[...]
    h = jax.nn.silu(_dot(xn, w1_ref[...])) * _dot(xn, w3_ref[...])
    acc_sc[...] += _dot(h.astype(BF16), w2_ref[...])

    @pl.when(f == pl.num_programs(1) - 1)
    def _():
        y = x_ref[...] + acc_sc[...]
        o_ref[...] = _rms(y, fg_ref[...]) if final else y


def _ffn_tile(f_dim):
    half = f_dim // 2
    return half if f_dim % 2 == 0 and half % LANES == 0 else f_dim


def _ffn_call(x, g, w1, w3, w2, fg, *, layer, tm, final):
    t, dm = x.shape
    fd = w1.shape[-1]
    ft = _ffn_tile(fd)
    row = pl.BlockSpec((tm, dm), lambda i, f: (i, 0))
    vec = pl.BlockSpec((1, dm), lambda i, f: (0, 0))
    w_in_spec = pl.BlockSpec((None, dm, ft), lambda i, f: (layer, 0, f))
    return pl.pallas_call(
        functools.partial(_ffn_body, final=final), name="ffn",
        grid=(t // tm, fd // ft),
        in_specs=[row, vec, w_in_spec, w_in_spec, pl.BlockSpec((None, ft, dm), lambda i, f: (layer, f, 0)), vec],
        out_specs=row,
        out_shape=jax.ShapeDtypeStruct((t, dm), F32),
        scratch_shapes=[pltpu.VMEM((tm, dm), BF16), pltpu.VMEM((tm, dm), F32)],
        compiler_params=_cp("parallel", "arbitrary"),
    )(x, g, w1, w3, w2, fg)


def _route_body(x_ref, g_ref, wr_ref, br_ref, xn_ref, comb_ref, key_ref, keyt_ref, cnt_ref):
    tm = x_ref.shape[0]
    xn = _rms(x_ref[...], g_ref[...])
    xn_ref[...] = xn.astype(BF16)
    logits = jnp.dot(xn, wr_ref[...], preferred_element_type=F32, precision=lax.Precision.HIGHEST) + br_ref[...]
    lane = lax.broadcasted_iota(jnp.int32, logits.shape, 1)
    nl = logits.shape[1]
    m1 = jnp.max(logits, axis=1, keepdims=True)
    i1 = jnp.min(jnp.where(logits == m1, lane, nl), axis=1, keepdims=True)
    rest = jnp.where(lane == i1, M_INIT, logits)
    m2 = jnp.max(rest, axis=1, keepdims=True)
    i2 = jnp.min(jnp.where(rest == m2, lane, nl), axis=1, keepdims=True)
    e2 = jnp.exp(m2 - m1)
    den = 1.0 + e2
    comb_ref[...] = jnp.where(lane == i1, 1.0 / den, 0.0) + jnp.where(lane == i2, e2 / den, 0.0)
    member = (lane == i1) | (lane == i2)
    earlier = lax.broadcasted_iota(jnp.int32, (tm, tm), 1) < lax.broadcasted_iota(jnp.int32, (tm, tm), 0)
    rank = _dot(earlier.astype(BF16), member.astype(BF16))
    key = jnp.where(member, rank, -1.0)
    key_ref[...] = key
    keyt_ref[...] = key.T
    count = jnp.sum(member.astype(F32), axis=0, keepdims=True).astype(jnp.int32)
    cnt_ref[...] = jnp.broadcast_to(count, cnt_ref.shape)


def _route_call(x, g, wr, br, *, tm):
    t, dm = x.shape
    nl = wr.shape[1]
    nt = t // tm
    row = lambda n: pl.BlockSpec((tm, n), lambda i: (i, 0))
    return pl.pallas_call(
        _route_body, name="moe_route",
        grid=(nt,),
        in_specs=[row(dm), _full(g.shape), _full(wr.shape), _full(br.shape)],
        out_specs=[row(dm), row(nl), row(nl), pl.BlockSpec((None, nl, tm), lambda i: (i, 0, 0)),
                   pl.BlockSpec((None, 8, nl), lambda i: (i, 0, 0))],
        out_shape=[jax.ShapeDtypeStruct((t, dm), BF16), jax.ShapeDtypeStruct((t, nl), F32),
                   jax.ShapeDtypeStruct((t, nl), F32), jax.ShapeDtypeStruct((nt, nl, tm), F32),
                   jax.ShapeDtypeStruct((nt, 8, nl), jnp.int32)],
        compiler_params=_cp("parallel"),
    )(x, g, wr, br)


def _moe_body(cnt_ref, x_ref, xn_ref, comb_ref, key_ref, keyt_ref, w1_ref, w3_ref, w2_ref, fg_ref, o_ref,
              xe_sc, ye_sc, acc_sc, *, cap, final):
    i = pl.program_id(0)
    e = pl.program_id(1)
    f = pl.program_id(2)
    tm = x_ref.shape[0]
    n_batches = (cnt_ref[i, e] + cap - 1) // cap
    rows_of = lambda b: pl.ds(pl.multiple_of(b * cap, cap), cap)

    @pl.when((e == 0) & (f == 0))
    def _():
        acc_sc[...] = jnp.zeros(acc_sc.shape, F32)

    @pl.when(f == 0)
    def _():
        key_row = keyt_ref[pl.ds(e, 1), :]

        def gather(b, carry):
            slot = (b * cap + lax.broadcasted_iota(jnp.int32, (cap, tm), 0)).astype(F32)
            xe_sc[rows_of(b), :] = _dot((key_row == slot).astype(BF16), xn_ref[...]).astype(BF16)
            return carry

        lax.fori_loop(0, n_batches, gather, 0)

    def expert(b, carry):
        xe = xe_sc[rows_of(b), :]
        h = jax.nn.silu(_dot(xe, w1_ref[...])) * _dot(xe, w3_ref[...])
        part = _dot(h.astype(BF16), w2_ref[...])

        @pl.when(f == 0)
        def _():
            ye_sc[rows_of(b), :] = part

        @pl.when(f != 0)
        def _():
            ye_sc[rows_of(b), :] += part

        return carry

    lax.fori_loop(0, n_batches, expert, 0)

    @pl.when(f == pl.num_programs(2) - 1)
    def _():
        lane = lax.broadcasted_iota(jnp.int32, key_ref.shape, 1)
        key_col = jnp.sum(jnp.where(lane == e, key_ref[...], 0.0), axis=1, keepdims=True)
        gate = jnp.sum(jnp.where(lane == e, comb_ref[...], 0.0), axis=1, keepdims=True)

        def scatter(b, carry):
            slot = (b * cap + lax.broadcasted_iota(jnp.int32, (tm, cap), 1)).astype(F32)
            onehot = (key_col == slot).astype(BF16)
            ye = ye_sc[rows_of(b), :]
            hi = ye.astype(BF16)
            lo = (ye - hi.astype(F32)).astype(BF16)
            back = _dot(jnp.concatenate([onehot, onehot], axis=1), jnp.concatenate([hi, lo], axis=0))
            acc_sc[...] += gate * back
            return carry

        lax.fori_loop(0, n_batches, scatter, 0)

    @pl.when((e == pl.num_programs(1) - 1) & (f == pl.num_programs(2) - 1))
    def _():
        y = x_ref[...] + acc_sc[...]
        o_ref[...] = _rms(y, fg_ref[...]) if final else y


def _moe_call(x, g, wr, br, w1, w3, w2, fg, *, layer, tm, final):
    t, dm = x.shape
    _, ne, _, fd = w1.shape
    nl = wr.shape[1]
    ft = _ffn_tile(fd)
    cap = min(MOE_BATCH, tm)
    assert cap % BF16_ROWS == 0
    max_rows = -(-tm // cap) * cap
    xn, comb, key, key_t, counts = _route_call(x, g, wr, br, tm=tm)
    once = dict(pipeline_mode=pl.Buffered(1))
    row = lambda n: pl.BlockSpec((tm, n), lambda i, e, f, cnt: (i, 0), **once)
    grid_spec = pltpu.PrefetchScalarGridSpec(
        num_scalar_prefetch=1, grid=(t // tm, ne, fd // ft),
        in_specs=[row(dm), row(dm), row(nl), row(nl),
                  pl.BlockSpec((None, nl, tm), lambda i, e, f, cnt: (i, 0, 0), **once),
                  pl.BlockSpec((None, None, dm, ft), lambda i, e, f, cnt: (layer, e, 0, f)),
                  pl.BlockSpec((None, None, dm, ft), lambda i, e, f, cnt: (layer, e, 0, f)),
                  pl.BlockSpec((None, None, ft, dm), lambda i, e, f, cnt: (layer, e, f, 0)),
                  pl.BlockSpec((1, dm), lambda i, e, f, cnt: (0, 0))],
        out_specs=pl.BlockSpec((tm, dm), lambda i, e, f, cnt: (i, 0)),
        scratch_shapes=[pltpu.VMEM((max_rows, dm), BF16), pltpu.VMEM((max_rows, dm), F32), pltpu.VMEM((tm, dm), F32)])
    return pl.pallas_call(
        functools.partial(_moe_body, cap=cap, final=final), name="moe",
        grid_spec=grid_spec,
        out_shape=jax.ShapeDtypeStruct((t, dm), F32),
        compiler_params=_cp("parallel", "arbitrary", "arbitrary"),
    )(counts[:, 0, :ne], x, xn, comb, key, key_t, w1, w3, w2, fg)


def kernel(x_prompt, x_sample, cache_cmp_k, cache_cmp_v, cache_slc_k, cache_slc_v, cache_win_k, cache_win_v, state_conv, state_lru, page_table, norm_mix_g, w_in, lru_conv_w, lru_conv_b, lru_wa, lru_ba, lru_wx, lru_bx, lru_lambda, nsa_cmp_wk, nsa_cmp_wv, gm_norm_g, gm_ws, gm_bs, w_branch, w_out, norm_ffn_g, ffn_w1, ffn_w3, ffn_w2, moe_wr, moe_br, moe_w1, moe_w3, moe_w2, final_norm_g):
    depth, dm, _ = w_in.shape
    nbp, seq, _ = x_prompt.shape
    nbs, ts, _ = x_sample.shape
    _, n_pool, page, ng, dh = cache_cmp_k.shape
    wd = state_lru.shape[-1]
    conv_w = lru_conv_w.shape[1]
    gmw = gm_norm_g.shape[-1]
    n_gm, chunk = gm_ws.shape[1], gm_ws.shape[2]
    kvw = ng * dh
    qw = NSA_HEADS * dh
    rep = NSA_HEADS // ng
    npg = page_table.shape[1]
    past = npg * page
    wbuf = cache_win_k.shape[2]
    tp, tsm = nbp * seq, nbs * ts
    ne = moe_wr.shape[-1]
    nblk = seq // NSA_BLOCK
    scale = dh ** -0.5
    s_scale = 1.0 if _is_power_of_two(scale) else scale

    assert seq % chunk == 0 and seq % LRU_CHUNK == 0 and seq >= NSA_WINDOW
    assert past % NSA_BLOCK == 0 and ts < NSA_BLOCK and ts < chunk and wbuf == NSA_WINDOW and nbs % 8 == 0
    assert past // NSA_BLOCK >= NSA_TOPN - 1 and page % LANES == 0

    offs = [0, wd, 2 * wd, 2 * wd + qw, 2 * wd + qw + 6 * kvw, 2 * wd + qw + 6 * kvw + 3 * NSA_HEADS]
    offs += [offs[-1] + 2 * gmw, offs[-1] + 2 * gmw + 3 * dm]
    w_part = lambda lo, hi: w_in[:, :, lo:hi].astype(BF16)
    w_lru = w_part(0, offs[2])
    w_nsa = jnp.pad(w_part(offs[2], offs[5]), ((0, 0), (0, 0), (0, LANES - 3 * NSA_HEADS)))
    w_uv = w_part(offs[5], offs[6])
    w_mg = w_part(offs[6], offs[7])
    wa_b, wx_b = lru_wa.astype(BF16), lru_wx.astype(BF16)
    wb_b, wo_b = w_branch.astype(BF16), w_out.astype(BF16)
    f1_b, f3_b, f2_b = ffn_w1.astype(BF16), ffn_w3.astype(BF16), ffn_w2.astype(BF16)
    m1_b, m3_b, m2_b = moe_w1.astype(BF16), moe_w3.astype(BF16), moe_w2.astype(BF16)
    wr_pad = jnp.pad(moe_wr, ((0, 0), (0, 0), (0, LANES - ne)))
    br_pad = jnp.pad(moe_br, ((0, 0), (0, LANES - ne)), constant_values=NEG)[:, None, :]
    wck = nsa_cmp_wk.reshape(depth, NSA_BLOCK, kvw)
    wcv = nsa_cmp_wv.reshape(depth, NSA_BLOCK, kvw)
    cmp_w_t = lambda w: jnp.tile(jnp.transpose(w, (0, 2, 3, 1)).reshape(depth, kvw, NSA_BLOCK), (1, 1, page // NSA_BLOCK))
    wck_t, wcv_t = cmp_w_t(nsa_cmp_wk), cmp_w_t(nsa_cmp_wv)
    gc = gmw // n_gm
    bs_full = jnp.repeat(jnp.swapaxes(gm_bs, 1, 2), gc, axis=2)
    ws_short = jnp.repeat(jnp.transpose(gm_ws[:, :, :ts, :ts], (0, 2, 3, 1)).reshape(depth, ts * ts, n_gm), gc, axis=2)
    vec = lambda a, l: a[l][None, :]
    pps = min(PAGES_PER_STEP, npg)
    feature_major = lambda a: jnp.transpose(a, (0, 1, 3, 4, 2)).reshape(a.shape[0], a.shape[1], kvw, a.shape[2])
    pool_ck, pool_cv = feature_major(cache_cmp_k), feature_major(cache_cmp_v)
    pool_sk, pool_sv = feature_major(cache_slc_k), feature_major(cache_slc_v)
    win_k, win_v = feature_major(cache_win_k), feature_major(cache_win_v)
    eye_g = jnp.eye(ng, dtype=BF16)

    tm_p = min(ROW_TILE, tp)
    xp = x_prompt.reshape(tp, dm)
    xs = x_sample.reshape(tsm, dm)
    zeros_cs = jnp.zeros((nbp, conv_w - 1, wd), F32)
    zeros_h = jnp.zeros((nbp, 1, wd), F32)
    to_tm = lambda a, n: jnp.swapaxes(a.reshape(nbs, n, -1), 0, 1).reshape(1, n * nbs, -1)
    from_tm = lambda a, n: jnp.swapaxes(a.reshape(n, nbs, -1), 0, 1)
    new_rows_t = lambda a: jnp.pad(jnp.swapaxes(a.reshape(nbs, ts, kvw), 1, 2), ((0, 0), (0, 0), (0, LANES - ts)))

    p_stacks, w_stacks = (), ()
    p_small = [[] for _ in range(2)]
    s_out = [[] for _ in range(7)]
    for l in range(depth):
        last = l == depth - 1
        gmix = vec(norm_mix_g, l)
        lru_w = (lru_conv_w[l], vec(lru_conv_b, l), wa_b, vec(lru_ba, l), wx_b, vec(lru_bx, l), vec(lru_lambda, l))

        o_a, conv_new, h_last = _lru_call(xp.reshape(nbp, seq, dm), gmix, w_lru, zeros_cs, zeros_h, *lru_w,
                                          layer=l, nb=1, tch=min(LRU_CHUNK, seq), starts_at_zero=True)
        o_c = _gmlp_call(xp, gmix, w_uv, vec(gm_norm_g, l), gm_ws, bs_full[l], layer=l, tm=tm_p)
        qt, gates_t, kcmp, vcmp, ka, vst, kwg, vwt, *p_stacks = _qkv_prompt_call(
            xp, gmix, w_nsa, wck[l], wcv[l], p_stacks, layer=l, depth=depth, nbatch=nbp, seq=seq, tm=tm_p,
            qw=qw, kvw=kvw, dh=dh)
        vcmp_t = jnp.swapaxes(vcmp.reshape(nbp, nblk, kvw), 1, 2)
        ocw_t, selb = _nsa_cw_call(qt, gates_t, kcmp, vcmp_t, kwg, vwt, nbatch=nbp, seq=seq, ng=ng, dh=dh,
                                   s_scale=s_scale)
        o_b = _nsa_sel_call(qt, selb, ka, vst, ocw_t, gates_t, nbatch=nbp, seq=seq, ng=ng, dh=dh, s_scale=s_scale)
        xp = _merge_call(xp, gmix, w_mg, o_a.reshape(tp, wd), o_b, o_c, wb_b, wo_b, layer=l, tm=tm_p)
        if l % 2 == 0:
            xp = _ffn_call(xp, vec(norm_ffn_g, l), f1_b, f3_b, f2_b, final_norm_g[None, :],
                           layer=l // 2, tm=tm_p, final=last)
        else:
            xp = _moe_call(xp, vec(norm_ffn_g, l), wr_pad[l // 2], br_pad[l // 2], m1_b, m3_b, m2_b,
                           final_norm_g[None, :], layer=l // 2, tm=min(MOE_TILE, tp), final=last)
        p_small[0].append(conv_new)
        p_small[1].append(h_last.reshape(nbp, wd))

        xs_tm = to_tm(xs, ts)
        o_a, conv_new, h_last = _lru_call(xs_tm, gmix, w_lru, to_tm(state_conv[l], conv_w - 1),
                                          state_lru[l][None], *lru_w, layer=l, nb=nbs, tch=ts, starts_at_zero=False)
        o_a = from_tm(o_a, ts).reshape(tsm, wd)
        o_c, gm_v = _gmlp_short_call(xs_tm[0], gmix, w_uv, vec(gm_norm_g, l), ws_short[l], bs_full[l][:ts],
                                     layer=l, n=ts, nb=nbs)
        o_c = from_tm(o_c, ts).reshape(tsm, gmw)
        q, k_c, v_c, k_s, v_s, k_w, v_w, gates = _qkv_sample_call(xs, gmix, w_nsa, layer=l, qw=qw, kvw=kvw, dh=dh)
        q5 = jnp.transpose(q.reshape(nbs, ts, ng, rep, dh), (0, 3, 2, 1, 4))
        q_bd = (q5[:, :, :, :, None, :] * eye_g[None, None, :, None, :, None]).reshape(nbs, rep * ng * ts, kvw)
        g5 = jnp.transpose(gates[:, :3 * NSA_HEADS].reshape(nbs, ts, ng, rep, 3), (0, 3, 2, 1, 4))
        gall = jnp.pad(g5.reshape(nbs, rep * ng * ts, 3), ((0, 0), (0, 0), (0, LANES - 3)))
        kct, vct = _cmp_pages_call(page_table, pool_ck, pool_cv, wck_t[l], wcv_t[l], layer=l, pps=pps)
        ocw, selb, *w_stacks = _samp_cw_call(
            q_bd, gall, kct, vct, win_k, win_v, new_rows_t(k_w), new_rows_t(v_w), w_stacks,
            layer=l, depth=depth, ts=ts, past=past, rep=rep, dh=dh)
        o_all = _samp_sel_call(page_table, pool_sk, pool_sv, q_bd, selb, new_rows_t(k_s), new_rows_t(v_s),
                               ocw, gall, layer=l, pps=pps, ts=ts, dh=dh)
        o6 = o_all.reshape(nbs, rep, ng, ts, ng, dh)
        o_b = jnp.stack([o6[:, :, gi, :, gi, :] for gi in range(ng)], axis=2)
        o_b = jnp.transpose(o_b, (0, 3, 2, 1, 4)).reshape(tsm, qw).astype(BF16)
        xs = _merge_call(xs, gmix, w_mg, o_a, o_b, o_c, wb_b, wo_b, layer=l, tm=tsm)
        if l % 2 == 0:
            xs = _ffn_call(xs, vec(norm_ffn_g, l), f1_b, f3_b, f2_b, final_norm_g[None, :],
                           layer=l // 2, tm=tsm, final=last)
        else:
            xs = _moe_call(xs, vec(norm_ffn_g, l), wr_pad[l // 2], br_pad[l // 2], m1_b, m3_b, m2_b,
                           final_norm_g[None, :], layer=l // 2, tm=tsm, final=last)
        shs = (nbs, ts, ng, dh)
        for lst, a in zip(s_out, (k_c.reshape(shs), v_c.reshape(shs), k_s.reshape(shs), v_s.reshape(shs),
                                  from_tm(conv_new[0], conv_w - 1), h_last[0], from_tm(gm_v, ts))):
            lst.append(a)

    row_major = lambda a: jnp.transpose(a.reshape(a.shape[0], a.shape[1], ng, dh, a.shape[3]), (0, 1, 4, 2, 3))
    keep = min(NSA_WINDOW, seq)
    kc_st, vc_st, ks_st, vs_st, kw_st, vw_st = p_stacks
    s_small = [jnp.stack(a) for a in s_out]
    return (xp.reshape(nbp, seq, dm), xs.reshape(nbs, ts, dm),
            row_major(kc_st), row_major(vc_st), row_major(ks_st), row_major(vs_st),
            row_major(kw_st[..., seq - keep:]), row_major(vw_st[..., seq - keep:]),
            jnp.stack(p_small[0]), jnp.stack(p_small[1]),
            s_small[0], s_small[1], s_small[2], s_small[3],
            row_major(w_stacks[0]), row_major(w_stacks[1]),
            s_small[4], s_small[5], s_small[6])
```

```python
import functools
import math

import jax
import jax.numpy as jnp
from jax import lax
from jax.experimental import pallas as pl
from jax.experimental.pallas import tpu as pltpu

F32 = jnp.float32
BF16 = jnp.bfloat16

NSA_HEADS = 16
NSA_BLOCK = 64
NSA_TOPN = 8
NSA_WINDOW = 512
SEL_FORCE = 1.0e4
LRU_C = 8.0
TOP_K = 2
RMS_EPS = 1e-6
NEG = -1e30
M_INIT = -3.0e38

Q_BLOCK = 256
KV_TILE = 1024
LRU_CHUNK = 256
ROW_TILE = 512
MOE_TILE = 1024
MOE_BATCH = 288
PAGES_PER_STEP = 32
BF16_ROWS = 16
LANES = 128
VMEM_LIMIT = 56 * 2 ** 20


def _cp(*sem):
    return pltpu.CompilerParams(dimension_semantics=sem, vmem_limit_bytes=VMEM_LIMIT)


def _dot(a, b):
    return jnp.dot(a, b, preferred_element_type=F32)


def _dot_nt(a, b):
    return lax.dot_general(a, b, (((1,), (1,)), ((), ())), preferred_element_type=F32)


def _rms(x, g):
    return x * lax.rsqrt(jnp.mean(x * x, axis=-1, keepdims=True) + RMS_EPS) * g


def _msoftmax(s, mask, axis=-1):
    s = jnp.where(mask, s, NEG)
    e = jnp.where(mask, jnp.exp(s - jnp.max(s, axis=axis, keepdims=True)), 0.0)
    return e / jnp.maximum(jnp.sum(e, axis=axis, keepdims=True), 1e-30)


def _is_power_of_two(x):
    return math.frexp(x)[0] == 0.5


def _full(shape):
    n = len(shape)
    return pl.BlockSpec(shape, lambda *_: (0,) * n)


def _at_layer(stacked, layer):
    n = stacked.ndim - 1
    return pl.BlockSpec((None,) + stacked.shape[1:], lambda *_: (layer,) + (0,) * n)


def _lru_body(xn_ref, w_ref, cs_ref, h0_ref, cw_ref, cb_ref, wa_ref, ba_ref, wx_ref, bx_ref, lam_ref,
              oa_ref, cn_ref, hl_ref, xbuf, hc, *, nb, tch, conv_w, starts_at_zero):
    c = pl.program_id(1)
    rows = nb * tch
    wd = hc.shape[-1]
    nst = (conv_w - 1) * nb
    cr = xbuf.shape[0] - rows
    nh = wa_ref.shape[0]
    bw = wd // nh

    @pl.when(c == 0)
    def _():
        if cr > nst:
            xbuf[0:cr - nst, :] = jnp.zeros((cr - nst, wd), F32)
        xbuf[cr - nst:cr, :] = cs_ref[...]
        hc[...] = h0_ref[...]

    z = _dot(xn_ref[...], w_ref[...])
    gate = z[:, wd:]
    xbuf[cr:cr + rows, :] = z[:, :wd]
    y = cb_ref[...] + xbuf[cr - nst:cr - nst + rows, :] * cw_ref[0:1, :]
    for k in range(1, conv_w):
        st = cr - nst + k * nb
        y = y + xbuf[st:st + rows, :] * cw_ref[k:k + 1, :]
    cn_ref[...] = xbuf[cr + rows - nst:cr + rows, :]
    xbuf[0:cr, :] = xbuf[rows:rows + cr, :]

    yb = y.astype(BF16)

    def block_diag(wr):
        return jnp.concatenate([_dot(yb[:, j * bw:(j + 1) * bw], wr[j]) for j in range(nh)], axis=1)

    r = jax.nn.sigmoid(block_diag(wa_ref) + ba_ref[...])
    i = jax.nn.sigmoid(block_diag(wx_ref) + bx_ref[...])
    lam = lam_ref[...]
    softplus_neg_lam = jnp.maximum(-lam, 0.0) + jnp.log1p(jnp.exp(-jnp.abs(lam)))
    a = jnp.exp((-LRU_C) * r * softplus_neg_lam)
    mult = jnp.sqrt(1.0 - a * a)
    row = lax.broadcasted_iota(jnp.int32, (rows, wd), 0)
    if starts_at_zero:
        mult = jnp.where((row < nb) & (c == 0), 1.0, mult)
    u = mult * (i * y)
    if nb == 1:
        u = u + jnp.where(row < 1, a * hc[...], 0.0)
    else:
        u = jnp.concatenate([u[:nb] + a[:nb] * hc[...], u[nb:]], axis=0)
    d = nb
    while d < rows:
        a_sh = jnp.where(row < d, 1.0, pltpu.roll(a, d, axis=0))
        u_sh = jnp.where(row < d, 0.0, pltpu.roll(u, d, axis=0))
        u = a * u_sh + u
        a = a * a_sh
        d *= 2
    hc[...] = u[rows - nb:rows, :]
    hl_ref[...] = u[rows - nb:rows, :]
    oa_ref[...] = (jax.nn.gelu(gate) * u).astype(oa_ref.dtype)


def _lru_call(xn, w, cs, h0, cw, cb, wa, ba, wx, bx, lam, *, layer, nb, tch, starts_at_zero):
    nbatch, rows_total, dm = xn.shape
    wd = h0.shape[-1]
    conv_w = cw.shape[0]
    nst = (conv_w - 1) * nb
    rows = nb * tch
    cr = -(-nst // 8) * 8
    assert rows_total % rows == 0 and rows >= cr
    body = functools.partial(_lru_body, nb=nb, tch=tch, conv_w=conv_w, starts_at_zero=starts_at_zero)
    return pl.pallas_call(
        body, name="lru",
        grid=(nbatch, rows_total // rows),
        in_specs=[pl.BlockSpec((None, rows, dm), lambda b, c: (b, c, 0)),
                  _at_layer(w, layer),
                  pl.BlockSpec((None, nst, wd), lambda b, c: (b, 0, 0)),
                  pl.BlockSpec((None, nb, wd), lambda b, c: (b, 0, 0)),
                  _full(cw.shape), _full(cb.shape), _at_layer(wa, layer), _full(ba.shape), _at_layer(wx, layer),
                  _full(bx.shape), _full(lam.shape)],
        out_specs=[pl.BlockSpec((None, rows, wd), lambda b, c: (b, c, 0)),
                   pl.BlockSpec((None, nst, wd), lambda b, c: (b, 0, 0)),
                   pl.BlockSpec((None, nb, wd), lambda b, c: (b, 0, 0))],
        out_shape=[jax.ShapeDtypeStruct((nbatch, rows_total, wd), BF16),
                   jax.ShapeDtypeStruct((nbatch, nst, wd), F32),
                   jax.ShapeDtypeStruct((nbatch, nb, wd), F32)],
        scratch_shapes=[pltpu.VMEM((cr + rows, wd), F32), pltpu.VMEM((nb, wd), F32)],
        compiler_params=_cp("arbitrary", "arbitrary"),
    )(xn, w, cs, h0, cw, cb, wa, ba, wx, bx, lam)


def _gmlp_body(xn_ref, w_ref, gg_ref, ws_ref, bs_ref, o_ref, *, chunk):
    tm = xn_ref.shape[0]
    ge = jax.nn.gelu(_dot(xn_ref[...], w_ref[...]))
    wd = ge.shape[1] // 2
    u = ge[:, :wd]
    vb = _rms(ge[:, wd:], gg_ref[...]).astype(BF16)
    ng = ws_ref.shape[0]
    gc = wd // ng
    tril = lax.broadcasted_iota(jnp.int32, (chunk, chunk), 0) >= lax.broadcasted_iota(jnp.int32, (chunk, chunk), 1)
    wsm = [jnp.where(tril, ws_ref[gi], 0.0).astype(BF16) for gi in range(ng)]
    for ch in range(tm // chunk):
        lo, hi = ch * chunk, (ch + 1) * chunk
        mixed = jnp.concatenate([_dot(wsm[gi], vb[lo:hi, gi * gc:(gi + 1) * gc]) for gi in range(ng)], axis=1)
        o_ref[lo:hi, :] = (u[lo:hi] * (mixed + bs_ref[...])).astype(o_ref.dtype)


def _gmlp_call(xn, w, gg, ws, bs_full, *, layer, tm):
    t, dm = xn.shape
    wd = gg.shape[-1]
    chunk = ws.shape[-1]
    assert t % tm == 0 and tm % chunk == 0
    return pl.pallas_call(
        functools.partial(_gmlp_body, chunk=chunk), name="gmlp",
        grid=(t // tm,),
        in_specs=[pl.BlockSpec((tm, dm), lambda i: (i, 0)), _at_layer(w, layer), _full(gg.shape),
                  _at_layer(ws, layer), _full(bs_full.shape)],
        out_specs=pl.BlockSpec((tm, wd), lambda i: (i, 0)),
        out_shape=jax.ShapeDtypeStruct((t, wd), BF16),
        compiler_params=_cp("parallel"),
    )(xn, w, gg, ws, bs_full)


def _gmlp_short_body(xn_ref, w_ref, gg_ref, wexp_ref, bexp_ref, o_ref, v_ref, *, n, nb):
    ge = jax.nn.gelu(_dot(xn_ref[...], w_ref[...]))
    wd = ge.shape[1] // 2
    u = ge[:, :wd]
    v = _rms(ge[:, wd:], gg_ref[...])
    v_ref[...] = v
    for t in range(n):
        m = bexp_ref[t:t + 1, :] + wexp_ref[t * n:t * n + 1, :] * v[0:nb]
        for s in range(1, t + 1):
            m = m + wexp_ref[t * n + s:t * n + s + 1, :] * v[s * nb:(s + 1) * nb]
        o_ref[t * nb:(t + 1) * nb, :] = (u[t * nb:(t + 1) * nb] * m).astype(o_ref.dtype)


def _gmlp_short_call(xn, w, gg, wexp, bexp, *, layer, n, nb):
    t, dm = xn.shape
    wd = gg.shape[-1]
    return pl.pallas_call(
        functools.partial(_gmlp_short_body, n=n, nb=nb), name="gmlp_short",
        grid=(1,),
        in_specs=[_full(xn.shape), _at_layer(w, layer), _full(gg.shape), _full(wexp.shape),
                  _full(bexp.shape)],
        out_specs=[_full((t, wd)), _full((t, wd))],
        out_shape=[jax.ShapeDtypeStruct((t, wd), BF16), jax.ShapeDtypeStruct((t, wd), F32)],
        compiler_params=_cp("arbitrary"),
    )(xn, w, gg, wexp, bexp)


def _qkv_prompt_body(xn_ref, w_ref, wck_ref, wcv_ref, *refs, qw, kvw, dh, kt, q_scale, n_alias):
    (q_ref, gt_ref, kcmp_ref, vcmp_ref, ka_ref, vst_ref, kwg_ref, vwt_ref) = refs[n_alias:n_alias + 8]
    stack_refs = refs[n_alias + 8:]
    tm = xn_ref.shape[0]
    ng = kvw // dh
    z = _dot(xn_ref[...], w_ref[...])
    parts = [z[:, qw + j * kvw:qw + (j + 1) * kvw] for j in range(6)]
    parts_t = [p.T for p in parts]
    for ref, p in zip(stack_refs, parts_t):
        ref[...] = p
    q_ref[...] = (z[:, :qw] * q_scale).T.astype(BF16)
    gt_ref[...] = jax.nn.sigmoid(z[:, qw + 6 * kvw:]).T
    nbt = tm // NSA_BLOCK
    kcmp_ref[...] = jnp.sum(parts[0].reshape(nbt, NSA_BLOCK, kvw) * wck_ref[...][None], axis=1)
    vcmp_ref[...] = jnp.sum(parts[1].reshape(nbt, NSA_BLOCK, kvw) * wcv_ref[...][None], axis=1)
    bpt = kt // NSA_BLOCK
    pos0 = lax.rem(pl.program_id(0) * tm, kt)
    blk = (pos0 + lax.broadcasted_iota(jnp.int32, (tm, bpt), 0)) // NSA_BLOCK
    onehot = (blk == lax.broadcasted_iota(jnp.int32, (tm, bpt), 1)).astype(BF16)
    ones_row = (lax.broadcasted_iota(jnp.int32, (BF16_ROWS, tm), 0) == 0).astype(BF16)
    ksb = parts[2].astype(BF16)
    kwb = parts[4].astype(BF16)
    vst = parts_t[3].astype(BF16)
    vwt = parts_t[5].astype(BF16)
    for gi in range(ng):
        ka_ref[gi, :, 0:dh] = ksb[:, gi * dh:(gi + 1) * dh]
        ka_ref[gi, :, dh:dh + bpt] = onehot
        kwg_ref[gi] = kwb[:, gi * dh:(gi + 1) * dh]
        for ref, v in ((vst_ref, vst), (vwt_ref, vwt)):
            ref[gi, 0:dh, :] = v[gi * dh:(gi + 1) * dh, :]
            ref[gi, dh:dh + BF16_ROWS, :] = ones_row


def _qkv_prompt_call(xn, w, wck, wcv, stacks, *, layer, depth, nbatch, seq, tm, qw, kvw, dh):
    t, dm = xn.shape
    ng = kvw // dh
    kt = min(KV_TILE, seq)
    bpt = kt // NSA_BLOCK
    nbt = tm // NSA_BLOCK
    nst = seq // tm
    assert t % tm == 0 and tm % (8 * NSA_BLOCK) == 0 and seq % tm == 0 and kt % tm == 0 and tm % LANES == 0
    row = lambda n: pl.BlockSpec((tm, n), lambda i: (i, 0))
    col = lambda n: pl.BlockSpec((n, tm), lambda i: (0, i))
    stack_spec = pl.BlockSpec((None, None, kvw, tm), lambda i: (layer, i // nst, 0, i % nst))
    stack_shape = jax.ShapeDtypeStruct((depth, nbatch, kvw, seq), F32)
    out_specs = [col(qw), col(LANES), pl.BlockSpec((nbt, kvw), lambda i: (i, 0)), pl.BlockSpec((nbt, kvw), lambda i: (i, 0)),
                 pl.BlockSpec((ng, tm, dh + bpt), lambda i: (0, i, 0)),
                 pl.BlockSpec((ng, dh + BF16_ROWS, tm), lambda i: (0, 0, i)),
                 pl.BlockSpec((ng, tm, dh), lambda i: (0, i, 0)),
                 pl.BlockSpec((ng, dh + BF16_ROWS, tm), lambda i: (0, 0, i))] + [stack_spec] * 6
    out_shape = [jax.ShapeDtypeStruct((qw, t), BF16), jax.ShapeDtypeStruct((LANES, t), F32),
                 jax.ShapeDtypeStruct((t // NSA_BLOCK, kvw), F32), jax.ShapeDtypeStruct((t // NSA_BLOCK, kvw), F32),
                 jax.ShapeDtypeStruct((ng, t, dh + bpt), BF16), jax.ShapeDtypeStruct((ng, dh + BF16_ROWS, t), BF16),
                 jax.ShapeDtypeStruct((ng, t, dh), BF16),
                 jax.ShapeDtypeStruct((ng, dh + BF16_ROWS, t), BF16)] + [stack_shape] * 6
    scale = dh ** -0.5
    q_scale = scale if _is_power_of_two(scale) else 1.0
    n_alias = len(stacks)
    body = functools.partial(_qkv_prompt_body, qw=qw, kvw=kvw, dh=dh, kt=kt, q_scale=q_scale, n_alias=n_alias)
    return pl.pallas_call(
        body, name="qkv_prompt",
        grid=(t // tm,),
        in_specs=[row(dm), _at_layer(w, layer), _full(wck.shape), _full(wcv.shape)]
        + [pl.BlockSpec(memory_space=pl.ANY)] * n_alias,
        out_specs=out_specs, out_shape=out_shape,
        input_output_aliases={4 + k: 8 + k for k in range(n_alias)},
        compiler_params=_cp("parallel"),
    )(xn, w, wck, wcv, *stacks)


def _qkv_sample_body(xn_ref, w_ref, q_ref, kc_ref, vc_ref, ks_ref, vs_ref, kw_ref, vw_ref, gt_ref,
                     *, qw, kvw, dh):
    z = _dot(xn_ref[...], w_ref[...])
    for j, ref in enumerate((kc_ref, vc_ref, ks_ref, vs_ref, kw_ref, vw_ref)):
        for gi in range(kvw // dh):
            ref[:, gi, :] = z[:, qw + j * kvw + gi * dh:qw + j * kvw + (gi + 1) * dh]
    q_ref[...] = z[:, :qw].astype(BF16)
    gt_ref[...] = jax.nn.sigmoid(z[:, qw + 6 * kvw:])


def _qkv_sample_call(xn, w, *, layer, qw, kvw, dh):
    t, dm = xn.shape
    ng = kvw // dh
    return pl.pallas_call(
        functools.partial(_qkv_sample_body, qw=qw, kvw=kvw, dh=dh), name="qkv_sample",
        grid=(1,),
        in_specs=[_full(xn.shape), _at_layer(w, layer)],
        out_specs=[_full((t, qw))] + [_full((t, ng, dh))] * 6 + [_full((t, LANES))],
        out_shape=[jax.ShapeDtypeStruct((t, qw), BF16)] + [jax.ShapeDtypeStruct((t, ng, dh), F32)] * 6
        + [jax.ShapeDtypeStruct((t, LANES), F32)],
        compiler_params=_cp("arbitrary"),
    )(xn, w)


def _select_blocks(score, blk, nblk, count, axis):
    for _ in range(count):
        m = jnp.max(score, axis=axis, keepdims=True)
        idx = jnp.min(jnp.where(score == m, blk, nblk), axis=axis, keepdims=True)
        score = jnp.where(blk == idx, M_INIT, score)
    return score == M_INIT


def _nsa_cw_body(qt_ref, gt_ref, kc_ref, vct_ref, kw_ref, vwt_ref, ocw_ref, selb_ref,
                 *, seq, ng, rep, dh, s_scale):
    c = pl.program_id(1)
    qb = qt_ref.shape[1]
    nblk = kc_ref.shape[0]
    band = min(NSA_WINDOW + qb, seq)
    st = pl.multiple_of(jnp.maximum(c * qb - NSA_WINDOW, 0), qb)
    cols = rep * qb
    t_q = c * qb + lax.broadcasted_iota(jnp.int32, (1, qb), 1)
    t_c = jnp.concatenate([t_q] * rep, axis=1)
    ok_c = (lax.broadcasted_iota(jnp.int32, (nblk, cols), 0) + 1) * NSA_BLOCK - 1 <= t_c
    kp = st + lax.broadcasted_iota(jnp.int32, (band, cols), 0)
    ok_w = (kp <= t_c) & (kp > t_c - NSA_WINDOW)
    blk = lax.broadcasted_iota(jnp.int32, (nblk, qb), 0)
    cur = t_q // NSA_BLOCK
    for gi in range(ng):
        heads = [gi * rep + r for r in range(rep)]
        lanes = slice(gi * dh, (gi + 1) * dh)
        qg = jnp.concatenate([qt_ref[h * dh:(h + 1) * dh, :] for h in heads], axis=1)
        s_c = _dot(kc_ref[:, lanes].astype(BF16), qg)
        if s_scale != 1.0:
            s_c = s_c * s_scale
        p_c = _msoftmax(s_c, ok_c, axis=0)
        o_c = _dot(vct_ref[lanes, :].astype(BF16), p_c.astype(BF16))
        imp = p_c[:, 0:qb]
        for r in range(1, rep):
            imp = imp + p_c[:, r * qb:(r + 1) * qb]
        score = jnp.where(blk == cur, SEL_FORCE, jnp.where(blk < cur, imp, -1.0))
        sel = _select_blocks(score, blk, nblk, min(NSA_TOPN, nblk), axis=0)
        selb_ref[gi] = jnp.where(sel, 0.0, NEG).astype(BF16)
        s_w = _dot(kw_ref[gi, pl.ds(st, band), :], qg)
        if s_scale != 1.0:
            s_w = s_w * s_scale
        s_w = jnp.where(ok_w, s_w, NEG)
        e_w = jnp.exp(s_w - jnp.max(s_w, axis=0, keepdims=True))
        ov = _dot(vwt_ref[gi, :, pl.ds(st, band)], e_w.astype(BF16))
        o_w = ov[0:dh] / ov[dh:dh + 1]
        g_c = jnp.concatenate([gt_ref[3 * h:3 * h + 1, :] for h in heads], axis=1)
        g_w = jnp.concatenate([gt_ref[3 * h + 2:3 * h + 3, :] for h in heads], axis=1)
        ocw = g_c * o_c + g_w * o_w
        for r, h in enumerate(heads):
            ocw_ref[h * dh:(h + 1) * dh, :] = ocw[:, r * qb:(r + 1) * qb]


def _nsa_cw_call(qt, gates_t, kcmp, vcmp_t, kwg, vwt, *, nbatch, seq, ng, dh, s_scale):
    qw, t = qt.shape
    kvw = ng * dh
    rep = qw // kvw
    qb = min(Q_BLOCK, seq)
    nq = seq // qb
    nblk = seq // NSA_BLOCK
    body = functools.partial(_nsa_cw_body, seq=seq, ng=ng, rep=rep, dh=dh, s_scale=s_scale)
    qcol = lambda n: pl.BlockSpec((n, qb), lambda b, c: (0, b * nq + c))
    return pl.pallas_call(
        body, name="nsa_cmp_win",
        grid=(nbatch, nq),
        in_specs=[qcol(qw), qcol(LANES),
                  pl.BlockSpec((nblk, kvw), lambda b, c: (b, 0)), pl.BlockSpec((None, kvw, nblk), lambda b, c: (b, 0, 0)),
                  pl.BlockSpec((ng, seq, dh), lambda b, c: (0, b, 0)),
                  pl.BlockSpec((ng, dh + BF16_ROWS, seq), lambda b, c: (0, 0, b))],
        out_specs=[qcol(qw), pl.BlockSpec((ng, None, nblk, qb), lambda b, c: (0, b * nq + c, 0, 0))],
        out_shape=[jax.ShapeDtypeStruct((qw, t), F32), jax.ShapeDtypeStruct((ng, nbatch * nq, nblk, qb), BF16)],
        compiler_params=_cp("parallel", "arbitrary"),
    )(qt, gates_t, kcmp, vcmp_t, kwg, vwt)


def _nsa_sel_body(qt_ref, selb_ref, ka_ref, vst_ref, ocw_ref, gt_ref, o_ref, m_sc, acc_sc, sa_sc, sb_sc, ma_sc, mb_sc,
                  *, rep, dh, kt, s_scale):
    g = pl.program_id(1)
    c = pl.program_id(2)
    qb = qt_ref.shape[1]
    cols = rep * qb
    bpt = kt // NSA_BLOCK
    qg = jnp.concatenate([qt_ref[r * dh:(r + 1) * dh, :] for r in range(rep)], axis=1)
    m_sc[...] = jnp.full(m_sc.shape, M_INIT, F32)
    acc_sc[...] = jnp.zeros(acc_sc.shape, F32)

    def scores(j, s_ref, mt_ref):
        st = pl.multiple_of(j * kt, kt)
        sb = selb_ref[pl.ds(pl.multiple_of(j * bpt, bpt), bpt), :]
        qa = jnp.concatenate([qg, jnp.concatenate([sb] * rep, axis=1)], axis=0)
        s = _dot(ka_ref[pl.ds(st, kt), :], qa)
        if s_scale != 1.0:
            s = s * s_scale
        s_ref[...] = s
        mt_ref[...] = jnp.max(s, axis=0, keepdims=True)

    def absorb(j, s, m_tile):
        st = pl.multiple_of(j * kt, kt)
        m_old = m_sc[...]
        m_new = jnp.maximum(m_old, m_tile)
        p = jnp.exp(s - m_new)
        acc_sc[...] = jnp.exp(m_old - m_new) * acc_sc[...] + _dot(vst_ref[:, pl.ds(st, kt)], p.astype(BF16))
        m_sc[...] = m_new

    def finish(s_ref):
        st = pl.multiple_of(n_full * kt, kt)
        kp = st + lax.broadcasted_iota(jnp.int32, (kt, cols), 0)
        t_c = c * qb + (lax.broadcasted_iota(jnp.int32, (kt, cols), 1) & (qb - 1))
        s = jnp.where(kp <= t_c, s_ref[...], NEG)
        absorb(n_full, s, jnp.max(s, axis=0, keepdims=True))
        acc = acc_sc[...]
        o_s = acc[0:dh] / acc[dh:dh + 1]
        out = []
        for r in range(rep):
            g_s = gt_ref[pl.ds(3 * (g * rep + r) + 1, 1), :]
            out.append(ocw_ref[r * dh:(r + 1) * dh, :] + g_s * o_s[:, r * qb:(r + 1) * qb])
        o_ref[...] = jnp.concatenate(out, axis=0).T.astype(o_ref.dtype)

    n_full = (c * qb) // kt
    scores(0, sa_sc, ma_sc)

    def pair(k, carry):
        scores(2 * k + 1, sb_sc, mb_sc)
        absorb(2 * k, sa_sc[...], ma_sc[...])
        scores(2 * k + 2, sa_sc, ma_sc)
        absorb(2 * k + 1, sb_sc[...], mb_sc[...])
        return carry

    lax.fori_loop(0, n_full // 2, pair, 0)

    @pl.when(n_full % 2 == 1)
    def _():
        scores(n_full, sb_sc, mb_sc)
        absorb(n_full - 1, sa_sc[...], ma_sc[...])
        finish(sb_sc)

    @pl.when(n_full % 2 == 0)
    def _():
        finish(sa_sc)


def _nsa_sel_call(qt, selb, ka, vst, ocw_t, gates_t, *, nbatch, seq, ng, dh, s_scale):
    qw, t = qt.shape
    rep = qw // (ng * dh)
    qb = min(Q_BLOCK, seq)
    assert qb & (qb - 1) == 0
    nq = seq // qb
    nblk = selb.shape[2]
    kt = min(KV_TILE, seq)
    bpt = kt // NSA_BLOCK
    assert seq % kt == 0 and kt % qb == 0 and bpt % BF16_ROWS == 0 and ka.shape[-1] == dh + bpt
    body = functools.partial(_nsa_sel_body, rep=rep, dh=dh, kt=kt, s_scale=s_scale)
    qcol = pl.BlockSpec((rep * dh, qb), lambda b, g, c: (g, b * nq + c))
    return pl.pallas_call(
        body, name="nsa_selected",
        grid=(nbatch, ng, nq),
        in_specs=[qcol,
                  pl.BlockSpec((None, None, nblk, qb), lambda b, g, c: (g, b * nq + c, 0, 0)),
                  pl.BlockSpec((None, seq, dh + bpt), lambda b, g, c: (g, b, 0)),
                  pl.BlockSpec((None, dh + BF16_ROWS, seq), lambda b, g, c: (g, 0, b)),
                  qcol,
                  pl.BlockSpec((LANES, qb), lambda b, g, c: (0, b * nq + c))],
        out_specs=pl.BlockSpec((qb, rep * dh), lambda b, g, c: (b * nq + c, g)),
        out_shape=jax.ShapeDtypeStruct((t, qw), BF16),
        scratch_shapes=[pltpu.VMEM((1, rep * qb), F32), pltpu.VMEM((dh + BF16_ROWS, rep * qb), F32),
                        pltpu.VMEM((kt, rep * qb), F32), pltpu.VMEM((kt, rep * qb), F32),
                        pltpu.VMEM((1, rep * qb), F32), pltpu.VMEM((1, rep * qb), F32)],
        compiler_params=_cp("parallel", "parallel", "arbitrary"),
    )(qt, selb, ka, vst, ocw_t, gates_t)


def _page_specs(pps, kvw, page, layer):
    return [pl.BlockSpec((None, None, kvw, page), lambda b, j, pt, i=i: (layer, pt[b, j * pps + i], 0, 0))
            for i in range(pps)]


def _cmp_pages_body(pt_ref, *refs, pps):
    del pt_ref
    k_refs, v_refs = refs[:pps], refs[pps:2 * pps]
    wk_ref, wv_ref, kc_ref, vc_ref = refs[2 * pps:]
    j = pl.program_id(1)
    kvw, page = k_refs[0].shape
    nblk = kc_ref.shape[1]
    bpp = page // NSA_BLOCK

    @pl.when(j == 0)
    def _():
        kc_ref[...] = jnp.zeros(kc_ref.shape, F32)
        vc_ref[...] = jnp.zeros(vc_ref.shape, F32)

    row_blk = lax.broadcasted_iota(jnp.int32, (page, nblk), 0) // NSA_BLOCK
    col = lax.broadcasted_iota(jnp.int32, (page, nblk), 1)
    for page_refs, w_ref, o_ref in ((k_refs, wk_ref, kc_ref), (v_refs, wv_ref, vc_ref)):
        acc = jnp.zeros((kvw, nblk), F32)
        for i, r in enumerate(page_refs):
            y = r[...] * w_ref[...]
            hi = y.astype(BF16)
            lo = (y - hi.astype(F32)).astype(BF16)
            onehot = (col == (j * pps + i) * bpp + row_blk).astype(BF16)
            acc = acc + _dot(jnp.concatenate([hi, lo], axis=1), jnp.concatenate([onehot, onehot], axis=0))
        o_ref[...] += acc


def _cmp_pages_call(page_table, pool_k, pool_v, wk_t, wv_t, *, layer, pps):
    nbs, npg = page_table.shape
    _, _, kvw, page = pool_k.shape
    nblk = npg * (page // NSA_BLOCK)
    assert npg % pps == 0
    wspec = pl.BlockSpec((kvw, page), lambda b, j, pt: (0, 0))
    grid_spec = pltpu.PrefetchScalarGridSpec(
        num_scalar_prefetch=1, grid=(nbs, npg // pps),
        in_specs=_page_specs(pps, kvw, page, layer) * 2 + [wspec] * 2,
        out_specs=[pl.BlockSpec((None, kvw, nblk), lambda b, j, pt: (b, 0, 0))] * 2)
    return pl.pallas_call(
        functools.partial(_cmp_pages_body, pps=pps), name="sample_cmp_pages",
        grid_spec=grid_spec,
        out_shape=[jax.ShapeDtypeStruct((nbs, kvw, nblk), F32)] * 2,
        compiler_params=_cp("parallel", "arbitrary"),
    )(page_table, *([pool_k] * pps), *([pool_v] * pps), wk_t, wv_t)


def _samp_cw_body(q_ref, g_ref, kct_ref, vct_ref, wk_ref, wv_ref, kn_ref, vn_ref, *refs,
                  ts, past, rep, scale, n_alias):
    ocw_ref, selb_ref, wko_ref, wvo_ref = refs[n_alias:]
    rows = q_ref.shape[0]
    kvw, nblk = kct_ref.shape
    wb = wk_ref.shape[1]
    npad = kn_ref.shape[1]
    gt = rows // rep
    q = q_ref[...]
    ti = lax.rem(lax.broadcasted_iota(jnp.int32, (rows, 1), 0), ts)
    t = past + ti
    blk_r = lax.broadcasted_iota(jnp.int32, (rows, nblk), 1)
    p_c = _msoftmax(_dot(q, kct_ref[...].astype(BF16)) * scale, (blk_r + 1) * NSA_BLOCK - 1 <= t)
    o_c = _dot_nt(p_c.astype(BF16), vct_ref[...].astype(BF16))
    imp = p_c[0:gt]
    for r in range(1, rep):
        imp = imp + p_c[r * gt:(r + 1) * gt]
    blk = lax.broadcasted_iota(jnp.int32, (gt, nblk), 1)
    sel = _select_blocks(imp, blk, nblk, min(NSA_TOPN - 1, nblk), axis=1)
    selb_ref[...] = jnp.concatenate([jnp.where(sel, 0.0, NEG)] * rep, axis=0)
    wk, wv, kn, vn = wk_ref[...], wv_ref[...], kn_ref[...], vn_ref[...]
    kp1 = past - wb + lax.broadcasted_iota(jnp.int32, (rows, wb), 1)
    j2 = lax.broadcasted_iota(jnp.int32, (rows, npad), 1)
    ok1 = (kp1 <= t) & (kp1 > t - NSA_WINDOW) & (kp1 >= 0)
    ok2 = (j2 <= ti) & (j2 < ts)
    s1 = jnp.where(ok1, _dot(q, wk.astype(BF16)) * scale, NEG)
    s2 = jnp.where(ok2, _dot(q, kn.astype(BF16)) * scale, NEG)
    m = jnp.maximum(jnp.max(s1, axis=1, keepdims=True), jnp.max(s2, axis=1, keepdims=True))
    e1 = jnp.where(ok1, jnp.exp(s1 - m), 0.0)
    e2 = jnp.where(ok2, jnp.exp(s2 - m), 0.0)
    den = jnp.maximum(jnp.sum(e1, axis=1, keepdims=True) + jnp.sum(e2, axis=1, keepdims=True), 1e-30)
    o_w = _dot_nt((e1 / den).astype(BF16), wv.astype(BF16)) + _dot_nt((e2 / den).astype(BF16), vn.astype(BF16))
    g = g_ref[...]
    ocw_ref[...] = g[:, 0:1] * o_c + g[:, 2:3] * o_w
    lane = lax.broadcasted_iota(jnp.int32, (kvw, wb), 1)

    def advance(old, new):
        tail = jnp.concatenate([jnp.zeros((kvw, wb - npad), F32), pltpu.roll(new, npad - ts, axis=1)], axis=1)
        return jnp.where(lane >= wb - ts, tail, pltpu.roll(old, wb - ts, axis=1))

    wko_ref[...] = advance(wk, kn)
    wvo_ref[...] = advance(wv, vn)


def _samp_cw_call(q_bd, gall, kct, vct, win_k, win_v, kn_t, vn_t, stacks, *, layer, depth, ts, past, rep, dh):
    nbs, rows, kvw = q_bd.shape
    nblk = kct.shape[2]
    wb = win_k.shape[3]
    npad = kn_t.shape[2]
    assert wb >= npad
    per_b = lambda *s: pl.BlockSpec((None,) + s, lambda b: (b,) + (0,) * len(s))
    win = pl.BlockSpec((None, None, kvw, wb), lambda b: (layer, b, 0, 0))
    n_alias = len(stacks)
    body = functools.partial(_samp_cw_body, ts=ts, past=past, rep=rep, scale=dh ** -0.5, n_alias=n_alias)
    return pl.pallas_call(
        body, name="sample_cmp_win",
        grid=(nbs,),
        in_specs=[per_b(rows, kvw), per_b(rows, LANES), per_b(kvw, nblk), per_b(kvw, nblk), win, win,
                  per_b(kvw, npad), per_b(kvw, npad)] + [pl.BlockSpec(memory_space=pl.ANY)] * n_alias,
        out_specs=[per_b(rows, kvw), per_b(rows, nblk), win, win],
        out_shape=[jax.ShapeDtypeStruct((nbs, rows, kvw), F32), jax.ShapeDtypeStruct((nbs, rows, nblk), F32),
                   jax.ShapeDtypeStruct((depth, nbs, kvw, wb), F32), jax.ShapeDtypeStruct((depth, nbs, kvw, wb), F32)],
        input_output_aliases={8 + k: 2 + k for k in range(n_alias)},
        compiler_params=_cp("parallel"),
    )(q_bd, gall, kct, vct, win_k, win_v, kn_t, vn_t, *stacks)


def _samp_sel_body(pt_ref, *refs, pps, ts, scale):
    del pt_ref
    k_refs, v_refs = refs[:pps], refs[pps:2 * pps]
    q_ref, selb_ref, kn_ref, vn_ref, ocw_ref, g_ref, o_ref, m_sc, l_sc, acc_sc = refs[2 * pps:]
    j = pl.program_id(1)
    rows = q_ref.shape[0]
    nblk = selb_ref.shape[1]
    page = k_refs[0].shape[1]
    kt = pps * page
    q = q_ref[...]

    @pl.when(j == 0)
    def _():
        m_sc[...] = jnp.full(m_sc.shape, M_INIT, F32)
        l_sc[...] = jnp.zeros(l_sc.shape, F32)
        acc_sc[...] = jnp.zeros(acc_sc.shape, F32)

    def update(s, v_t):
        m_old = m_sc[...]
        m_new = jnp.maximum(m_old, jnp.max(s, axis=1, keepdims=True))
        alpha = jnp.exp(m_old - m_new)
        p = jnp.exp(s - m_new)
        l_sc[...] = alpha * l_sc[...] + jnp.sum(p, axis=1, keepdims=True)
        acc_sc[...] = alpha * acc_sc[...] + _dot_nt(p.astype(BF16), v_t)
        m_sc[...] = m_new

    k_t = jnp.concatenate([r[...] for r in k_refs], axis=1).astype(BF16)
    v_t = jnp.concatenate([r[...] for r in v_refs], axis=1).astype(BF16)
    key_blk = j * (kt // NSA_BLOCK) + lax.broadcasted_iota(jnp.int32, (nblk, kt), 1) // NSA_BLOCK
    onehot = (key_blk == lax.broadcasted_iota(jnp.int32, (nblk, kt), 0)).astype(BF16)
    update(_dot(q, k_t) * scale + _dot(selb_ref[...].astype(BF16), onehot), v_t)

    @pl.when(j == pl.num_programs(1) - 1)
    def _():
        npad = kn_ref.shape[1]
        ti = lax.rem(lax.broadcasted_iota(jnp.int32, (rows, npad), 0), ts)
        j2 = lax.broadcasted_iota(jnp.int32, (rows, npad), 1)
        s = jnp.where((j2 <= ti) & (j2 < ts), _dot(q, kn_ref[...].astype(BF16)) * scale, NEG)
        update(s, vn_ref[...].astype(BF16))
        o_ref[...] = ocw_ref[...] + g_ref[:, 1:2] * (acc_sc[...] / l_sc[...])


def _samp_sel_call(page_table, pool_k, pool_v, q_bd, selb, kn_t, vn_t, ocw, gall, *, layer, pps, ts, dh):
    nbs, npg = page_table.shape
    _, _, kvw, page = pool_k.shape
    rows = q_bd.shape[1]
    per_b = lambda *s: pl.BlockSpec((None,) + s, lambda b, j, pt: (b,) + (0,) * len(s))
    npad = kn_t.shape[2]
    grid_spec = pltpu.PrefetchScalarGridSpec(
        num_scalar_prefetch=1, grid=(nbs, npg // pps),
        in_specs=_page_specs(pps, kvw, page, layer) * 2 + [
            per_b(rows, kvw), per_b(rows, selb.shape[2]), per_b(kvw, npad), per_b(kvw, npad),
            per_b(rows, kvw), per_b(rows, LANES)],
        out_specs=per_b(rows, kvw),
        scratch_shapes=[pltpu.VMEM((rows, 1), F32), pltpu.VMEM((rows, 1), F32), pltpu.VMEM((rows, kvw), F32)])
    body = functools.partial(_samp_sel_body, pps=pps, ts=ts, scale=dh ** -0.5)
    return pl.pallas_call(
        body, name="sample_selected",
        grid_spec=grid_spec,
        out_shape=jax.ShapeDtypeStruct((nbs, rows, kvw), F32),
        compiler_params=_cp("parallel", "arbitrary"),
    )(page_table, *([pool_k] * pps), *([pool_v] * pps), q_bd, selb, kn_t, vn_t, ocw, gall)


def _norm_body(x_ref, g_ref, o_ref):
    o_ref[...] = _rms(x_ref[...], g_ref[...]).astype(o_ref.dtype)


def _norm_call(x, g, *, tm):
    t, dm = x.shape
    row = pl.BlockSpec((tm, dm), lambda i: (i, 0))
    return pl.pallas_call(
        _norm_body, name="norm", grid=(t // tm,), in_specs=[row, _full(g.shape)], out_specs=row,
        out_shape=jax.ShapeDtypeStruct((t, dm), BF16), compiler_params=_cp("parallel"),
    )(x, g)


def _merge_body(x_ref, xn_ref, wmg_ref, oa_ref, ob_ref, oc_ref, wb_ref, wo_ref, gn_ref, o_ref, on_ref):
    x = x_ref[...]
    dm = x.shape[1]
    mg = jax.nn.sigmoid(_dot(xn_ref[...], wmg_ref[...]))
    y = mg[:, 0:dm] * _dot(oa_ref[...], wb_ref[0])
    y = y + mg[:, dm:2 * dm] * _dot(ob_ref[...], wb_ref[1])
    y = y + mg[:, 2 * dm:3 * dm] * _dot(oc_ref[...], wb_ref[2])
    x_new = x + _dot(y.astype(BF16), wo_ref[...])
    o_ref[...] = x_new
    on_ref[...] = _rms(x_new, gn_ref[...]).astype(BF16)


def _merge_call(x, xn, wmg, oa, ob, oc, wb, wo, g_next, *, layer, tm):
    t, dm = x.shape
    row = pl.BlockSpec((tm, dm), lambda i: (i, 0))
    return pl.pallas_call(
        _merge_body, name="merge",
        grid=(t // tm,),
        in_specs=[row, row, _at_layer(wmg, layer), row, row, row, _at_layer(wb, layer),
                  _at_layer(wo, layer), _full(g_next.shape)],
        out_specs=[row, row],
        out_shape=[jax.ShapeDtypeStruct((t, dm), F32), jax.ShapeDtypeStruct((t, dm), BF16)],
        compiler_params=_cp("parallel"),
    )(x, xn, wmg, oa, ob, oc, wb, wo, g_next)


def _mixer_outputs(y, gn_ref, o_ref, on_ref, final):
    yn = _rms(y, gn_ref[...])
    o_ref[...] = yn if final else y
    on_ref[...] = yn.astype(BF16)


def _ffn_body(x_ref, xn_ref, w1_ref, w3_ref, w2_ref, gn_ref, o_ref, on_ref, acc_sc, *, final):
    f = pl.program_id(1)

    @pl.when(f == 0)
    def _():
        acc_sc[...] = jnp.zeros(acc_sc.shape, F32)

    xn = xn_ref[...]
    h = jax.nn.silu(_dot(xn, w1_ref[...])) * _dot(xn, w3_ref[...])
    acc_sc[...] += _dot(h.astype(BF16), w2_ref[...])

    @pl.when(f == pl.num_programs(1) - 1)
    def _():
        _mixer_outputs(x_ref[...] + acc_sc[...], gn_ref, o_ref, on_ref, final)


def _ffn_tile(f_dim):
    half = f_dim // 2
    return half if f_dim % 2 == 0 and half % LANES == 0 else f_dim


def _ffn_call(x, xn, w1, w3, w2, g_next, *, layer, tm, final):
    t, dm = x.shape
    fd = w1.shape[-1]
    ft = _ffn_tile(fd)
    row = pl.BlockSpec((tm, dm), lambda i, f: (i, 0))
    vec = pl.BlockSpec((1, dm), lambda i, f: (0, 0))
    w_in_spec = pl.BlockSpec((None, dm, ft), lambda i, f: (layer, 0, f))
    return pl.pallas_call(
        functools.partial(_ffn_body, final=final), name="ffn",
        grid=(t // tm, fd // ft),
        in_specs=[row, row, w_in_spec, w_in_spec, pl.BlockSpec((None, ft, dm), lambda i, f: (layer, f, 0)), vec],
        out_specs=[row, row],
        out_shape=[jax.ShapeDtypeStruct((t, dm), F32), jax.ShapeDtypeStruct((t, dm), BF16)],
        scratch_shapes=[pltpu.VMEM((tm, dm), F32)],
        compiler_params=_cp("parallel", "arbitrary"),
    )(x, xn, w1, w3, w2, g_next)


def _route_body(x_ref, g_ref, wr_ref, br_ref, xn_ref, comb_ref, key_ref, keyt_ref, cnt_ref):
    tm = x_ref.shape[0]
    xn = _rms(x_ref[...], g_ref[...])
    xn_ref[...] = xn.astype(BF16)
    logits = jnp.dot(xn, wr_ref[...], preferred_element_type=F32, precision=lax.Precision.HIGHEST) + br_ref[...]
    lane = lax.broadcasted_iota(jnp.int32, logits.shape, 1)
    nl = logits.shape[1]
    m1 = jnp.max(logits, axis=1, keepdims=True)
    i1 = jnp.min(jnp.where(logits == m1, lane, nl), axis=1, keepdims=True)
    rest = jnp.where(lane == i1, M_INIT, logits)
    m2 = jnp.max(rest, axis=1, keepdims=True)
    i2 = jnp.min(jnp.where(rest == m2, lane, nl), axis=1, keepdims=True)
    e2 = jnp.exp(m2 - m1)
    den = 1.0 + e2
    comb_ref[...] = jnp.where(lane == i1, 1.0 / den, 0.0) + jnp.where(lane == i2, e2 / den, 0.0)
    member = (lane == i1) | (lane == i2)
    earlier = lax.broadcasted_iota(jnp.int32, (tm, tm), 1) < lax.broadcasted_iota(jnp.int32, (tm, tm), 0)
    rank = _dot(earlier.astype(BF16), member.astype(BF16))
    key = jnp.where(member, rank, -1.0)
    key_ref[...] = key
    keyt_ref[...] = key.T
    count = jnp.sum(member.astype(F32), axis=0, keepdims=True).astype(jnp.int32)
    cnt_ref[...] = jnp.broadcast_to(count, cnt_ref.shape)


def _route_call(x, g, wr, br, *, tm):
    t, dm = x.shape
    nl = wr.shape[1]
    nt = t // tm
    row = lambda n: pl.BlockSpec((tm, n), lambda i: (i, 0))
    return pl.pallas_call(
        _route_body, name="moe_route",
        grid=(nt,),
        in_specs=[row(dm), _full(g.shape), _full(wr.shape), _full(br.shape)],
        out_specs=[row(dm), row(nl), row(nl), pl.BlockSpec((None, nl, tm), lambda i: (i, 0, 0)),
                   pl.BlockSpec((None, 8, nl), lambda i: (i, 0, 0))],
        out_shape=[jax.ShapeDtypeStruct((t, dm), BF16), jax.ShapeDtypeStruct((t, nl), F32),
                   jax.ShapeDtypeStruct((t, nl), F32), jax.ShapeDtypeStruct((nt, nl, tm), F32),
                   jax.ShapeDtypeStruct((nt, 8, nl), jnp.int32)],
        compiler_params=_cp("parallel"),
    )(x, g, wr, br)


def _moe_body(cnt_ref, x_ref, xn_ref, comb_ref, key_ref, keyt_ref, w1_ref, w3_ref, w2_ref, gn_ref, o_ref, on_ref,
              xe_sc, ye_sc, acc_sc, *, cap, final):
    i = pl.program_id(0)
    e = pl.program_id(1)
    f = pl.program_id(2)
    tm = x_ref.shape[0]
    n_batches = (cnt_ref[i, e] + cap - 1) // cap
    rows_of = lambda b: pl.ds(pl.multiple_of(b * cap, cap), cap)

    @pl.when((e == 0) & (f == 0))
    def _():
        acc_sc[...] = jnp.zeros(acc_sc.shape, F32)

    @pl.when(f == 0)
    def _():
        key_row = keyt_ref[pl.ds(e, 1), :]

        def gather(b, carry):
            slot = (b * cap + lax.broadcasted_iota(jnp.int32, (cap, tm), 0)).astype(F32)
            xe_sc[rows_of(b), :] = _dot((key_row == slot).astype(BF16), xn_ref[...]).astype(BF16)
            return carry

        lax.fori_loop(0, n_batches, gather, 0)

    def expert(b, carry):
        xe = xe_sc[rows_of(b), :]
        h = jax.nn.silu(_dot(xe, w1_ref[...])) * _dot(xe, w3_ref[...])
        part = _dot(h.astype(BF16), w2_ref[...])

        @pl.when(f == 0)
        def _():
            ye_sc[rows_of(b), :] = part

        @pl.when(f != 0)
        def _():
            ye_sc[rows_of(b), :] += part

        return carry

    lax.fori_loop(0, n_batches, expert, 0)

    @pl.when(f == pl.num_programs(2) - 1)
    def _():
        lane = lax.broadcasted_iota(jnp.int32, key_ref.shape, 1)
        key_col = jnp.sum(jnp.where(lane == e, key_ref[...], 0.0), axis=1, keepdims=True)
        gate = jnp.sum(jnp.where(lane == e, comb_ref[...], 0.0), axis=1, keepdims=True)

        def scatter(b, carry):
            slot = (b * cap + lax.broadcasted_iota(jnp.int32, (tm, cap), 1)).astype(F32)
            onehot = (key_col == slot).astype(BF16)
            ye = ye_sc[rows_of(b), :]
            hi = ye.astype(BF16)
            lo = (ye - hi.astype(F32)).astype(BF16)
            back = _dot(jnp.concatenate([onehot, onehot], axis=1), jnp.concatenate([hi, lo], axis=0))
            acc_sc[...] += gate * back
            return carry

        lax.fori_loop(0, n_batches, scatter, 0)

    @pl.when((e == pl.num_programs(1) - 1) & (f == pl.num_programs(2) - 1))
    def _():
        _mixer_outputs(x_ref[...] + acc_sc[...], gn_ref, o_ref, on_ref, final)


def _moe_call(x, g, wr, br, w1, w3, w2, g_next, *, layer, tm, final):
    t, dm = x.shape
    _, ne, _, fd = w1.shape
    nl = wr.shape[1]
    ft = _ffn_tile(fd)
    cap = min(MOE_BATCH, tm)
    assert cap % BF16_ROWS == 0
    max_rows = -(-tm // cap) * cap
    xn, comb, key, key_t, counts = _route_call(x, g, wr, br, tm=tm)
    once = dict(pipeline_mode=pl.Buffered(1))
    row = lambda n: pl.BlockSpec((tm, n), lambda i, e, f, cnt: (i, 0), **once)
    grid_spec = pltpu.PrefetchScalarGridSpec(
        num_scalar_prefetch=1, grid=(t // tm, ne, fd // ft),
        in_specs=[row(dm), row(dm), row(nl), row(nl),
                  pl.BlockSpec((None, nl, tm), lambda i, e, f, cnt: (i, 0, 0), **once),
                  pl.BlockSpec((None, None, dm, ft), lambda i, e, f, cnt: (layer, e, 0, f)),
                  pl.BlockSpec((None, None, dm, ft), lambda i, e, f, cnt: (layer, e, 0, f)),
                  pl.BlockSpec((None, None, ft, dm), lambda i, e, f, cnt: (layer, e, f, 0)),
                  pl.BlockSpec((1, dm), lambda i, e, f, cnt: (0, 0))],
        out_specs=[pl.BlockSpec((tm, dm), lambda i, e, f, cnt: (i, 0))] * 2,
        scratch_shapes=[pltpu.VMEM((max_rows, dm), BF16), pltpu.VMEM((max_rows, dm), F32), pltpu.VMEM((tm, dm), F32)])
    return pl.pallas_call(
        functools.partial(_moe_body, cap=cap, final=final), name="moe",
        grid_spec=grid_spec,
        out_shape=[jax.ShapeDtypeStruct((t, dm), F32), jax.ShapeDtypeStruct((t, dm), BF16)],
        compiler_params=_cp("parallel", "arbitrary", "arbitrary"),
    )(counts[:, 0, :ne], x, xn, comb, key, key_t, w1, w3, w2, g_next)


def kernel(x_prompt, x_sample, cache_cmp_k, cache_cmp_v, cache_slc_k, cache_slc_v, cache_win_k, cache_win_v, state_conv, state_lru, page_table, norm_mix_g, w_in, lru_conv_w, lru_conv_b, lru_wa, lru_ba, lru_wx, lru_bx, lru_lambda, nsa_cmp_wk, nsa_cmp_wv, gm_norm_g, gm_ws, gm_bs, w_branch, w_out, norm_ffn_g, ffn_w1, ffn_w3, ffn_w2, moe_wr, moe_br, moe_w1, moe_w3, moe_w2, final_norm_g):
    depth, dm, _ = w_in.shape
    nbp, seq, _ = x_prompt.shape
    nbs, ts, _ = x_sample.shape
    _, n_pool, page, ng, dh = cache_cmp_k.shape
    wd = state_lru.shape[-1]
    conv_w = lru_conv_w.shape[1]
    gmw = gm_norm_g.shape[-1]
    n_gm, chunk = gm_ws.shape[1], gm_ws.shape[2]
    kvw = ng * dh
    qw = NSA_HEADS * dh
    rep = NSA_HEADS // ng
    npg = page_table.shape[1]
    past = npg * page
    wbuf = cache_win_k.shape[2]
    tp, tsm = nbp * seq, nbs * ts
    ne = moe_wr.shape[-1]
    nblk = seq // NSA_BLOCK
    scale = dh ** -0.5
    s_scale = 1.0 if _is_power_of_two(scale) else scale

    assert seq % chunk == 0 and seq % LRU_CHUNK == 0 and seq >= NSA_WINDOW
    assert past % NSA_BLOCK == 0 and ts < NSA_BLOCK and ts < chunk and wbuf == NSA_WINDOW and nbs % 8 == 0
    assert past // NSA_BLOCK >= NSA_TOPN - 1 and page % LANES == 0

    offs = [0, wd, 2 * wd, 2 * wd + qw, 2 * wd + qw + 6 * kvw, 2 * wd + qw + 6 * kvw + 3 * NSA_HEADS]
    offs += [offs[-1] + 2 * gmw, offs[-1] + 2 * gmw + 3 * dm]
    w_part = lambda lo, hi: w_in[:, :, lo:hi].astype(BF16)
    w_lru = w_part(0, offs[2])
    w_nsa = jnp.pad(w_part(offs[2], offs[5]), ((0, 0), (0, 0), (0, LANES - 3 * NSA_HEADS)))
    w_uv = w_part(offs[5], offs[6])
    w_mg = w_part(offs[6], offs[7])
    wa_b, wx_b = lru_wa.astype(BF16), lru_wx.astype(BF16)
    wb_b, wo_b = w_branch.astype(BF16), w_out.astype(BF16)
    f1_b, f3_b, f2_b = ffn_w1.astype(BF16), ffn_w3.astype(BF16), ffn_w2.astype(BF16)
    m1_b, m3_b, m2_b = moe_w1.astype(BF16), moe_w3.astype(BF16), moe_w2.astype(BF16)
    wr_pad = jnp.pad(moe_wr, ((0, 0), (0, 0), (0, LANES - ne)))
    br_pad = jnp.pad(moe_br, ((0, 0), (0, LANES - ne)), constant_values=NEG)[:, None, :]
    wck = nsa_cmp_wk.reshape(depth, NSA_BLOCK, kvw)
    wcv = nsa_cmp_wv.reshape(depth, NSA_BLOCK, kvw)
    cmp_w_t = lambda w: jnp.tile(jnp.transpose(w, (0, 2, 3, 1)).reshape(depth, kvw, NSA_BLOCK), (1, 1, page // NSA_BLOCK))
    wck_t, wcv_t = cmp_w_t(nsa_cmp_wk), cmp_w_t(nsa_cmp_wv)
    gc = gmw // n_gm
    bs_full = jnp.repeat(jnp.swapaxes(gm_bs, 1, 2), gc, axis=2)
    ws_short = jnp.repeat(jnp.transpose(gm_ws[:, :, :ts, :ts], (0, 2, 3, 1)).reshape(depth, ts * ts, n_gm), gc, axis=2)
    vec = lambda a, l: a[l][None, :]
    pps = min(PAGES_PER_STEP, npg)
    feature_major = lambda a: jnp.transpose(a, (0, 1, 3, 4, 2)).reshape(a.shape[0], a.shape[1], kvw, a.shape[2])
    pool_ck, pool_cv = feature_major(cache_cmp_k), feature_major(cache_cmp_v)
    pool_sk, pool_sv = feature_major(cache_slc_k), feature_major(cache_slc_v)
    win_k, win_v = feature_major(cache_win_k), feature_major(cache_win_v)
    eye_g = jnp.eye(ng, dtype=BF16)

    tm_p = min(ROW_TILE, tp)
    xp = x_prompt.reshape(tp, dm)
    xs = x_sample.reshape(tsm, dm)
    zeros_cs = jnp.zeros((nbp, conv_w - 1, wd), F32)
    zeros_h = jnp.zeros((nbp, 1, wd), F32)
    to_tm = lambda a, n: jnp.swapaxes(a.reshape(nbs, n, -1), 0, 1).reshape(1, n * nbs, -1)
    from_tm = lambda a, n: jnp.swapaxes(a.reshape(n, nbs, -1), 0, 1)
    new_rows_t = lambda a: jnp.pad(jnp.swapaxes(a.reshape(nbs, ts, kvw), 1, 2), ((0, 0), (0, 0), (0, LANES - ts)))

    p_stacks, w_stacks = (), ()
    p_small = [[] for _ in range(2)]
    s_out = [[] for _ in range(7)]
    xpn = _norm_call(xp, vec(norm_mix_g, 0), tm=tm_p)
    xsn = _norm_call(xs, vec(norm_mix_g, 0), tm=tsm)
    for l in range(depth):
        last = l == depth - 1
        g_ffn = vec(norm_ffn_g, l)
        g_next = final_norm_g[None, :] if last else vec(norm_mix_g, l + 1)
        lru_w = (lru_conv_w[l], vec(lru_conv_b, l), wa_b, vec(lru_ba, l), wx_b, vec(lru_bx, l), vec(lru_lambda, l))

        o_a, conv_new, h_last = _lru_call(xpn.reshape(nbp, seq, dm), w_lru, zeros_cs, zeros_h, *lru_w,
                                          layer=l, nb=1, tch=min(LRU_CHUNK, seq), starts_at_zero=True)
        o_c = _gmlp_call(xpn, w_uv, vec(gm_norm_g, l), gm_ws, bs_full[l], layer=l, tm=tm_p)
        qt, gates_t, kcmp, vcmp, ka, vst, kwg, vwt, *p_stacks = _qkv_prompt_call(
            xpn, w_nsa, wck[l], wcv[l], p_stacks, layer=l, depth=depth, nbatch=nbp, seq=seq, tm=tm_p,
            qw=qw, kvw=kvw, dh=dh)
        vcmp_t = jnp.swapaxes(vcmp.reshape(nbp, nblk, kvw), 1, 2)
        ocw_t, selb = _nsa_cw_call(qt, gates_t, kcmp, vcmp_t, kwg, vwt, nbatch=nbp, seq=seq, ng=ng, dh=dh,
                                   s_scale=s_scale)
        o_b = _nsa_sel_call(qt, selb, ka, vst, ocw_t, gates_t, nbatch=nbp, seq=seq, ng=ng, dh=dh, s_scale=s_scale)
        xp, xpn = _merge_call(xp, xpn, w_mg, o_a.reshape(tp, wd), o_b, o_c, wb_b, wo_b, g_ffn, layer=l, tm=tm_p)
        if l % 2 == 0:
            xp, xpn = _ffn_call(xp, xpn, f1_b, f3_b, f2_b, g_next, layer=l // 2, tm=tm_p, final=last)
        else:
            xp, xpn = _moe_call(xp, g_ffn, wr_pad[l // 2], br_pad[l // 2], m1_b, m3_b, m2_b, g_next,
                                layer=l // 2, tm=min(MOE_TILE, tp), final=last)
        p_small[0].append(conv_new)
        p_small[1].append(h_last.reshape(nbp, wd))

        xsn_tm = to_tm(xsn, ts)
        o_a, conv_new, h_last = _lru_call(xsn_tm, w_lru, to_tm(state_conv[l], conv_w - 1),
                                          state_lru[l][None], *lru_w, layer=l, nb=nbs, tch=ts, starts_at_zero=False)
        o_a = from_tm(o_a, ts).reshape(tsm, wd)
        o_c, gm_v = _gmlp_short_call(xsn_tm[0], w_uv, vec(gm_norm_g, l), ws_short[l], bs_full[l][:ts],
                                     layer=l, n=ts, nb=nbs)
        o_c = from_tm(o_c, ts).reshape(tsm, gmw)
        q, k_c, v_c, k_s, v_s, k_w, v_w, gates = _qkv_sample_call(xsn, w_nsa, layer=l, qw=qw, kvw=kvw, dh=dh)
        q5 = jnp.transpose(q.reshape(nbs, ts, ng, rep, dh), (0, 3, 2, 1, 4))
        q_bd = (q5[:, :, :, :, None, :] * eye_g[None, None, :, None, :, None]).reshape(nbs, rep * ng * ts, kvw)
        g5 = jnp.transpose(gates[:, :3 * NSA_HEADS].reshape(nbs, ts, ng, rep, 3), (0, 3, 2, 1, 4))
        gall = jnp.pad(g5.reshape(nbs, rep * ng * ts, 3), ((0, 0), (0, 0), (0, LANES - 3)))
        kct, vct = _cmp_pages_call(page_table, pool_ck, pool_cv, wck_t[l], wcv_t[l], layer=l, pps=pps)
        ocw, selb, *w_stacks = _samp_cw_call(
            q_bd, gall, kct, vct, win_k, win_v, new_rows_t(k_w), new_rows_t(v_w), w_stacks,
            layer=l, depth=depth, ts=ts, past=past, rep=rep, dh=dh)
        o_all = _samp_sel_call(page_table, pool_sk, pool_sv, q_bd, selb, new_rows_t(k_s), new_rows_t(v_s),
                               ocw, gall, layer=l, pps=pps, ts=ts, dh=dh)
        o6 = o_all.reshape(nbs, rep, ng, ts, ng, dh)
        o_b = jnp.stack([o6[:, :, gi, :, gi, :] for gi in range(ng)], axis=2)
        o_b = jnp.transpose(o_b, (0, 3, 2, 1, 4)).reshape(tsm, qw).astype(BF16)
        xs, xsn = _merge_call(xs, xsn, w_mg, o_a, o_b, o_c, wb_b, wo_b, g_ffn, layer=l, tm=tsm)
        if l % 2 == 0:
            xs, xsn = _ffn_call(xs, xsn, f1_b, f3_b, f2_b, g_next, layer=l // 2, tm=tsm, final=last)
        else:
            xs, xsn = _moe_call(xs, g_ffn, wr_pad[l // 2], br_pad[l // 2], m1_b, m3_b, m2_b, g_next,
                                layer=l // 2, tm=tsm, final=last)
        shs = (nbs, ts, ng, dh)
        for lst, a in zip(s_out, (k_c.reshape(shs), v_c.reshape(shs), k_s.reshape(shs), v_s.reshape(shs),
                                  from_tm(conv_new[0], conv_w - 1), h_last[0], from_tm(gm_v, ts))):
            lst.append(a)

    row_major = lambda a: jnp.transpose(a.reshape(a.shape[0], a.shape[1], ng, dh, a.shape[3]), (0, 1, 4, 2, 3))
    keep = min(NSA_WINDOW, seq)
    kc_st, vc_st, ks_st, vs_st, kw_st, vw_st = p_stacks
    s_small = [jnp.stack(a) for a in s_out]
    return (xp.reshape(nbp, seq, dm), xs.reshape(nbs, ts, dm),
            row_major(kc_st), row_major(vc_st), row_major(ks_st), row_major(vs_st),
            row_major(kw_st[..., seq - keep:]), row_major(vw_st[..., seq - keep:]),
            jnp.stack(p_small[0]), jnp.stack(p_small[1]),
            s_small[0], s_small[1], s_small[2], s_small[3],
            row_major(w_stacks[0]), row_major(w_stacks[1]),
            s_small[4], s_small[5], s_small[6])
```

```python
import functools
import math

import jax
import jax.numpy as jnp
from jax import lax
from jax.experimental import pallas as pl
from jax.experimental.pallas import tpu as pltpu

F32 = jnp.float32
BF16 = jnp.bfloat16

NSA_HEADS = 16
NSA_BLOCK = 64
NSA_TOPN = 8
NSA_WINDOW = 512
SEL_FORCE = 1.0e4
LRU_C = 8.0
TOP_K = 2
RMS_EPS = 1e-6
NEG = -1e30
M_INIT = -3.0e38

Q_BLOCK = 256
SEL_Q_BLOCK = 512
KV_TILE = 1024
LRU_CHUNK = 256
ROW_TILE = 512
MOE_TILE = 1024
MOE_BATCH = 288
PAGES_PER_STEP = 32
BF16_ROWS = 16
LANES = 128
VMEM_LIMIT = 56 * 2 ** 20


def _cp(*sem):
    return pltpu.CompilerParams(dimension_semantics=sem, vmem_limit_bytes=VMEM_LIMIT)


def _dot(a, b):
    return jnp.dot(a, b, preferred_element_type=F32)


def _dot_nt(a, b):
    return lax.dot_general(a, b, (((1,), (1,)), ((), ())), preferred_element_type=F32)


def _rms(x, g):
    return x * lax.rsqrt(jnp.mean(x * x, axis=-1, keepdims=True) + RMS_EPS) * g


def _msoftmax(s, mask, axis=-1):
    s = jnp.where(mask, s, NEG)
    e = jnp.where(mask, jnp.exp(s - jnp.max(s, axis=axis, keepdims=True)), 0.0)
    return e / jnp.maximum(jnp.sum(e, axis=axis, keepdims=True), 1e-30)


def _is_power_of_two(x):
    return math.frexp(x)[0] == 0.5


def _full(shape):
    n = len(shape)
    return pl.BlockSpec(shape, lambda *_: (0,) * n)


def _at_layer(stacked, layer):
    n = stacked.ndim - 1
    return pl.BlockSpec((None,) + stacked.shape[1:], lambda *_: (layer,) + (0,) * n)


def _lru_body(xn_ref, w_ref, cs_ref, h0_ref, cw_ref, cb_ref, wa_ref, ba_ref, wx_ref, bx_ref, lam_ref,
              oa_ref, cn_ref, hl_ref, xbuf, hc, *, nb, tch, conv_w, starts_at_zero):
    c = pl.program_id(1)
    rows = nb * tch
    wd = hc.shape[-1]
    nst = (conv_w - 1) * nb
    cr = xbuf.shape[0] - rows
    nh = wa_ref.shape[0]
    bw = wd // nh

    @pl.when(c == 0)
    def _():
        if cr > nst:
            xbuf[0:cr - nst, :] = jnp.zeros((cr - nst, wd), F32)
        xbuf[cr - nst:cr, :] = cs_ref[...]
        hc[...] = h0_ref[...]

    z = _dot(xn_ref[...], w_ref[...])
    gate = z[:, wd:]
    xbuf[cr:cr + rows, :] = z[:, :wd]
    y = cb_ref[...] + xbuf[cr - nst:cr - nst + rows, :] * cw_ref[0:1, :]
    for k in range(1, conv_w):
        st = cr - nst + k * nb
        y = y + xbuf[st:st + rows, :] * cw_ref[k:k + 1, :]
    cn_ref[...] = xbuf[cr + rows - nst:cr + rows, :]
    xbuf[0:cr, :] = xbuf[rows:rows + cr, :]

    yb = y.astype(BF16)

    def block_diag(wr):
        return jnp.concatenate([_dot(yb[:, j * bw:(j + 1) * bw], wr[j]) for j in range(nh)], axis=1)

    r = jax.nn.sigmoid(block_diag(wa_ref) + ba_ref[...])
    i = jax.nn.sigmoid(block_diag(wx_ref) + bx_ref[...])
    lam = lam_ref[...]
    softplus_neg_lam = jnp.maximum(-lam, 0.0) + jnp.log1p(jnp.exp(-jnp.abs(lam)))
    a = jnp.exp((-LRU_C) * r * softplus_neg_lam)
    mult = jnp.sqrt(1.0 - a * a)
    row = lax.broadcasted_iota(jnp.int32, (rows, wd), 0)
    if starts_at_zero:
        mult = jnp.where((row < nb) & (c == 0), 1.0, mult)
    u = mult * (i * y)
    if nb == 1:
        u = u + jnp.where(row < 1, a * hc[...], 0.0)
    else:
        u = jnp.concatenate([u[:nb] + a[:nb] * hc[...], u[nb:]], axis=0)
    d = nb
    while d < rows:
        a_sh = jnp.where(row < d, 1.0, pltpu.roll(a, d, axis=0))
        u_sh = jnp.where(row < d, 0.0, pltpu.roll(u, d, axis=0))
        u = a * u_sh + u
        a = a * a_sh
        d *= 2
    hc[...] = u[rows - nb:rows, :]
    hl_ref[...] = u[rows - nb:rows, :]
    oa_ref[...] = (jax.nn.gelu(gate) * u).astype(oa_ref.dtype)


def _lru_call(xn, w, cs, h0, cw, cb, wa, ba, wx, bx, lam, *, layer, nb, tch, starts_at_zero):
    nbatch, rows_total, dm = xn.shape
    wd = h0.shape[-1]
    conv_w = cw.shape[0]
    nst = (conv_w - 1) * nb
    rows = nb * tch
    cr = -(-nst // 8) * 8
    assert rows_total % rows == 0 and rows >= cr
    body = functools.partial(_lru_body, nb=nb, tch=tch, conv_w=conv_w, starts_at_zero=starts_at_zero)
    return pl.pallas_call(
        body, name="lru",
        grid=(nbatch, rows_total // rows),
        in_specs=[pl.BlockSpec((None, rows, dm), lambda b, c: (b, c, 0)),
                  _at_layer(w, layer),
                  pl.BlockSpec((None, nst, wd), lambda b, c: (b, 0, 0)),
                  pl.BlockSpec((None, nb, wd), lambda b, c: (b, 0, 0)),
                  _full(cw.shape), _full(cb.shape), _at_layer(wa, layer), _full(ba.shape), _at_layer(wx, layer),
                  _full(bx.shape), _full(lam.shape)],
        out_specs=[pl.BlockSpec((None, rows, wd), lambda b, c: (b, c, 0)),
                   pl.BlockSpec((None, nst, wd), lambda b, c: (b, 0, 0)),
                   pl.BlockSpec((None, nb, wd), lambda b, c: (b, 0, 0))],
        out_shape=[jax.ShapeDtypeStruct((nbatch, rows_total, wd), BF16),
                   jax.ShapeDtypeStruct((nbatch, nst, wd), F32),
                   jax.ShapeDtypeStruct((nbatch, nb, wd), F32)],
        scratch_shapes=[pltpu.VMEM((cr + rows, wd), F32), pltpu.VMEM((nb, wd), F32)],
        compiler_params=_cp("arbitrary", "arbitrary"),
    )(xn, w, cs, h0, cw, cb, wa, ba, wx, bx, lam)


def _gmlp_body(xn_ref, w_ref, gg_ref, ws_ref, bs_ref, o_ref, *, chunk):
    tm = xn_ref.shape[0]
    ge = jax.nn.gelu(_dot(xn_ref[...], w_ref[...]))
    wd = ge.shape[1] // 2
    u = ge[:, :wd]
    vb = _rms(ge[:, wd:], gg_ref[...]).astype(BF16)
    ng = ws_ref.shape[0]
    gc = wd // ng
    tril = lax.broadcasted_iota(jnp.int32, (chunk, chunk), 0) >= lax.broadcasted_iota(jnp.int32, (chunk, chunk), 1)
    wsm = [jnp.where(tril, ws_ref[gi], 0.0).astype(BF16) for gi in range(ng)]
    for ch in range(tm // chunk):
        lo, hi = ch * chunk, (ch + 1) * chunk
        mixed = jnp.concatenate([_dot(wsm[gi], vb[lo:hi, gi * gc:(gi + 1) * gc]) for gi in range(ng)], axis=1)
        o_ref[lo:hi, :] = (u[lo:hi] * (mixed + bs_ref[...])).astype(o_ref.dtype)


def _gmlp_call(xn, w, gg, ws, bs_full, *, layer, tm):
    t, dm = xn.shape
    wd = gg.shape[-1]
    chunk = ws.shape[-1]
    assert t % tm == 0 and tm % chunk == 0
    return pl.pallas_call(
        functools.partial(_gmlp_body, chunk=chunk), name="gmlp",
        grid=(t // tm,),
        in_specs=[pl.BlockSpec((tm, dm), lambda i: (i, 0)), _at_layer(w, layer), _full(gg.shape),
                  _at_layer(ws, layer), _full(bs_full.shape)],
        out_specs=pl.BlockSpec((tm, wd), lambda i: (i, 0)),
        out_shape=jax.ShapeDtypeStruct((t, wd), BF16),
        compiler_params=_cp("parallel"),
    )(xn, w, gg, ws, bs_full)


def _gmlp_short_body(xn_ref, w_ref, gg_ref, wexp_ref, bexp_ref, o_ref, v_ref, *, n, nb):
    ge = jax.nn.gelu(_dot(xn_ref[...], w_ref[...]))
    wd = ge.shape[1] // 2
    u = ge[:, :wd]
    v = _rms(ge[:, wd:], gg_ref[...])
    v_ref[...] = v
    for t in range(n):
        m = bexp_ref[t:t + 1, :] + wexp_ref[t * n:t * n + 1, :] * v[0:nb]
        for s in range(1, t + 1):
            m = m + wexp_ref[t * n + s:t * n + s + 1, :] * v[s * nb:(s + 1) * nb]
        o_ref[t * nb:(t + 1) * nb, :] = (u[t * nb:(t + 1) * nb] * m).astype(o_ref.dtype)


def _gmlp_short_call(xn, w, gg, wexp, bexp, *, layer, n, nb):
    t, dm = xn.shape
    wd = gg.shape[-1]
    return pl.pallas_call(
        functools.partial(_gmlp_short_body, n=n, nb=nb), name="gmlp_short",
        grid=(1,),
        in_specs=[_full(xn.shape), _at_layer(w, layer), _full(gg.shape), _full(wexp.shape),
                  _full(bexp.shape)],
        out_specs=[_full((t, wd)), _full((t, wd))],
        out_shape=[jax.ShapeDtypeStruct((t, wd), BF16), jax.ShapeDtypeStruct((t, wd), F32)],
        compiler_params=_cp("arbitrary"),
    )(xn, w, gg, wexp, bexp)


def _qkv_prompt_body(xn_ref, w_ref, wck_ref, wcv_ref, *refs, qw, kvw, dh, kt, q_scale, n_alias):
    (q_ref, gt_ref, kcmp_ref, vcmp_ref, ka_ref, vst_ref, kwg_ref, vwt_ref) = refs[n_alias:n_alias + 8]
    stack_refs = refs[n_alias + 8:]
    tm = xn_ref.shape[0]
    ng = kvw // dh
    z = _dot(xn_ref[...], w_ref[...])
    parts = [z[:, qw + j * kvw:qw + (j + 1) * kvw] for j in range(6)]
    parts_t = [p.T for p in parts]
    for ref, p in zip(stack_refs, parts_t):
        ref[...] = p
    q_ref[...] = (z[:, :qw] * q_scale).T.astype(BF16)
    gt_ref[...] = jax.nn.sigmoid(z[:, qw + 6 * kvw:]).T
    nbt = tm // NSA_BLOCK
    kcmp_ref[...] = jnp.sum(parts[0].reshape(nbt, NSA_BLOCK, kvw) * wck_ref[...][None], axis=1)
    vcmp_ref[...] = jnp.sum(parts[1].reshape(nbt, NSA_BLOCK, kvw) * wcv_ref[...][None], axis=1)
    bpt = kt // NSA_BLOCK
    pos0 = lax.rem(pl.program_id(0) * tm, kt)
    blk = (pos0 + lax.broadcasted_iota(jnp.int32, (tm, bpt), 0)) // NSA_BLOCK
    onehot = (blk == lax.broadcasted_iota(jnp.int32, (tm, bpt), 1)).astype(BF16)
    ones_row = (lax.broadcasted_iota(jnp.int32, (BF16_ROWS, tm), 0) == 0).astype(BF16)
    ksb = parts[2].astype(BF16)
    kwb = parts[4].astype(BF16)
    vst = parts_t[3].astype(BF16)
    vwt = parts_t[5].astype(BF16)
    for gi in range(ng):
        ka_ref[gi, :, 0:dh] = ksb[:, gi * dh:(gi + 1) * dh]
        ka_ref[gi, :, dh:dh + bpt] = onehot
        kwg_ref[gi] = kwb[:, gi * dh:(gi + 1) * dh]
        for ref, v in ((vst_ref, vst), (vwt_ref, vwt)):
            ref[gi, 0:dh, :] = v[gi * dh:(gi + 1) * dh, :]
            ref[gi, dh:dh + BF16_ROWS, :] = ones_row


def _qkv_prompt_call(xn, w, wck, wcv, stacks, *, layer, depth, nbatch, seq, tm, qw, kvw, dh):
    t, dm = xn.shape
    ng = kvw // dh
    kt = min(KV_TILE, seq)
    bpt = kt // NSA_BLOCK
    nbt = tm // NSA_BLOCK
    nst = seq // tm
    assert t % tm == 0 and tm % (8 * NSA_BLOCK) == 0 and seq % tm == 0 and kt % tm == 0 and tm % LANES == 0
    row = lambda n: pl.BlockSpec((tm, n), lambda i: (i, 0))
    col = lambda n: pl.BlockSpec((n, tm), lambda i: (0, i))
    stack_spec = pl.BlockSpec((None, None, kvw, tm), lambda i: (layer, i // nst, 0, i % nst))
    stack_shape = jax.ShapeDtypeStruct((depth, nbatch, kvw, seq), F32)
    out_specs = [col(qw), col(LANES), pl.BlockSpec((nbt, kvw), lambda i: (i, 0)), pl.BlockSpec((nbt, kvw), lambda i: (i, 0)),
                 pl.BlockSpec((ng, tm, dh + bpt), lambda i: (0, i, 0)),
                 pl.BlockSpec((ng, dh + BF16_ROWS, tm), lambda i: (0, 0, i)),
                 pl.BlockSpec((ng, tm, dh), lambda i: (0, i, 0)),
                 pl.BlockSpec((ng, dh + BF16_ROWS, tm), lambda i: (0, 0, i))] + [stack_spec] * 6
    out_shape = [jax.ShapeDtypeStruct((qw, t), BF16), jax.ShapeDtypeStruct((LANES, t), F32),
                 jax.ShapeDtypeStruct((t // NSA_BLOCK, kvw), F32), jax.ShapeDtypeStruct((t // NSA_BLOCK, kvw), F32),
                 jax.ShapeDtypeStruct((ng, t, dh + bpt), BF16), jax.ShapeDtypeStruct((ng, dh + BF16_ROWS, t), BF16),
                 jax.ShapeDtypeStruct((ng, t, dh), BF16),
                 jax.ShapeDtypeStruct((ng, dh + BF16_ROWS, t), BF16)] + [stack_shape] * 6
    scale = dh ** -0.5
    q_scale = scale if _is_power_of_two(scale) else 1.0
    n_alias = len(stacks)
    body = functools.partial(_qkv_prompt_body, qw=qw, kvw=kvw, dh=dh, kt=kt, q_scale=q_scale, n_alias=n_alias)
    return pl.pallas_call(
        body, name="qkv_prompt",
        grid=(t // tm,),
        in_specs=[row(dm), _at_layer(w, layer), _full(wck.shape), _full(wcv.shape)]
        + [pl.BlockSpec(memory_space=pl.ANY)] * n_alias,
        out_specs=out_specs, out_shape=out_shape,
        input_output_aliases={4 + k: 8 + k for k in range(n_alias)},
        compiler_params=_cp("parallel"),
    )(xn, w, wck, wcv, *stacks)


def _qkv_sample_body(xn_ref, w_ref, q_ref, kc_ref, vc_ref, ks_ref, vs_ref, kw_ref, vw_ref, gt_ref,
                     *, qw, kvw, dh):
    z = _dot(xn_ref[...], w_ref[...])
    for j, ref in enumerate((kc_ref, vc_ref, ks_ref, vs_ref, kw_ref, vw_ref)):
        for gi in range(kvw // dh):
            ref[:, gi, :] = z[:, qw + j * kvw + gi * dh:qw + j * kvw + (gi + 1) * dh]
    q_ref[...] = z[:, :qw].astype(BF16)
    gt_ref[...] = jax.nn.sigmoid(z[:, qw + 6 * kvw:])


def _qkv_sample_call(xn, w, *, layer, qw, kvw, dh):
    t, dm = xn.shape
    ng = kvw // dh
    return pl.pallas_call(
        functools.partial(_qkv_sample_body, qw=qw, kvw=kvw, dh=dh), name="qkv_sample",
        grid=(1,),
        in_specs=[_full(xn.shape), _at_layer(w, layer)],
        out_specs=[_full((t, qw))] + [_full((t, ng, dh))] * 6 + [_full((t, LANES))],
        out_shape=[jax.ShapeDtypeStruct((t, qw), BF16)] + [jax.ShapeDtypeStruct((t, ng, dh), F32)] * 6
        + [jax.ShapeDtypeStruct((t, LANES), F32)],
        compiler_params=_cp("arbitrary"),
    )(xn, w)


def _select_blocks(score, blk, nblk, count, axis):
    for _ in range(count):
        m = jnp.max(score, axis=axis, keepdims=True)
        idx = jnp.min(jnp.where(score == m, blk, nblk), axis=axis, keepdims=True)
        score = jnp.where(blk == idx, M_INIT, score)
    return score == M_INIT


def _nsa_cw_body(qt_ref, gt_ref, kc_ref, vct_ref, kw_ref, vwt_ref, ocw_ref, selb_ref,
                 *, seq, ng, rep, dh, s_scale):
    c = pl.program_id(1)
    qb = qt_ref.shape[1]
    nblk = kc_ref.shape[0]
    band = min(NSA_WINDOW + qb, seq)
    st = pl.multiple_of(jnp.maximum(c * qb - NSA_WINDOW, 0), qb)
    cols = rep * qb
    t_q = c * qb + lax.broadcasted_iota(jnp.int32, (1, qb), 1)
    t_c = jnp.concatenate([t_q] * rep, axis=1)
    ok_c = (lax.broadcasted_iota(jnp.int32, (nblk, cols), 0) + 1) * NSA_BLOCK - 1 <= t_c
    kp = st + lax.broadcasted_iota(jnp.int32, (band, cols), 0)
    ok_w = (kp <= t_c) & (kp > t_c - NSA_WINDOW)
    blk = lax.broadcasted_iota(jnp.int32, (nblk, qb), 0)
    cur = t_q // NSA_BLOCK
    for gi in range(ng):
        heads = [gi * rep + r for r in range(rep)]
        lanes = slice(gi * dh, (gi + 1) * dh)
        qg = jnp.concatenate([qt_ref[h * dh:(h + 1) * dh, :] for h in heads], axis=1)
        s_c = _dot(kc_ref[:, lanes].astype(BF16), qg)
        if s_scale != 1.0:
            s_c = s_c * s_scale
        p_c = _msoftmax(s_c, ok_c, axis=0)
        o_c = _dot(vct_ref[lanes, :].astype(BF16), p_c.astype(BF16))
        imp = p_c[:, 0:qb]
        for r in range(1, rep):
            imp = imp + p_c[:, r * qb:(r + 1) * qb]
        score = jnp.where(blk == cur, SEL_FORCE, jnp.where(blk < cur, imp, -1.0))
        sel = _select_blocks(score, blk, nblk, min(NSA_TOPN, nblk), axis=0)
        selb_ref[gi] = jnp.where(sel, 0.0, NEG).astype(BF16)
        s_w = _dot(kw_ref[gi, pl.ds(st, band), :], qg)
        if s_scale != 1.0:
            s_w = s_w * s_scale
        s_w = jnp.where(ok_w, s_w, NEG)
        e_w = jnp.exp(s_w - jnp.max(s_w, axis=0, keepdims=True))
        ov = _dot(vwt_ref[gi, :, pl.ds(st, band)], e_w.astype(BF16))
        o_w = ov[0:dh] / ov[dh:dh + 1]
        g_c = jnp.concatenate([gt_ref[3 * h:3 * h + 1, :] for h in heads], axis=1)
        g_w = jnp.concatenate([gt_ref[3 * h + 2:3 * h + 3, :] for h in heads], axis=1)
        ocw = g_c * o_c + g_w * o_w
        for r, h in enumerate(heads):
            ocw_ref[h * dh:(h + 1) * dh, :] = ocw[:, r * qb:(r + 1) * qb]


def _nsa_cw_call(qt, gates_t, kcmp, vcmp_t, kwg, vwt, *, nbatch, seq, ng, dh, s_scale):
    qw, t = qt.shape
    kvw = ng * dh
    rep = qw // kvw
    qb = min(Q_BLOCK, seq)
    nq = seq // qb
    nblk = seq // NSA_BLOCK
    body = functools.partial(_nsa_cw_body, seq=seq, ng=ng, rep=rep, dh=dh, s_scale=s_scale)
    qcol = lambda n: pl.BlockSpec((n, qb), lambda b, c: (0, b * nq + c))
    return pl.pallas_call(
        body, name="nsa_cmp_win",
        grid=(nbatch, nq),
        in_specs=[qcol(qw), qcol(LANES),
                  pl.BlockSpec((nblk, kvw), lambda b, c: (b, 0)), pl.BlockSpec((None, kvw, nblk), lambda b, c: (b, 0, 0)),
                  pl.BlockSpec((ng, seq, dh), lambda b, c: (0, b, 0)),
                  pl.BlockSpec((ng, dh + BF16_ROWS, seq), lambda b, c: (0, 0, b))],
        out_specs=[qcol(qw), pl.BlockSpec((ng, None, nblk, qb), lambda b, c: (0, b * nq + c, 0, 0))],
        out_shape=[jax.ShapeDtypeStruct((qw, t), F32), jax.ShapeDtypeStruct((ng, nbatch * nq, nblk, qb), BF16)],
        compiler_params=_cp("parallel", "arbitrary"),
    )(qt, gates_t, kcmp, vcmp_t, kwg, vwt)


def _nsa_sel_body(qt_ref, selb_ref, ka_ref, vst_ref, ocw_ref, gt_ref, o_ref, m_sc, acc_sc, sa_sc, sb_sc, ma_sc, mb_sc,
                  *, rep, dh, kt, s_scale):
    g = pl.program_id(1)
    c = pl.program_id(2)
    qb = qt_ref.shape[1]
    cols = rep * qb
    bpt = kt // NSA_BLOCK
    qg = jnp.concatenate([qt_ref[r * dh:(r + 1) * dh, :] for r in range(rep)], axis=1)
    m_sc[...] = jnp.full(m_sc.shape, M_INIT, F32)
    acc_sc[...] = jnp.zeros(acc_sc.shape, F32)

    def scores(j, s_ref, mt_ref):
        st = pl.multiple_of(j * kt, kt)
        blks = pl.ds(pl.multiple_of(j * bpt, bpt), bpt)
        sb = jnp.concatenate([selb_ref[i, blks, :] for i in range(selb_ref.shape[0])], axis=1)
        qa = jnp.concatenate([qg, jnp.concatenate([sb] * rep, axis=1)], axis=0)
        s = _dot(ka_ref[pl.ds(st, kt), :], qa)
        if s_scale != 1.0:
            s = s * s_scale
        s_ref[...] = s
        mt_ref[...] = jnp.max(s, axis=0, keepdims=True)

    def absorb(j, s, m_tile):
        st = pl.multiple_of(j * kt, kt)
        m_old = m_sc[...]
        m_new = jnp.maximum(m_old, m_tile)
        p = jnp.exp(s - m_new)
        acc_sc[...] = jnp.exp(m_old - m_new) * acc_sc[...] + _dot(vst_ref[:, pl.ds(st, kt)], p.astype(BF16))
        m_sc[...] = m_new

    def finish(s_ref):
        st = pl.multiple_of(n_full * kt, kt)
        kp = st + lax.broadcasted_iota(jnp.int32, (kt, cols), 0)
        t_c = c * qb + (lax.broadcasted_iota(jnp.int32, (kt, cols), 1) & (qb - 1))
        s = jnp.where(kp <= t_c, s_ref[...], NEG)
        absorb(n_full, s, jnp.max(s, axis=0, keepdims=True))
        acc = acc_sc[...]
        o_s = acc[0:dh] / acc[dh:dh + 1]
        out = []
        for r in range(rep):
            g_s = gt_ref[pl.ds(3 * (g * rep + r) + 1, 1), :]
            out.append(ocw_ref[r * dh:(r + 1) * dh, :] + g_s * o_s[:, r * qb:(r + 1) * qb])
        o_ref[...] = jnp.concatenate(out, axis=0).T.astype(o_ref.dtype)

    n_full = (c * qb) // kt
    scores(0, sa_sc, ma_sc)

    def pair(k, carry):
        scores(2 * k + 1, sb_sc, mb_sc)
        absorb(2 * k, sa_sc[...], ma_sc[...])
        scores(2 * k + 2, sa_sc, ma_sc)
        absorb(2 * k + 1, sb_sc[...], mb_sc[...])
        return carry

    lax.fori_loop(0, n_full // 2, pair, 0)

    @pl.when(n_full % 2 == 1)
    def _():
        scores(n_full, sb_sc, mb_sc)
        absorb(n_full - 1, sa_sc[...], ma_sc[...])
        finish(sb_sc)

    @pl.when(n_full % 2 == 0)
    def _():
        finish(sa_sc)


def _nsa_sel_call(qt, selb, ka, vst, ocw_t, gates_t, *, nbatch, seq, ng, dh, s_scale):
    qw, t = qt.shape
    rep = qw // (ng * dh)
    qb = min(SEL_Q_BLOCK, seq)
    assert qb & (qb - 1) == 0
    nq = seq // qb
    _, _, nblk, qb_in = selb.shape
    per_step = qb // qb_in
    kt = min(KV_TILE, seq)
    bpt = kt // NSA_BLOCK
    assert seq % kt == 0 and kt % qb == 0 and bpt % BF16_ROWS == 0 and ka.shape[-1] == dh + bpt
    assert qb % qb_in == 0
    body = functools.partial(_nsa_sel_body, rep=rep, dh=dh, kt=kt, s_scale=s_scale)
    qcol = pl.BlockSpec((rep * dh, qb), lambda b, g, c: (g, b * nq + c))
    return pl.pallas_call(
        body, name="nsa_selected",
        grid=(nbatch, ng, nq),
        in_specs=[qcol,
                  pl.BlockSpec((None, per_step, nblk, qb_in), lambda b, g, c: (g, b * nq + c, 0, 0)),
                  pl.BlockSpec((None, seq, dh + bpt), lambda b, g, c: (g, b, 0)),
                  pl.BlockSpec((None, dh + BF16_ROWS, seq), lambda b, g, c: (g, 0, b)),
                  qcol,
                  pl.BlockSpec((LANES, qb), lambda b, g, c: (0, b * nq + c))],
        out_specs=pl.BlockSpec((qb, rep * dh), lambda b, g, c: (b * nq + c, g)),
        out_shape=jax.ShapeDtypeStruct((t, qw), BF16),
        scratch_shapes=[pltpu.VMEM((1, rep * qb), F32), pltpu.VMEM((dh + BF16_ROWS, rep * qb), F32),
                        pltpu.VMEM((kt, rep * qb), F32), pltpu.VMEM((kt, rep * qb), F32),
                        pltpu.VMEM((1, rep * qb), F32), pltpu.VMEM((1, rep * qb), F32)],
        compiler_params=_cp("parallel", "parallel", "arbitrary"),
    )(qt, selb, ka, vst, ocw_t, gates_t)


def _page_specs(pps, kvw, page, layer):
    return [pl.BlockSpec((None, None, kvw, page), lambda b, j, pt, i=i: (layer, pt[b, j * pps + i], 0, 0))
            for i in range(pps)]


def _cmp_pages_body(pt_ref, *refs, pps):
    del pt_ref
    k_refs, v_refs = refs[:pps], refs[pps:2 * pps]
    wk_ref, wv_ref, kc_ref, vc_ref = refs[2 * pps:]
    j = pl.program_id(1)
    kvw, page = k_refs[0].shape
    nblk = kc_ref.shape[1]
    bpp = page // NSA_BLOCK

    @pl.when(j == 0)
    def _():
        kc_ref[...] = jnp.zeros(kc_ref.shape, F32)
        vc_ref[...] = jnp.zeros(vc_ref.shape, F32)

    row_blk = lax.broadcasted_iota(jnp.int32, (page, nblk), 0) // NSA_BLOCK
    col = lax.broadcasted_iota(jnp.int32, (page, nblk), 1)
    for page_refs, w_ref, o_ref in ((k_refs, wk_ref, kc_ref), (v_refs, wv_ref, vc_ref)):
        acc = jnp.zeros((kvw, nblk), F32)
        for i, r in enumerate(page_refs):
            y = r[...] * w_ref[...]
            hi = y.astype(BF16)
            lo = (y - hi.astype(F32)).astype(BF16)
            onehot = (col == (j * pps + i) * bpp + row_blk).astype(BF16)
            acc = acc + _dot(jnp.concatenate([hi, lo], axis=1), jnp.concatenate([onehot, onehot], axis=0))
        o_ref[...] += acc


def _cmp_pages_call(page_table, pool_k, pool_v, wk_t, wv_t, *, layer, pps):
    nbs, npg = page_table.shape
    _, _, kvw, page = pool_k.shape
    nblk = npg * (page // NSA_BLOCK)
    assert npg % pps == 0
    wspec = pl.BlockSpec((kvw, page), lambda b, j, pt: (0, 0))
    grid_spec = pltpu.PrefetchScalarGridSpec(
        num_scalar_prefetch=1, grid=(nbs, npg // pps),
        in_specs=_page_specs(pps, kvw, page, layer) * 2 + [wspec] * 2,
        out_specs=[pl.BlockSpec((None, kvw, nblk), lambda b, j, pt: (b, 0, 0))] * 2)
    return pl.pallas_call(
        functools.partial(_cmp_pages_body, pps=pps), name="sample_cmp_pages",
        grid_spec=grid_spec,
        out_shape=[jax.ShapeDtypeStruct((nbs, kvw, nblk), F32)] * 2,
        compiler_params=_cp("parallel", "arbitrary"),
    )(page_table, *([pool_k] * pps), *([pool_v] * pps), wk_t, wv_t)


def _samp_cw_body(q_ref, g_ref, kct_ref, vct_ref, wk_ref, wv_ref, kn_ref, vn_ref, *refs,
                  ts, past, rep, scale, n_alias):
    ocw_ref, selb_ref, wko_ref, wvo_ref = refs[n_alias:]
    rows = q_ref.shape[0]
    kvw, nblk = kct_ref.shape
    wb = wk_ref.shape[1]
    npad = kn_ref.shape[1]
    gt = rows // rep
    q = q_ref[...]
    ti = lax.rem(lax.broadcasted_iota(jnp.int32, (rows, 1), 0), ts)
    t = past + ti
    blk_r = lax.broadcasted_iota(jnp.int32, (rows, nblk), 1)
    p_c = _msoftmax(_dot(q, kct_ref[...].astype(BF16)) * scale, (blk_r + 1) * NSA_BLOCK - 1 <= t)
    o_c = _dot_nt(p_c.astype(BF16), vct_ref[...].astype(BF16))
    imp = p_c[0:gt]
    for r in range(1, rep):
        imp = imp + p_c[r * gt:(r + 1) * gt]
    blk = lax.broadcasted_iota(jnp.int32, (gt, nblk), 1)
    sel = _select_blocks(imp, blk, nblk, min(NSA_TOPN - 1, nblk), axis=1)
    selb_ref[...] = jnp.concatenate([jnp.where(sel, 0.0, NEG)] * rep, axis=0)
    wk, wv, kn, vn = wk_ref[...], wv_ref[...], kn_ref[...], vn_ref[...]
    kp1 = past - wb + lax.broadcasted_iota(jnp.int32, (rows, wb), 1)
    j2 = lax.broadcasted_iota(jnp.int32, (rows, npad), 1)
    ok1 = (kp1 <= t) & (kp1 > t - NSA_WINDOW) & (kp1 >= 0)
    ok2 = (j2 <= ti) & (j2 < ts)
    s1 = jnp.where(ok1, _dot(q, wk.astype(BF16)) * scale, NEG)
    s2 = jnp.where(ok2, _dot(q, kn.astype(BF16)) * scale, NEG)
    m = jnp.maximum(jnp.max(s1, axis=1, keepdims=True), jnp.max(s2, axis=1, keepdims=True))
    e1 = jnp.where(ok1, jnp.exp(s1 - m), 0.0)
    e2 = jnp.where(ok2, jnp.exp(s2 - m), 0.0)
    den = jnp.maximum(jnp.sum(e1, axis=1, keepdims=True) + jnp.sum(e2, axis=1, keepdims=True), 1e-30)
    o_w = _dot_nt((e1 / den).astype(BF16), wv.astype(BF16)) + _dot_nt((e2 / den).astype(BF16), vn.astype(BF16))
    g = g_ref[...]
    ocw_ref[...] = g[:, 0:1] * o_c + g[:, 2:3] * o_w
    lane = lax.broadcasted_iota(jnp.int32, (kvw, wb), 1)

    def advance(old, new):
        tail = jnp.concatenate([jnp.zeros((kvw, wb - npad), F32), pltpu.roll(new, npad - ts, axis=1)], axis=1)
        return jnp.where(lane >= wb - ts, tail, pltpu.roll(old, wb - ts, axis=1))

    wko_ref[...] = advance(wk, kn)
    wvo_ref[...] = advance(wv, vn)


def _samp_cw_call(q_bd, gall, kct, vct, win_k, win_v, kn_t, vn_t, stacks, *, layer, depth, ts, past, rep, dh):
    nbs, rows, kvw = q_bd.shape
    nblk = kct.shape[2]
    wb = win_k.shape[3]
    npad = kn_t.shape[2]
    assert wb >= npad
    per_b = lambda *s: pl.BlockSpec((None,) + s, lambda b: (b,) + (0,) * len(s))
    win = pl.BlockSpec((None, None, kvw, wb), lambda b: (layer, b, 0, 0))
    n_alias = len(stacks)
    body = functools.partial(_samp_cw_body, ts=ts, past=past, rep=rep, scale=dh ** -0.5, n_alias=n_alias)
    return pl.pallas_call(
        body, name="sample_cmp_win",
        grid=(nbs,),
        in_specs=[per_b(rows, kvw), per_b(rows, LANES), per_b(kvw, nblk), per_b(kvw, nblk), win, win,
                  per_b(kvw, npad), per_b(kvw, npad)] + [pl.BlockSpec(memory_space=pl.ANY)] * n_alias,
        out_specs=[per_b(rows, kvw), per_b(rows, nblk), win, win],
        out_shape=[jax.ShapeDtypeStruct((nbs, rows, kvw), F32), jax.ShapeDtypeStruct((nbs, rows, nblk), F32),
                   jax.ShapeDtypeStruct((depth, nbs, kvw, wb), F32), jax.ShapeDtypeStruct((depth, nbs, kvw, wb), F32)],
        input_output_aliases={8 + k: 2 + k for k in range(n_alias)},
        compiler_params=_cp("parallel"),
    )(q_bd, gall, kct, vct, win_k, win_v, kn_t, vn_t, *stacks)


def _samp_sel_body(pt_ref, *refs, pps, ts, scale):
    del pt_ref
    k_refs, v_refs = refs[:pps], refs[pps:2 * pps]
    q_ref, selb_ref, kn_ref, vn_ref, ocw_ref, g_ref, o_ref, m_sc, l_sc, acc_sc = refs[2 * pps:]
    j = pl.program_id(1)
    rows = q_ref.shape[0]
    nblk = selb_ref.shape[1]
    page = k_refs[0].shape[1]
    kt = pps * page
    q = q_ref[...]

    @pl.when(j == 0)
    def _():
        m_sc[...] = jnp.full(m_sc.shape, M_INIT, F32)
        l_sc[...] = jnp.zeros(l_sc.shape, F32)
        acc_sc[...] = jnp.zeros(acc_sc.shape, F32)

    def update(s, v_t):
        m_old = m_sc[...]
        m_new = jnp.maximum(m_old, jnp.max(s, axis=1, keepdims=True))
        alpha = jnp.exp(m_old - m_new)
        p = jnp.exp(s - m_new)
        l_sc[...] = alpha * l_sc[...] + jnp.sum(p, axis=1, keepdims=True)
        acc_sc[...] = alpha * acc_sc[...] + _dot_nt(p.astype(BF16), v_t)
        m_sc[...] = m_new

    k_t = jnp.concatenate([r[...] for r in k_refs], axis=1).astype(BF16)
    v_t = jnp.concatenate([r[...] for r in v_refs], axis=1).astype(BF16)
    key_blk = j * (kt // NSA_BLOCK) + lax.broadcasted_iota(jnp.int32, (nblk, kt), 1) // NSA_BLOCK
    onehot = (key_blk == lax.broadcasted_iota(jnp.int32, (nblk, kt), 0)).astype(BF16)
    update(_dot(q, k_t) * scale + _dot(selb_ref[...].astype(BF16), onehot), v_t)

    @pl.when(j == pl.num_programs(1) - 1)
    def _():
        npad = kn_ref.shape[1]
        ti = lax.rem(lax.broadcasted_iota(jnp.int32, (rows, npad), 0), ts)
        j2 = lax.broadcasted_iota(jnp.int32, (rows, npad), 1)
        s = jnp.where((j2 <= ti) & (j2 < ts), _dot(q, kn_ref[...].astype(BF16)) * scale, NEG)
        update(s, vn_ref[...].astype(BF16))
        o_ref[...] = ocw_ref[...] + g_ref[:, 1:2] * (acc_sc[...] / l_sc[...])


def _samp_sel_call(page_table, pool_k, pool_v, q_bd, selb, kn_t, vn_t, ocw, gall, *, layer, pps, ts, dh):
    nbs, npg = page_table.shape
    _, _, kvw, page = pool_k.shape
    rows = q_bd.shape[1]
    per_b = lambda *s: pl.BlockSpec((None,) + s, lambda b, j, pt: (b,) + (0,) * len(s))
    npad = kn_t.shape[2]
    grid_spec = pltpu.PrefetchScalarGridSpec(
        num_scalar_prefetch=1, grid=(nbs, npg // pps),
        in_specs=_page_specs(pps, kvw, page, layer) * 2 + [
            per_b(rows, kvw), per_b(rows, selb.shape[2]), per_b(kvw, npad), per_b(kvw, npad),
            per_b(rows, kvw), per_b(rows, LANES)],
        out_specs=per_b(rows, kvw),
        scratch_shapes=[pltpu.VMEM((rows, 1), F32), pltpu.VMEM((rows, 1), F32), pltpu.VMEM((rows, kvw), F32)])
    body = functools.partial(_samp_sel_body, pps=pps, ts=ts, scale=dh ** -0.5)
    return pl.pallas_call(
        body, name="sample_selected",
        grid_spec=grid_spec,
        out_shape=jax.ShapeDtypeStruct((nbs, rows, kvw), F32),
        compiler_params=_cp("parallel", "arbitrary"),
    )(page_table, *([pool_k] * pps), *([pool_v] * pps), q_bd, selb, kn_t, vn_t, ocw, gall)


def _norm_body(x_ref, g_ref, o_ref):
    o_ref[...] = _rms(x_ref[...], g_ref[...]).astype(o_ref.dtype)


def _norm_call(x, g, *, tm):
    t, dm = x.shape
    row = pl.BlockSpec((tm, dm), lambda i: (i, 0))
    return pl.pallas_call(
        _norm_body, name="norm", grid=(t // tm,), in_specs=[row, _full(g.shape)], out_specs=row,
        out_shape=jax.ShapeDtypeStruct((t, dm), BF16), compiler_params=_cp("parallel"),
    )(x, g)


def _merge_body(x_ref, xn_ref, wmg_ref, oa_ref, ob_ref, oc_ref, wb_ref, wo_ref, gn_ref, o_ref, on_ref):
    x = x_ref[...]
    dm = x.shape[1]
    mg = jax.nn.sigmoid(_dot(xn_ref[...], wmg_ref[...]))
    y = mg[:, 0:dm] * _dot(oa_ref[...], wb_ref[0])
    y = y + mg[:, dm:2 * dm] * _dot(ob_ref[...], wb_ref[1])
    y = y + mg[:, 2 * dm:3 * dm] * _dot(oc_ref[...], wb_ref[2])
    x_new = x + _dot(y.astype(BF16), wo_ref[...])
    o_ref[...] = x_new
    on_ref[...] = _rms(x_new, gn_ref[...]).astype(BF16)


def _merge_call(x, xn, wmg, oa, ob, oc, wb, wo, g_next, *, layer, tm):
    t, dm = x.shape
    row = pl.BlockSpec((tm, dm), lambda i: (i, 0))
    return pl.pallas_call(
        _merge_body, name="merge",
        grid=(t // tm,),
        in_specs=[row, row, _at_layer(wmg, layer), row, row, row, _at_layer(wb, layer),
                  _at_layer(wo, layer), _full(g_next.shape)],
        out_specs=[row, row],
        out_shape=[jax.ShapeDtypeStruct((t, dm), F32), jax.ShapeDtypeStruct((t, dm), BF16)],
        compiler_params=_cp("parallel"),
    )(x, xn, wmg, oa, ob, oc, wb, wo, g_next)


def _mixer_outputs(y, gn_ref, o_ref, on_ref, final):
    yn = _rms(y, gn_ref[...])
    o_ref[...] = yn if final else y
    on_ref[...] = yn.astype(BF16)


def _ffn_body(x_ref, xn_ref, w1_ref, w3_ref, w2_ref, gn_ref, o_ref, on_ref, acc_sc, *, final):
    f = pl.program_id(1)

    @pl.when(f == 0)
    def _():
        acc_sc[...] = jnp.zeros(acc_sc.shape, F32)

    xn = xn_ref[...]
    h = jax.nn.silu(_dot(xn, w1_ref[...])) * _dot(xn, w3_ref[...])
    acc_sc[...] += _dot(h.astype(BF16), w2_ref[...])

    @pl.when(f == pl.num_programs(1) - 1)
    def _():
        _mixer_outputs(x_ref[...] + acc_sc[...], gn_ref, o_ref, on_ref, final)


def _ffn_tile(f_dim):
    half = f_dim // 2
    return half if f_dim % 2 == 0 and half % LANES == 0 else f_dim


def _ffn_call(x, xn, w1, w3, w2, g_next, *, layer, tm, final):
    t, dm = x.shape
    fd = w1.shape[-1]
    ft = _ffn_tile(fd)
    row = pl.BlockSpec((tm, dm), lambda i, f: (i, 0))
    vec = pl.BlockSpec((1, dm), lambda i, f: (0, 0))
    w_in_spec = pl.BlockSpec((None, dm, ft), lambda i, f: (layer, 0, f))
    return pl.pallas_call(
        functools.partial(_ffn_body, final=final), name="ffn",
        grid=(t // tm, fd // ft),
        in_specs=[row, row, w_in_spec, w_in_spec, pl.BlockSpec((None, ft, dm), lambda i, f: (layer, f, 0)), vec],
        out_specs=[row, row],
        out_shape=[jax.ShapeDtypeStruct((t, dm), F32), jax.ShapeDtypeStruct((t, dm), BF16)],
        scratch_shapes=[pltpu.VMEM((tm, dm), F32)],
        compiler_params=_cp("parallel", "arbitrary"),
    )(x, xn, w1, w3, w2, g_next)


def _route_body(x_ref, g_ref, wr_ref, br_ref, xn_ref, comb_ref, key_ref, keyt_ref, cnt_ref):
    tm = x_ref.shape[0]
    xn = _rms(x_ref[...], g_ref[...])
    xn_ref[...] = xn.astype(BF16)
    logits = jnp.dot(xn, wr_ref[...], preferred_element_type=F32, precision=lax.Precision.HIGHEST) + br_ref[...]
    lane = lax.broadcasted_iota(jnp.int32, logits.shape, 1)
    nl = logits.shape[1]
    m1 = jnp.max(logits, axis=1, keepdims=True)
    i1 = jnp.min(jnp.where(logits == m1, lane, nl), axis=1, keepdims=True)
    rest = jnp.where(lane == i1, M_INIT, logits)
    m2 = jnp.max(rest, axis=1, keepdims=True)
    i2 = jnp.min(jnp.where(rest == m2, lane, nl), axis=1, keepdims=True)
    e2 = jnp.exp(m2 - m1)
    den = 1.0 + e2
    comb_ref[...] = jnp.where(lane == i1, 1.0 / den, 0.0) + jnp.where(lane == i2, e2 / den, 0.0)
    member = (lane == i1) | (lane == i2)
    earlier = lax.broadcasted_iota(jnp.int32, (tm, tm), 1) < lax.broadcasted_iota(jnp.int32, (tm, tm), 0)
    rank = _dot(earlier.astype(BF16), member.astype(BF16))
    key = jnp.where(member, rank, -1.0)
    key_ref[...] = key
    keyt_ref[...] = key.T
    count = jnp.sum(member.astype(F32), axis=0, keepdims=True).astype(jnp.int32)
    cnt_ref[...] = jnp.broadcast_to(count, cnt_ref.shape)


def _route_call(x, g, wr, br, *, tm):
    t, dm = x.shape
    nl = wr.shape[1]
    nt = t // tm
    row = lambda n: pl.BlockSpec((tm, n), lambda i: (i, 0))
    return pl.pallas_call(
        _route_body, name="moe_route",
        grid=(nt,),
        in_specs=[row(dm), _full(g.shape), _full(wr.shape), _full(br.shape)],
        out_specs=[row(dm), row(nl), row(nl), pl.BlockSpec((None, nl, tm), lambda i: (i, 0, 0)),
                   pl.BlockSpec((None, 8, nl), lambda i: (i, 0, 0))],
        out_shape=[jax.ShapeDtypeStruct((t, dm), BF16), jax.ShapeDtypeStruct((t, nl), F32),
                   jax.ShapeDtypeStruct((t, nl), F32), jax.ShapeDtypeStruct((nt, nl, tm), F32),
                   jax.ShapeDtypeStruct((nt, 8, nl), jnp.int32)],
        compiler_params=_cp("parallel"),
    )(x, g, wr, br)


def _moe_body(cnt_ref, x_ref, xn_ref, comb_ref, key_ref, keyt_ref, w1_ref, w3_ref, w2_ref, gn_ref, o_ref, on_ref,
              xe_sc, ye_sc, acc_sc, *, cap, final):
    i = pl.program_id(0)
    e = pl.program_id(1)
    f = pl.program_id(2)
    tm = x_ref.shape[0]
    n_batches = (cnt_ref[i, e] + cap - 1) // cap
    rows_of = lambda b: pl.ds(pl.multiple_of(b * cap, cap), cap)

    @pl.when((e == 0) & (f == 0))
    def _():
        acc_sc[...] = jnp.zeros(acc_sc.shape, F32)

    @pl.when(f == 0)
    def _():
        key_row = keyt_ref[pl.ds(e, 1), :]

        def gather(b, carry):
            slot = (b * cap + lax.broadcasted_iota(jnp.int32, (cap, tm), 0)).astype(F32)
            xe_sc[rows_of(b), :] = _dot((key_row == slot).astype(BF16), xn_ref[...]).astype(BF16)
            return carry

        lax.fori_loop(0, n_batches, gather, 0)

    def expert(b, carry):
        xe = xe_sc[rows_of(b), :]
        h = jax.nn.silu(_dot(xe, w1_ref[...])) * _dot(xe, w3_ref[...])
        part = _dot(h.astype(BF16), w2_ref[...])

        @pl.when(f == 0)
        def _():
            ye_sc[rows_of(b), :] = part

        @pl.when(f != 0)
        def _():
            ye_sc[rows_of(b), :] += part

        return carry

    lax.fori_loop(0, n_batches, expert, 0)

    @pl.when(f == pl.num_programs(2) - 1)
    def _():
        lane = lax.broadcasted_iota(jnp.int32, key_ref.shape, 1)
        key_col = jnp.sum(jnp.where(lane == e, key_ref[...], 0.0), axis=1, keepdims=True)
        gate = jnp.sum(jnp.where(lane == e, comb_ref[...], 0.0), axis=1, keepdims=True)

        def scatter(b, carry):
            slot = (b * cap + lax.broadcasted_iota(jnp.int32, (tm, cap), 1)).astype(F32)
            onehot = (key_col == slot).astype(BF16)
            ye = ye_sc[rows_of(b), :]
            hi = ye.astype(BF16)
            lo = (ye - hi.astype(F32)).astype(BF16)
            back = _dot(jnp.concatenate([onehot, onehot], axis=1), jnp.concatenate([hi, lo], axis=0))
            acc_sc[...] += gate * back
            return carry

        lax.fori_loop(0, n_batches, scatter, 0)

    @pl.when((e == pl.num_programs(1) - 1) & (f == pl.num_programs(2) - 1))
    def _():
        _mixer_outputs(x_ref[...] + acc_sc[...], gn_ref, o_ref, on_ref, final)


def _moe_call(x, g, wr, br, w1, w3, w2, g_next, *, layer, tm, final):
    t, dm = x.shape
    _, ne, _, fd = w1.shape
    nl = wr.shape[1]
    ft = _ffn_tile(fd)
    cap = min(MOE_BATCH, tm)
    assert cap % BF16_ROWS == 0
    max_rows = -(-tm // cap) * cap
    xn, comb, key, key_t, counts = _route_call(x, g, wr, br, tm=tm)
    once = dict(pipeline_mode=pl.Buffered(1))
    row = lambda n: pl.BlockSpec((tm, n), lambda i, e, f, cnt: (i, 0), **once)
    grid_spec = pltpu.PrefetchScalarGridSpec(
        num_scalar_prefetch=1, grid=(t // tm, ne, fd // ft),
        in_specs=[row(dm), row(dm), row(nl), row(nl),
                  pl.BlockSpec((None, nl, tm), lambda i, e, f, cnt: (i, 0, 0), **once),
                  pl.BlockSpec((None, None, dm, ft), lambda i, e, f, cnt: (layer, e, 0, f)),
                  pl.BlockSpec((None, None, dm, ft), lambda i, e, f, cnt: (layer, e, 0, f)),
                  pl.BlockSpec((None, None, ft, dm), lambda i, e, f, cnt: (layer, e, f, 0)),
                  pl.BlockSpec((1, dm), lambda i, e, f, cnt: (0, 0))],
        out_specs=[pl.BlockSpec((tm, dm), lambda i, e, f, cnt: (i, 0))] * 2,
        scratch_shapes=[pltpu.VMEM((max_rows, dm), BF16), pltpu.VMEM((max_rows, dm), F32), pltpu.VMEM((tm, dm), F32)])
    return pl.pallas_call(
        functools.partial(_moe_body, cap=cap, final=final), name="moe",
        grid_spec=grid_spec,
        out_shape=[jax.ShapeDtypeStruct((t, dm), F32), jax.ShapeDtypeStruct((t, dm), BF16)],
        compiler_params=_cp("parallel", "arbitrary", "arbitrary"),
    )(counts[:, 0, :ne], x, xn, comb, key, key_t, w1, w3, w2, g_next)


def kernel(x_prompt, x_sample, cache_cmp_k, cache_cmp_v, cache_slc_k, cache_slc_v, cache_win_k, cache_win_v, state_conv, state_lru, page_table, norm_mix_g, w_in, lru_conv_w, lru_conv_b, lru_wa, lru_ba, lru_wx, lru_bx, lru_lambda, nsa_cmp_wk, nsa_cmp_wv, gm_norm_g, gm_ws, gm_bs, w_branch, w_out, norm_ffn_g, ffn_w1, ffn_w3, ffn_w2, moe_wr, moe_br, moe_w1, moe_w3, moe_w2, final_norm_g):
    depth, dm, _ = w_in.shape
    nbp, seq, _ = x_prompt.shape
    nbs, ts, _ = x_sample.shape
    _, n_pool, page, ng, dh = cache_cmp_k.shape
    wd = state_lru.shape[-1]
    conv_w = lru_conv_w.shape[1]
    gmw = gm_norm_g.shape[-1]
    n_gm, chunk = gm_ws.shape[1], gm_ws.shape[2]
    kvw = ng * dh
    qw = NSA_HEADS * dh
    rep = NSA_HEADS // ng
    npg = page_table.shape[1]
    past = npg * page
    wbuf = cache_win_k.shape[2]
    tp, tsm = nbp * seq, nbs * ts
    ne = moe_wr.shape[-1]
    nblk = seq // NSA_BLOCK
    scale = dh ** -0.5
    s_scale = 1.0 if _is_power_of_two(scale) else scale

    assert seq % chunk == 0 and seq % LRU_CHUNK == 0 and seq >= NSA_WINDOW
    assert past % NSA_BLOCK == 0 and ts < NSA_BLOCK and ts < chunk and wbuf == NSA_WINDOW and nbs % 8 == 0
    assert past // NSA_BLOCK >= NSA_TOPN - 1 and page % LANES == 0

    offs = [0, wd, 2 * wd, 2 * wd + qw, 2 * wd + qw + 6 * kvw, 2 * wd + qw + 6 * kvw + 3 * NSA_HEADS]
    offs += [offs[-1] + 2 * gmw, offs[-1] + 2 * gmw + 3 * dm]
    w_part = lambda lo, hi: w_in[:, :, lo:hi].astype(BF16)
    w_lru = w_part(0, offs[2])
    w_nsa = jnp.pad(w_part(offs[2], offs[5]), ((0, 0), (0, 0), (0, LANES - 3 * NSA_HEADS)))
    w_uv = w_part(offs[5], offs[6])
    w_mg = w_part(offs[6], offs[7])
    wa_b, wx_b = lru_wa.astype(BF16), lru_wx.astype(BF16)
    wb_b, wo_b = w_branch.astype(BF16), w_out.astype(BF16)
    f1_b, f3_b, f2_b = ffn_w1.astype(BF16), ffn_w3.astype(BF16), ffn_w2.astype(BF16)
    m1_b, m3_b, m2_b = moe_w1.astype(BF16), moe_w3.astype(BF16), moe_w2.astype(BF16)
    wr_pad = jnp.pad(moe_wr, ((0, 0), (0, 0), (0, LANES - ne)))
    br_pad = jnp.pad(moe_br, ((0, 0), (0, LANES - ne)), constant_values=NEG)[:, None, :]
    wck = nsa_cmp_wk.reshape(depth, NSA_BLOCK, kvw)
    wcv = nsa_cmp_wv.reshape(depth, NSA_BLOCK, kvw)
    cmp_w_t = lambda w: jnp.tile(jnp.transpose(w, (0, 2, 3, 1)).reshape(depth, kvw, NSA_BLOCK), (1, 1, page // NSA_BLOCK))
    wck_t, wcv_t = cmp_w_t(nsa_cmp_wk), cmp_w_t(nsa_cmp_wv)
    gc = gmw // n_gm
    bs_full = jnp.repeat(jnp.swapaxes(gm_bs, 1, 2), gc, axis=2)
    ws_short = jnp.repeat(jnp.transpose(gm_ws[:, :, :ts, :ts], (0, 2, 3, 1)).reshape(depth, ts * ts, n_gm), gc, axis=2)
    vec = lambda a, l: a[l][None, :]
    pps = min(PAGES_PER_STEP, npg)
    feature_major = lambda a: jnp.transpose(a, (0, 1, 3, 4, 2)).reshape(a.shape[0], a.shape[1], kvw, a.shape[2])
    pool_ck, pool_cv = feature_major(cache_cmp_k), feature_major(cache_cmp_v)
    pool_sk, pool_sv = feature_major(cache_slc_k), feature_major(cache_slc_v)
    win_k, win_v = feature_major(cache_win_k), feature_major(cache_win_v)
    eye_g = jnp.eye(ng, dtype=BF16)

    tm_p = min(ROW_TILE, tp)
    xp = x_prompt.reshape(tp, dm)
    xs = x_sample.reshape(tsm, dm)
    zeros_cs = jnp.zeros((nbp, conv_w - 1, wd), F32)
    zeros_h = jnp.zeros((nbp, 1, wd), F32)
    to_tm = lambda a, n: jnp.swapaxes(a.reshape(nbs, n, -1), 0, 1).reshape(1, n * nbs, -1)
    from_tm = lambda a, n: jnp.swapaxes(a.reshape(n, nbs, -1), 0, 1)
    new_rows_t = lambda a: jnp.pad(jnp.swapaxes(a.reshape(nbs, ts, kvw), 1, 2), ((0, 0), (0, 0), (0, LANES - ts)))

    p_stacks, w_stacks = (), ()
    p_small = [[] for _ in range(2)]
    s_out = [[] for _ in range(7)]
    xpn = _norm_call(xp, vec(norm_mix_g, 0), tm=tm_p)
    xsn = _norm_call(xs, vec(norm_mix_g, 0), tm=tsm)
    for l in range(depth):
        last = l == depth - 1
        g_ffn = vec(norm_ffn_g, l)
        g_next = final_norm_g[None, :] if last else vec(norm_mix_g, l + 1)
        lru_w = (lru_conv_w[l], vec(lru_conv_b, l), wa_b, vec(lru_ba, l), wx_b, vec(lru_bx, l), vec(lru_lambda, l))

        o_a, conv_new, h_last = _lru_call(xpn.reshape(nbp, seq, dm), w_lru, zeros_cs, zeros_h, *lru_w,
                                          layer=l, nb=1, tch=min(LRU_CHUNK, seq), starts_at_zero=True)
        o_c = _gmlp_call(xpn, w_uv, vec(gm_norm_g, l), gm_ws, bs_full[l], layer=l, tm=tm_p)
        qt, gates_t, kcmp, vcmp, ka, vst, kwg, vwt, *p_stacks = _qkv_prompt_call(
            xpn, w_nsa, wck[l], wcv[l], p_stacks, layer=l, depth=depth, nbatch=nbp, seq=seq, tm=tm_p,
            qw=qw, kvw=kvw, dh=dh)
        vcmp_t = jnp.swapaxes(vcmp.reshape(nbp, nblk, kvw), 1, 2)
        ocw_t, selb = _nsa_cw_call(qt, gates_t, kcmp, vcmp_t, kwg, vwt, nbatch=nbp, seq=seq, ng=ng, dh=dh,
                                   s_scale=s_scale)
        o_b = _nsa_sel_call(qt, selb, ka, vst, ocw_t, gates_t, nbatch=nbp, seq=seq, ng=ng, dh=dh, s_scale=s_scale)
        xp, xpn = _merge_call(xp, xpn, w_mg, o_a.reshape(tp, wd), o_b, o_c, wb_b, wo_b, g_ffn, layer=l, tm=tm_p)
        if l % 2 == 0:
            xp, xpn = _ffn_call(xp, xpn, f1_b, f3_b, f2_b, g_next, layer=l // 2, tm=tm_p, final=last)
        else:
            xp, xpn = _moe_call(xp, g_ffn, wr_pad[l // 2], br_pad[l // 2], m1_b, m3_b, m2_b, g_next,
                                layer=l // 2, tm=min(MOE_TILE, tp), final=last)
        p_small[0].append(conv_new)
        p_small[1].append(h_last.reshape(nbp, wd))

        xsn_tm = to_tm(xsn, ts)
        o_a, conv_new, h_last = _lru_call(xsn_tm, w_lru, to_tm(state_conv[l], conv_w - 1),
                                          state_lru[l][None], *lru_w, layer=l, nb=nbs, tch=ts, starts_at_zero=False)
        o_a = from_tm(o_a, ts).reshape(tsm, wd)
        o_c, gm_v = _gmlp_short_call(xsn_tm[0], w_uv, vec(gm_norm_g, l), ws_short[l], bs_full[l][:ts],
                                     layer=l, n=ts, nb=nbs)
        o_c = from_tm(o_c, ts).reshape(tsm, gmw)
        q, k_c, v_c, k_s, v_s, k_w, v_w, gates = _qkv_sample_call(xsn, w_nsa, layer=l, qw=qw, kvw=kvw, dh=dh)
        q5 = jnp.transpose(q.reshape(nbs, ts, ng, rep, dh), (0, 3, 2, 1, 4))
        q_bd = (q5[:, :, :, :, None, :] * eye_g[None, None, :, None, :, None]).reshape(nbs, rep * ng * ts, kvw)
        g5 = jnp.transpose(gates[:, :3 * NSA_HEADS].reshape(nbs, ts, ng, rep, 3), (0, 3, 2, 1, 4))
        gall = jnp.pad(g5.reshape(nbs, rep * ng * ts, 3), ((0, 0), (0, 0), (0, LANES - 3)))
        kct, vct = _cmp_pages_call(page_table, pool_ck, pool_cv, wck_t[l], wcv_t[l], layer=l, pps=pps)
        ocw, selb, *w_stacks = _samp_cw_call(
            q_bd, gall, kct, vct, win_k, win_v, new_rows_t(k_w), new_rows_t(v_w), w_stacks,
            layer=l, depth=depth, ts=ts, past=past, rep=rep, dh=dh)
        o_all = _samp_sel_call(page_table, pool_sk, pool_sv, q_bd, selb, new_rows_t(k_s), new_rows_t(v_s),
                               ocw, gall, layer=l, pps=pps, ts=ts, dh=dh)
        o6 = o_all.reshape(nbs, rep, ng, ts, ng, dh)
        o_b = jnp.stack([o6[:, :, gi, :, gi, :] for gi in range(ng)], axis=2)
        o_b = jnp.transpose(o_b, (0, 3, 2, 1, 4)).reshape(tsm, qw).astype(BF16)
        xs, xsn = _merge_call(xs, xsn, w_mg, o_a, o_b, o_c, wb_b, wo_b, g_ffn, layer=l, tm=tsm)
        if l % 2 == 0:
            xs, xsn = _ffn_call(xs, xsn, f1_b, f3_b, f2_b, g_next, layer=l // 2, tm=tsm, final=last)
        else:
            xs, xsn = _moe_call(xs, g_ffn, wr_pad[l // 2], br_pad[l // 2], m1_b, m3_b, m2_b, g_next,
                                layer=l // 2, tm=tsm, final=last)
        shs = (nbs, ts, ng, dh)
        for lst, a in zip(s_out, (k_c.reshape(shs), v_c.reshape(shs), k_s.reshape(shs), v_s.reshape(shs),
                                  from_tm(conv_new[0], conv_w - 1), h_last[0], from_tm(gm_v, ts))):
            lst.append(a)

    row_major = lambda a: jnp.transpose(a.reshape(a.shape[0], a.shape[1], ng, dh, a.shape[3]), (0, 1, 4, 2, 3))
    keep = min(NSA_WINDOW, seq)
    kc_st, vc_st, ks_st, vs_st, kw_st, vw_st = p_stacks
    s_small = [jnp.stack(a) for a in s_out]
    return (xp.reshape(nbp, seq, dm), xs.reshape(nbs, ts, dm),
            row_major(kc_st), row_major(vc_st), row_major(ks_st), row_major(vs_st),
            row_major(kw_st[..., seq - keep:]), row_major(vw_st[..., seq - keep:]),
            jnp.stack(p_small[0]), jnp.stack(p_small[1]),
            s_small[0], s_small[1], s_small[2], s_small[3],
            row_major(w_stacks[0]), row_major(w_stacks[1]),
            s_small[4], s_small[5], s_small[6])
```
